```python
import math
import jax, jax.numpy as jnp
from jax import lax
import numpy as np

D_MODEL = 2048
BATCH = 1
SEQ = 8192
DEPTH = 4

MLA_HEADS = 8
MLA_NOPE_DIM = 128
MLA_ROPE_DIM = 64
MLA_V_DIM = 128
MLA_QK_DIM = MLA_NOPE_DIM + MLA_ROPE_DIM
MLA_Q_RANK = 512
MLA_KV_RANK = 512
ROPE_THETA = 10000.0
ATTN_BLOCK = 128
GDN_HEADS = 8
GDN_HEAD_DIM = 128
GDN_DIM = GDN_HEADS * GDN_HEAD_DIM
GDN_CONV = 4
GDN_CHUNK = 64
GLA_HEADS = 4
GLA_KEY_DIM = D_MODEL // 2
GLA_VALUE_DIM = D_MODEL
GLA_HEAD_K = GLA_KEY_DIM // GLA_HEADS
GLA_HEAD_V = GLA_VALUE_DIM // GLA_HEADS
GLA_GATE_RANK = 16
GLA_GATE_NORMALIZER = 16.0
GLA_CHUNK = 64
D_FF = -(-8 * D_MODEL // (3 * 256)) * 256
MLA_IN = MLA_Q_RANK + MLA_KV_RANK + MLA_ROPE_DIM
GDN_IN = 4 * GDN_DIM + 2 * GDN_HEADS
EVEN_IN = MLA_IN + GDN_IN
EVEN_MIX = MLA_HEADS * MLA_V_DIM + GDN_DIM
GLA_IN = 2 * GLA_KEY_DIM + 2 * GLA_VALUE_DIM + GLA_GATE_RANK
N_EVEN = (DEPTH + 1) // 2
N_ODD = DEPTH // 2
EPS = 1e-6

kernel_name = "hybrid_mla_gdn_gla_adaln_trunk"


def rms_norm(x, gain):
    xf = x.astype(jnp.float32)
    y = xf * lax.rsqrt(jnp.mean(xf * xf, axis=-1, keepdims=True) + EPS)
    return (y * gain.astype(jnp.float32)).astype(x.dtype)


def l2_norm(x):
    xf = x.astype(jnp.float32)
    return xf * lax.rsqrt(jnp.sum(xf * xf, axis=-1, keepdims=True) + EPS)


def split_cols(t, sizes):
    return jnp.split(t, np.cumsum(sizes)[:-1].tolist(), axis=-1)


def ada_modulation(c, w, b):
    m = (jax.nn.silu(c) @ w + b)[:, None, :]
    return jnp.split(m, 3, axis=-1)


def rope_tables(positions):
    half = MLA_ROPE_DIM // 2
    inv_freq = ROPE_THETA ** (-jnp.arange(half, dtype=jnp.float32) / half)
    ang = positions.astype(jnp.float32)[..., None] * inv_freq
    return jnp.cos(ang), jnp.sin(ang)


def apply_rope(x, cos, sin):
    x1, x2 = jnp.split(x.astype(jnp.float32), 2, axis=-1)
    return jnp.concatenate([x1 * cos - x2 * sin, x2 * cos + x1 * sin], axis=-1)


def causal_depthwise_conv(x, w):
    k_width, s = w.shape[0], x.shape[1]
    xp = jnp.pad(x, ((0, 0), (k_width - 1, 0), (0, 0)))
    y = xp[:, 0:s] * w[0]
    for j in range(1, k_width):
        y = y + xp[:, j:j + s] * w[j]
    return y


def causal_block_attention(q, k, v):
    b, h, s, dqk = q.shape
    nb = s // ATTN_BLOCK
    scale = 1.0 / math.sqrt(dqk)
    qb = jnp.moveaxis(q.reshape(b, h, nb, ATTN_BLOCK, dqk), 2, 0)
    kpos = jnp.arange(s)

    def one_block(args):
        q_blk, i = args
        qpos = i * ATTN_BLOCK + jnp.arange(ATTN_BLOCK)
        sc = jnp.einsum('bhqd,bhkd->bhqk', q_blk, k) * scale
        sc = jnp.where(kpos[None, :] <= qpos[:, None], sc, -jnp.inf)
        p = jax.nn.softmax(sc, axis=-1)
        return jnp.einsum('bhqk,bhkd->bhqd', p, v)

    o = lax.map(one_block, (qb, jnp.arange(nb)))
    return jnp.moveaxis(o, 0, 2).reshape(b, h, s, v.shape[-1])


def mla_branch(c_q, c_kv, k_rope, cos, sin, q_norm, w_uq, kv_norm, w_ukv):
    b, s, _ = c_q.shape
    q = (rms_norm(c_q, q_norm) @ w_uq).reshape(b, s, MLA_HEADS, MLA_QK_DIM)
    kv = (rms_norm(c_kv, kv_norm) @ w_ukv).reshape(b, s, MLA_HEADS, MLA_NOPE_DIM + MLA_V_DIM)
    q_nope, q_rope = q[..., :MLA_NOPE_DIM], q[..., MLA_NOPE_DIM:]
    k_nope, v = kv[..., :MLA_NOPE_DIM], kv[..., MLA_NOPE_DIM:]
    q_rope = apply_rope(q_rope, cos[:, :, None, :], sin[:, :, None, :])
    k_rope = apply_rope(k_rope, cos, sin)
    q = jnp.concatenate([q_nope.astype(jnp.float32), q_rope], axis=-1)
    k = jnp.concatenate([k_nope.astype(jnp.float32),
                         jnp.broadcast_to(k_rope[:, :, None, :], (b, s, MLA_HEADS, MLA_ROPE_DIM))], axis=-1)
    o = causal_block_attention(q.transpose(0, 2, 1, 3), k.transpose(0, 2, 1, 3),
                               v.astype(jnp.float32).transpose(0, 2, 1, 3))
    return o.transpose(0, 2, 1, 3).reshape(b, s, MLA_HEADS * MLA_V_DIM)


def gated_delta_rule_chunked(q, k, v, g, beta):
    b, h, s, dk = q.shape
    dv = v.shape[-1]
    cs, n = GDN_CHUNK, s // GDN_CHUNK
    q = q * (1.0 / math.sqrt(dk))
    q, k, v = (t.reshape(b, h, n, cs, t.shape[-1]) for t in (q, k, v))
    g, beta = g.reshape(b, h, n, cs), beta.reshape(b, h, n, cs)
    gc = jnp.cumsum(g, axis=-1)
    incl = jnp.tril(jnp.ones((cs, cs), dtype=bool))
    strict = jnp.tril(jnp.ones((cs, cs), dtype=bool), -1)
    decay = jnp.exp(jnp.where(incl, gc[..., :, None] - gc[..., None, :], -jnp.inf))
    k_beta = k * beta[..., None]
    lower = jnp.where(strict, jnp.einsum('bhnid,bhnjd->bhnij', k_beta, k) * decay, 0.0)
    t_mat = jnp.eye(cs, dtype=q.dtype) + lower
    rhs = jnp.concatenate([v * beta[..., None], k_beta * jnp.exp(gc)[..., None]], axis=-1)
    sol = lax.linalg.triangular_solve(t_mat, rhs, left_side=True, lower=True, unit_diagonal=True)
    u, w = sol[..., :dv], sol[..., dv:]
    intra = jnp.einsum('bhnid,bhnjd->bhnij', q, k) * decay
    g_last = gc[..., -1]
    k_dec = k * jnp.exp(g_last[..., None] - gc)[..., None]
    q_dec = q * jnp.exp(gc)[..., None]

    def step(state, inp):
        q_d, k_d, u_c, w_c, a_c, gl = inp
        v_new = u_c - jnp.einsum('bhcd,bhde->bhce', w_c, state)
        o = jnp.einsum('bhcd,bhde->bhce', q_d, state) + jnp.einsum('bhij,bhje->bhie', a_c, v_new)
        state = state * jnp.exp(gl)[..., None, None] + jnp.einsum('bhcd,bhce->bhde', k_d, v_new)
        return state, o

    xs = tuple(jnp.moveaxis(t, 2, 0) for t in (q_dec, k_dec, u, w, intra, g_last))
    _, o = lax.scan(step, jnp.zeros((b, h, dk, dv), q.dtype), xs)
    return jnp.moveaxis(o, 0, 2).reshape(b, h, s, dv)


def gdn_branch(qkv, z, b_logit, a_logit, conv_w, a_log, dt_bias, norm_g):
    bsz, s, _ = qkv.shape
    qkv = jax.nn.silu(causal_depthwise_conv(qkv, conv_w))
    q, k, v = jnp.split(qkv, 3, axis=-1)
    heads = lambda t: t.reshape(bsz, s, GDN_HEADS, GDN_HEAD_DIM)
    q, k = l2_norm(heads(q)), l2_norm(heads(k))
    v = heads(v).astype(jnp.float32)
    beta = jax.nn.sigmoid(b_logit.astype(jnp.float32))
    g = -jnp.exp(a_log.astype(jnp.float32)) * jax.nn.softplus(a_logit.astype(jnp.float32) + dt_bias)
    o = gated_delta_rule_chunked(q.transpose(0, 2, 1, 3), k.transpose(0, 2, 1, 3), v.transpose(0, 2, 1, 3),
                                 g.transpose(0, 2, 1), beta.transpose(0, 2, 1))
    o = rms_norm(o.transpose(0, 2, 1, 3), norm_g) * jax.nn.silu(heads(z).astype(jnp.float32))
    return o.reshape(bsz, s, GDN_DIM)


def ab_mixer(h, cos, sin, w_in, q_norm, w_uq, kv_norm, w_ukv, conv_w, a_log, dt_bias, gdn_norm, w_out):
    proj = h @ w_in
    c_q, c_kv, k_rope, qkv, z, b_logit, a_logit = split_cols(
        proj, [MLA_Q_RANK, MLA_KV_RANK, MLA_ROPE_DIM, 3 * GDN_DIM, GDN_DIM, GDN_HEADS, GDN_HEADS])
    o_a = mla_branch(c_q, c_kv, k_rope, cos, sin, q_norm, w_uq, kv_norm, w_ukv)
    o_b = gdn_branch(qkv, z, b_logit, a_logit, conv_w, a_log, dt_bias, gdn_norm)
    return jnp.concatenate([o_a, o_b], axis=-1).astype(h.dtype) @ w_out


def gla_chunked(q, k, v, gk):
    b, h, s, dk = q.shape
    dv = v.shape[-1]
    cs, n = GLA_CHUNK, s // GLA_CHUNK
    q = q * (1.0 / math.sqrt(dk))
    chunks = lambda t: jnp.moveaxis(t.reshape(b, h, n, cs, t.shape[-1]), 2, 0)
    bc = jnp.cumsum(gk.reshape(b, h, n, cs, dk), axis=3)
    causal = jnp.tril(jnp.ones((cs, cs), dtype=bool))[:, :, None]

    def step(state, inp):
        q_c, k_c, v_c, b_c = inp
        o_inter = jnp.einsum('bhcd,bhde->bhce', q_c * jnp.exp(b_c), state)
        rel = jnp.exp(jnp.where(causal, b_c[:, :, :, None, :] - b_c[:, :, None, :, :], -jnp.inf))
        scores = jnp.einsum('bhid,bhjd,bhijd->bhij', q_c, k_c, rel)
        o = o_inter + jnp.einsum('bhij,bhje->bhie', scores, v_c)
        b_last = b_c[:, :, -1:, :]
        state = state * jnp.exp(b_last[:, :, 0, :, None]) + jnp.einsum('bhcd,bhce->bhde', k_c * jnp.exp(b_last - b_c), v_c)
        return state, o

    _, o = lax.scan(step, jnp.zeros((b, h, dk, dv), q.dtype),
                    (chunks(q), chunks(k), chunks(v), jnp.moveaxis(bc, 2, 0)))
    return jnp.moveaxis(o, 0, 2).reshape(b, h, s, dv)


def gla_mixer(h, w_in, w_gk2, b_gk2, norm_g, w_out):
    bsz, s, _ = h.shape
    q, k, v, r, gk_low = split_cols(h @ w_in, [GLA_KEY_DIM, GLA_KEY_DIM, GLA_VALUE_DIM, GLA_VALUE_DIM, GLA_GATE_RANK])
    gk = jax.nn.log_sigmoid((gk_low @ w_gk2 + b_gk2).astype(jnp.float32)) / GLA_GATE_NORMALIZER
    hk = lambda t: t.astype(jnp.float32).reshape(bsz, s, GLA_HEADS, GLA_HEAD_K).transpose(0, 2, 1, 3)
    hv = lambda t: t.astype(jnp.float32).reshape(bsz, s, GLA_HEADS, GLA_HEAD_V).transpose(0, 2, 1, 3)
    o = gla_chunked(hk(q), hk(k), hv(v), hk(gk)).transpose(0, 2, 1, 3)
    o = rms_norm(o, norm_g) * jax.nn.silu(r.astype(jnp.float32).reshape(bsz, s, GLA_HEADS, GLA_HEAD_V))
    return o.reshape(bsz, s, GLA_VALUE_DIM).astype(h.dtype) @ w_out


def swiglu(h, w1, w3, w2):
    return (jax.nn.silu(h @ w1) * (h @ w3)) @ w2


def setup_inputs(seed: int = 0) -> dict:
    key = jax.random.key(seed)
    ks = iter(jax.random.split(key, 32))
    nrm = lambda shape, std: jax.random.normal(next(ks), shape, jnp.float32) * std
    gain = lambda shape: 1.0 + nrm(shape, 0.02)
    offset = jax.random.randint(next(ks), (BATCH, 1), 0, 1024, dtype=jnp.int32)
    positions = offset + jnp.arange(SEQ, dtype=jnp.int32)[None, :]
    a_log = jnp.log(jax.random.uniform(next(ks), (N_EVEN, GDN_HEADS), jnp.float32, 1.0, 16.0))
    dt = jnp.exp(jax.random.uniform(next(ks), (N_EVEN, GDN_HEADS), jnp.float32, math.log(1e-3), math.log(1e-1)))
    dt_bias = dt + jnp.log(-jnp.expm1(-dt))
    return {
        "x": nrm((BATCH, SEQ, D_MODEL), 1.0),
        "c": nrm((BATCH, D_MODEL), 1.0),
        "positions": positions,
        "norm_g": gain((DEPTH, 2, D_MODEL)),
        "ada_w": nrm((DEPTH, 2, D_MODEL, 3 * D_MODEL), 0.5 * D_MODEL ** -0.5),
        "ada_b": nrm((DEPTH, 2, 3 * D_MODEL), 0.01),
        "ab_w_in": nrm((N_EVEN, D_MODEL, EVEN_IN), D_MODEL ** -0.5),
        "mla_q_norm": gain((N_EVEN, MLA_Q_RANK)),
        "mla_w_uq": nrm((N_EVEN, MLA_Q_RANK, MLA_HEADS * MLA_QK_DIM), MLA_Q_RANK ** -0.5),
        "mla_kv_norm": gain((N_EVEN, MLA_KV_RANK)),
        "mla_w_ukv": nrm((N_EVEN, MLA_KV_RANK, MLA_HEADS * (MLA_NOPE_DIM + MLA_V_DIM)), MLA_KV_RANK ** -0.5),
        "gdn_conv_w": nrm((N_EVEN, GDN_CONV, 3 * GDN_DIM), GDN_CONV ** -0.5),
        "gdn_a_log": a_log,
        "gdn_dt_bias": dt_bias,
        "gdn_norm": gain((N_EVEN, GDN_HEAD_DIM)),
        "ab_w_out": nrm((N_EVEN, EVEN_MIX, D_MODEL), EVEN_MIX ** -0.5),
        "gla_w_in": nrm((N_ODD, D_MODEL, GLA_IN), D_MODEL ** -0.5),
        "gla_w_gk2": nrm((N_ODD, GLA_GATE_RANK, GLA_KEY_DIM), GLA_GATE_RANK ** -0.5),
        "gla_b_gk2": nrm((N_ODD, GLA_KEY_DIM), 0.01),
        "gla_norm": gain((N_ODD, GLA_HEAD_V)),
        "gla_w_out": nrm((N_ODD, GLA_VALUE_DIM, D_MODEL), GLA_VALUE_DIM ** -0.5),
        "ffn_w1": nrm((DEPTH, D_MODEL, D_FF), D_MODEL ** -0.5),
        "ffn_w3": nrm((DEPTH, D_MODEL, D_FF), D_MODEL ** -0.5),
        "ffn_w2": nrm((DEPTH, D_FF, D_MODEL), D_FF ** -0.5),
        "final_norm": gain((D_MODEL,)),
    }


def reference(x, c, positions, norm_g, ada_w, ada_b, ab_w_in, mla_q_norm, mla_w_uq, mla_kv_norm, mla_w_ukv,
              gdn_conv_w, gdn_a_log, gdn_dt_bias, gdn_norm, ab_w_out, gla_w_in, gla_w_gk2, gla_b_gk2, gla_norm,
              gla_w_out, ffn_w1, ffn_w3, ffn_w2, final_norm):
    cos, sin = rope_tables(positions)
    for layer in range(DEPTH):
        i = layer // 2
        shift, scale, gate = ada_modulation(c, ada_w[layer, 0], ada_b[layer, 0])
        h = rms_norm(x, norm_g[layer, 0]) * (1.0 + scale) + shift
        if layer % 2 == 0:
            mix = ab_mixer(h, cos, sin, ab_w_in[i], mla_q_norm[i], mla_w_uq[i], mla_kv_norm[i], mla_w_ukv[i],
                           gdn_conv_w[i], gdn_a_log[i], gdn_dt_bias[i], gdn_norm[i], ab_w_out[i])
        else:
            mix = gla_mixer(h, gla_w_in[i], gla_w_gk2[i], gla_b_gk2[i], gla_norm[i], gla_w_out[i])
        x = x + gate * mix
        shift, scale, gate = ada_modulation(c, ada_w[layer, 1], ada_b[layer, 1])
        h = rms_norm(x, norm_g[layer, 1]) * (1.0 + scale) + shift
        x = x + gate * swiglu(h, ffn_w1[layer], ffn_w3[layer], ffn_w2[layer])
    return rms_norm(x, final_norm)
```

```python
import functools
import math

import jax
import jax.numpy as jnp
from jax import lax
from jax.experimental import pallas as pl
from jax.experimental.pallas import tpu as pltpu

F32 = jnp.float32
BF16 = jnp.bfloat16

EPS = 1e-6
ROPE_THETA = 10000.0
MLA_HEADS = 8
MLA_NOPE = 128
MLA_ROPE = 64
MLA_V = 128
MLA_QK = MLA_NOPE + MLA_ROPE
MLA_QK_PAD = 256
MLA_Q_RANK = 512
MLA_KV_RANK = 512
GDN_HEADS = 8
GDN_HEAD_DIM = 128
GDN_DIM = GDN_HEADS * GDN_HEAD_DIM
GDN_CONV = 4
GLA_HEADS = 4
GLA_GATE_RANK = 16
GLA_GATE_NORMALIZER = 16.0
GLA_SUB = 16
CHUNK = 64
LANES = 128
VMEM_LIMIT = 48 * 1024 * 1024


def _cparams(*sem):
    return pltpu.CompilerParams(dimension_semantics=sem, vmem_limit_bytes=VMEM_LIMIT)


def _dot(a, b):
    return jnp.dot(a, b, preferred_element_type=F32)


def _dot_nt(a, b):
    return lax.dot_general(a, b, (((1,), (1,)), ((), ())), preferred_element_type=F32)


def _dot_tn(a, b):
    return lax.dot_general(a, b, (((0,), (0,)), ((), ())), preferred_element_type=F32)


def _split3(x):
    x1 = x.astype(BF16)
    r1 = x - x1.astype(F32)
    x2 = r1.astype(BF16)
    x3 = (r1 - x2.astype(F32)).astype(BF16)
    return x1, x2, x3


def _cumsum_rows(tri, x):
    x1, x2, x3 = _split3(x)
    return _dot(tri, x1) + _dot(tri, x2) + _dot(tri, x3)


def _silu(x):
    return x * jax.nn.sigmoid(x)


def _softplus(x):
    return jnp.maximum(x, 0.0) + jnp.log1p(jnp.exp(-jnp.abs(x)))


def _rms(x):
    return x * lax.rsqrt(jnp.mean(x * x, axis=-1, keepdims=True) + EPS)


def _ada_kernel(c_ref, w_ref, b_ref, o_ref):
    s = _silu(c_ref[...])
    o_ref[0] = jnp.sum(w_ref[0] * s, axis=0, keepdims=True) + b_ref[0]


def _ada_modulation(c_col, w, b, *, tn=768):
    n_mod, d, n = w.shape
    return pl.pallas_call(
        _ada_kernel,
        grid=(n_mod, n // tn),
        in_specs=[
            pl.BlockSpec((d, 1), lambda l, j: (0, 0)),
            pl.BlockSpec((1, d, tn), lambda l, j: (l, 0, j)),
            pl.BlockSpec((1, 1, tn), lambda l, j: (l, 0, j)),
        ],
        out_specs=pl.BlockSpec((1, 1, tn), lambda l, j: (l, 0, j)),
        out_shape=jax.ShapeDtypeStruct((n_mod, 1, n), F32),
        compiler_params=_cparams("parallel", "parallel"),
        name="ada_modulation",
    )(c_col, w, b)


def _modulated_norm(x, g, sc, sh):
    return _rms(x) * g * (1.0 + sc) + sh


def _norm_proj_kernel(x_ref, g_ref, sc_ref, sh_ref, w_ref, ws_ref, o_ref, os_ref, h_ref):
    @pl.when(pl.program_id(1) == 0)
    def _():
        h = _modulated_norm(x_ref[...], g_ref[...], sc_ref[...], sh_ref[...]).astype(BF16)
        h_ref[...] = h
        os_ref[...] = _dot(h, ws_ref[...])

    o_ref[...] = _dot(h_ref[...], w_ref[...]).astype(o_ref.dtype)


def _norm_proj(x, g, sc, sh, w, ws, *, tm, tn):
    s, d = x.shape
    n, ns = w.shape[1], ws.shape[1]
    vec = pl.BlockSpec((1, d), lambda i, j: (0, 0))
    return pl.pallas_call(
        _norm_proj_kernel,
        grid=(s // tm, n // tn),
        in_specs=[
            pl.BlockSpec((tm, d), lambda i, j: (i, 0)),
            vec, vec, vec,
            pl.BlockSpec((d, tn), lambda i, j: (0, j)),
            pl.BlockSpec((d, ns), lambda i, j: (0, 0)),
        ],
        out_specs=[
            pl.BlockSpec((tm, tn), lambda i, j: (i, j)),
            pl.BlockSpec((tm, ns), lambda i, j: (i, 0)),
        ],
        out_shape=[jax.ShapeDtypeStruct((s, n), BF16), jax.ShapeDtypeStruct((s, ns), F32)],
        scratch_shapes=[pltpu.VMEM((tm, d), BF16)],
        compiler_params=_cparams("parallel", "arbitrary"),
        name="norm_proj",
    )(x, g, sc, sh, w, ws)


def _mla_up_kernel(cq_ref, ckv_ref, kr_ref, krr_ref, pos_ref, invf_ref, qn_ref, kvn_ref, wq_ref, wkv_ref,
                   q_out, k_out, v_out):
    hd = MLA_HEADS * LANES
    cqn = (_rms(cq_ref[...].astype(F32)) * qn_ref[...]).astype(BF16)
    ckvn = (_rms(ckv_ref[...].astype(F32)) * kvn_ref[...]).astype(BF16)
    qall = _dot(cqn, wq_ref[...])
    kv = _dot(ckvn, wkv_ref[...])
    ang = pos_ref[...].astype(F32) * invf_ref[...]
    cos, sin = jnp.cos(ang), jnp.sin(ang)
    k_rope = kr_ref[...] * cos + krr_ref[...] * sin
    scale = 1.0 / math.sqrt(MLA_QK)
    for h in range(MLA_HEADS):
        lo, hi = h * LANES, (h + 1) * LANES
        q_rope = qall[:, hd + lo:hd + hi] * cos + qall[:, 2 * hd + lo:2 * hd + hi] * sin
        q_out[h] = (jnp.concatenate([qall[:, lo:hi], q_rope], axis=-1) * scale).astype(BF16)
        k_out[h] = jnp.concatenate([kv[:, lo:hi], k_rope], axis=-1).astype(BF16)
        v_out[h] = kv[:, hd + lo:hd + hi].astype(BF16)


def _mla_up(main, small, pos_col, invf, qn, kvn, wq, wkv, *, tm):
    s = main.shape[0]
    full = lambda a: pl.BlockSpec(a.shape, lambda i: (0,) * a.ndim)
    qk_shape = jax.ShapeDtypeStruct((MLA_HEADS, s, MLA_QK_PAD), BF16)
    return pl.pallas_call(
        _mla_up_kernel,
        grid=(s // tm,),
        in_specs=[
            pl.BlockSpec((tm, MLA_Q_RANK), lambda i: (i, 0)),
            pl.BlockSpec((tm, MLA_KV_RANK), lambda i: (i, 1)),
            pl.BlockSpec((tm, LANES), lambda i: (i, 0)),
            pl.BlockSpec((tm, LANES), lambda i: (i, 1)),
            pl.BlockSpec((tm, 1), lambda i: (i, 0)),
            full(invf), full(qn), full(kvn), full(wq), full(wkv),
        ],
        out_specs=[
            pl.BlockSpec((MLA_HEADS, tm, MLA_QK_PAD), lambda i: (0, i, 0)),
            pl.BlockSpec((MLA_HEADS, tm, MLA_QK_PAD), lambda i: (0, i, 0)),
            pl.BlockSpec((MLA_HEADS, tm, MLA_V), lambda i: (0, i, 0)),
        ],
        out_shape=[qk_shape, qk_shape, jax.ShapeDtypeStruct((MLA_HEADS, s, MLA_V), BF16)],
        compiler_params=_cparams("parallel"),
        name="mla_up",
    )(main, main, small, small, pos_col, invf, qn, kvn, wq, wkv)


def _attn_kernel(q_ref, k_ref, v_ref, o_ref, m_ref, l_ref, acc_ref):
    i, j = pl.program_id(1), pl.program_id(2)

    @pl.when(j == 0)
    def _():
        m_ref[...] = jnp.full(m_ref.shape, -jnp.inf, F32)
        l_ref[...] = jnp.zeros(l_ref.shape, F32)
        acc_ref[...] = jnp.zeros(acc_ref.shape, F32)

    def step(diagonal):
        sc = _dot_nt(q_ref[0], k_ref[0])
        if diagonal:
            row = lax.broadcasted_iota(jnp.int32, sc.shape, 0)
            col = lax.broadcasted_iota(jnp.int32, sc.shape, 1)
            sc = jnp.where(col <= row, sc, -jnp.inf)
        m_prev = m_ref[...]
        m_new = jnp.maximum(m_prev, jnp.max(sc, axis=-1, keepdims=True))
        p = jnp.exp(sc - m_new)
        alpha = jnp.exp(m_prev - m_new)
        l_ref[...] = alpha * l_ref[...] + jnp.sum(p, axis=-1, keepdims=True)
        acc_ref[...] = alpha * acc_ref[...] + _dot(p.astype(BF16), v_ref[0])
        m_ref[...] = m_new

    @pl.when(j < i)
    def _():
        step(False)

    @pl.when(j == i)
    def _():
        step(True)
        o_ref[...] = (acc_ref[...] / l_ref[...]).astype(o_ref.dtype)


def _causal_attention(q, k, v, *, tq):
    h, s, dqk = q.shape
    dv = v.shape[-1]
    nb = s // tq
    return pl.pallas_call(
        _attn_kernel,
        grid=(h, nb, nb),
        in_specs=[
            pl.BlockSpec((1, tq, dqk), lambda hh, i, j: (hh, i, 0)),
            pl.BlockSpec((1, tq, dqk), lambda hh, i, j: (hh, jnp.minimum(j, i), 0)),
            pl.BlockSpec((1, tq, dv), lambda hh, i, j: (hh, jnp.minimum(j, i), 0)),
        ],
        out_specs=pl.BlockSpec((tq, dv), lambda hh, i, j: (i, hh)),
        out_shape=jax.ShapeDtypeStruct((s, h * dv), BF16),
        scratch_shapes=[pltpu.VMEM((tq, 1), F32), pltpu.VMEM((tq, 1), F32), pltpu.VMEM((tq, dv), F32)],
        compiler_params=_cparams("parallel", "parallel", "arbitrary"),
        name="mla_attention",
    )(q, k, v)


def _unit_lower_inverse(low, eye):
    n = low.shape[0]
    p = low
    r = eye - low
    for _ in range(int(math.log2(n)) - 1):
        pb = p.astype(BF16)
        p = _dot(pb, pb)
        r = r + _dot(r.astype(BF16), p.astype(BF16))
    return r


def _gdn_kernel(q_ref, k_ref, v_ref, z_ref, sm_ref, cw_ref, alog_ref, dtb_ref, gn_ref, o_ref, hist_ref, state_ref):
    c = CHUNK
    d = GDN_HEAD_DIM

    @pl.when(pl.program_id(0) == 0)
    def _():
        hist_ref[pl.ds(0, 8), :] = jnp.zeros((8, 3 * GDN_DIM), F32)
        state_ref[...] = jnp.zeros(state_ref.shape, F32)

    hist_ref[pl.ds(8, c), 0:GDN_DIM] = q_ref[...].astype(F32)
    hist_ref[pl.ds(8, c), GDN_DIM:2 * GDN_DIM] = k_ref[...].astype(F32)
    hist_ref[pl.ds(8, c), 2 * GDN_DIM:3 * GDN_DIM] = v_ref[...].astype(F32)
    base = 8 - (GDN_CONV - 1)
    y = hist_ref[pl.ds(base, c), :] * cw_ref[0:1, :]
    for j in range(1, GDN_CONV):
        y = y + hist_ref[pl.ds(base + j, c), :] * cw_ref[j:j + 1, :]
    hist_ref[pl.ds(0, 8), :] = hist_ref[pl.ds(c, 8), :]
    y = _silu(y)

    sm = sm_ref[...]
    beta_all = jax.nn.sigmoid(sm)
    g_all = -jnp.exp(alog_ref[...]) * _softplus(sm + dtb_ref[...])
    row = lax.broadcasted_iota(jnp.int32, (c, c), 0)
    col = lax.broadcasted_iota(jnp.int32, (c, c), 1)
    incl, strict = col <= row, col < row
    tri = jnp.where(incl, 1.0, 0.0).astype(BF16)
    eye = jnp.where(col == row, 1.0, 0.0).astype(F32)
    gc_all = _cumsum_rows(tri, g_all)
    gc_t = jnp.concatenate([gc_all, jnp.zeros((LANES - c, LANES), F32)], axis=0).T

    for h in range(GDN_HEADS):
        lo, hi = h * d, (h + 1) * d
        qh = y[:, lo:hi]
        kh = y[:, GDN_DIM + lo:GDN_DIM + hi]
        vh = y[:, 2 * GDN_DIM + lo:2 * GDN_DIM + hi]
        qh = qh * lax.rsqrt(jnp.sum(qh * qh, axis=-1, keepdims=True) + EPS) * (1.0 / math.sqrt(d))
        kh = kh * lax.rsqrt(jnp.sum(kh * kh, axis=-1, keepdims=True) + EPS)
        beta = beta_all[:, h:h + 1]
        gc = gc_all[:, GDN_HEADS + h:GDN_HEADS + h + 1]
        gcr = gc_t[GDN_HEADS + h:GDN_HEADS + h + 1, 0:c]
        decay = jnp.exp(jnp.where(incl, gc - gcr, -jnp.inf))
        kb = kh * beta
        khb = kh.astype(BF16)
        low = jnp.where(strict, _dot_nt(kb.astype(BF16), khb) * decay, 0.0)
        t_inv = _unit_lower_inverse(low, eye)
        egc = jnp.exp(gc)
        rhs = jnp.concatenate([vh * beta, kb * egc], axis=-1)
        sol = _dot(t_inv.astype(BF16), rhs.astype(BF16))
        u, w = sol[:, :d], sol[:, d:]
        intra = _dot_nt(qh.astype(BF16), khb) * decay
        g_last = gcr[:, c - 1:c]
        k_dec = kh * jnp.exp(g_last - gc)
        q_dec = qh * egc
        st = state_ref[h]
        stb = st.astype(BF16)
        v_new = u - _dot(w.astype(BF16), stb)
        v_newb = v_new.astype(BF16)
        o = _dot(q_dec.astype(BF16), stb) + _dot(intra.astype(BF16), v_newb)
        state_ref[h] = st * jnp.exp(g_last) + _dot_tn(k_dec.astype(BF16), v_newb)
        o = _rms(o) * gn_ref[...] * _silu(z_ref[:, lo:hi].astype(F32))
        o_ref[:, lo:hi] = o.astype(o_ref.dtype)


def _gdn(main, small, conv_w, alog, dtb, gnorm):
    s = main.shape[0]
    full = lambda a: pl.BlockSpec(a.shape, lambda i: (0,) * a.ndim)
    blk = lambda jcol: pl.BlockSpec((CHUNK, GDN_DIM), lambda i: (i, jcol))
    return pl.pallas_call(
        _gdn_kernel,
        grid=(s // CHUNK,),
        in_specs=[blk(1), blk(2), blk(3), blk(4), pl.BlockSpec((CHUNK, LANES), lambda i: (i, 2)),
                  full(conv_w), full(alog), full(dtb), full(gnorm)],
        out_specs=pl.BlockSpec((CHUNK, GDN_DIM), lambda i: (i, 0)),
        out_shape=jax.ShapeDtypeStruct((s, GDN_DIM), BF16),
        scratch_shapes=[pltpu.VMEM((CHUNK + 8, 3 * GDN_DIM), F32),
                        pltpu.VMEM((GDN_HEADS, GDN_HEAD_DIM, GDN_HEAD_DIM), F32)],
        compiler_params=_cparams("arbitrary"),
        name="gated_deltanet",
    )(main, main, main, main, small, conv_w, alog, dtb, gnorm)


def _gla_kernel(q_ref, k_ref, v_ref, r_ref, sm_ref, wg_ref, bg_ref, gn_ref, o_ref, state_ref):
    c = CHUNK
    dk = q_ref.shape[1] // GLA_HEADS
    dv = v_ref.shape[1] // GLA_HEADS

    @pl.when(pl.program_id(0) == 0)
    def _():
        state_ref[...] = jnp.zeros(state_ref.shape, F32)

    a1, a2, _ = _split3(sm_ref[...])
    w1, w2, _ = _split3(wg_ref[...])
    pre = _dot(a1, w1) + _dot(a1, w2) + _dot(a2, w1) + bg_ref[...]
    gk = (jnp.minimum(pre, 0.0) - jnp.log1p(jnp.exp(-jnp.abs(pre)))) * (1.0 / GLA_GATE_NORMALIZER)
    row = lax.broadcasted_iota(jnp.int32, (c, c), 0)
    col = lax.broadcasted_iota(jnp.int32, (c, c), 1)
    tri = jnp.where(col <= row, 1.0, 0.0).astype(BF16)
    b_all = _cumsum_rows(tri, gk)
    sub_row = lax.broadcasted_iota(jnp.int32, (GLA_SUB, 1), 0)
    sub_col = lax.broadcasted_iota(jnp.int32, (GLA_SUB, GLA_SUB), 1)

    for h in range(GLA_HEADS):
        bh = b_all[:, h * dk:(h + 1) * dk]
        qh = q_ref[:, h * dk:(h + 1) * dk].astype(F32) * (1.0 / math.sqrt(dk))
        kh = k_ref[:, h * dk:(h + 1) * dk].astype(F32)
        vh = v_ref[:, h * dv:(h + 1) * dv]
        st = state_ref[h]
        o_inter = _dot_nt((qh * jnp.exp(bh)).astype(BF16), st.astype(BF16))
        o_rows = []
        for blk in range(c // GLA_SUB):
            r0 = blk * GLA_SUB
            qs, bs = qh[r0:r0 + GLA_SUB], bh[r0:r0 + GLA_SUB]
            ks = kh[r0:r0 + GLA_SUB]
            a_diag = jnp.zeros((GLA_SUB, GLA_SUB), F32)
            for j in range(GLA_SUB):
                e = jnp.exp(jnp.where(sub_row >= j, bs - bs[j:j + 1], -jnp.inf))
                dj = jnp.sum(qs * ks[j:j + 1] * e, axis=-1, keepdims=True)
                a_diag = jnp.where(sub_col == j, dj, a_diag)
            o_blk = _dot(a_diag.astype(BF16), vh[r0:r0 + GLA_SUB])
            if blk > 0:
                b_ref0 = bs[0:1]
                q_t = qs * jnp.exp(bs - b_ref0)
                k_t = kh[0:r0] * jnp.exp(b_ref0 - bh[0:r0])
                a_off = _dot_nt(q_t.astype(BF16), k_t.astype(BF16))
                o_blk = o_blk + _dot(a_off.astype(BF16), vh[0:r0])
            o_rows.append(o_blk)
        o = o_inter + jnp.concatenate(o_rows, axis=0)
        b_last = bh[c - 1:c]
        k_dec = kh * jnp.exp(b_last - bh)
        state_ref[h] = st * jnp.exp(b_last) + _dot_tn(vh, k_dec.astype(BF16))
        o = _rms(o) * gn_ref[...] * _silu(r_ref[:, h * dv:(h + 1) * dv].astype(F32))
        o_ref[:, h * dv:(h + 1) * dv] = o.astype(o_ref.dtype)


def _gla(main, small, wg, bg, gnorm, *, key_dim, value_dim):
    s = main.shape[0]
    full = lambda a: pl.BlockSpec(a.shape, lambda i: (0,) * a.ndim)
    return pl.pallas_call(
        _gla_kernel,
        grid=(s // CHUNK,),
        in_specs=[
            pl.BlockSpec((CHUNK, key_dim), lambda i: (i, 0)),
            pl.BlockSpec((CHUNK, key_dim), lambda i: (i, 1)),
            pl.BlockSpec((CHUNK, value_dim), lambda i: (i, 1)),
            pl.BlockSpec((CHUNK, value_dim), lambda i: (i, 2)),
            pl.BlockSpec((CHUNK, LANES), lambda i: (i, 0)),
            full(wg), full(bg), full(gnorm),
        ],
        out_specs=pl.BlockSpec((CHUNK, value_dim), lambda i: (i, 0)),
        out_shape=jax.ShapeDtypeStruct((s, value_dim), BF16),
        scratch_shapes=[pltpu.VMEM((GLA_HEADS, value_dim // GLA_HEADS, key_dim // GLA_HEADS), F32)],
        compiler_params=_cparams("arbitrary"),
        name="gated_linear_attention",
    )(main, main, main, main, small, wg, bg, gnorm)


def _out_proj_kernel(a1_ref, a2_ref, w_ref, x_ref, gate_ref, o_ref):
    half = a1_ref.shape[1]
    mix = _dot(a1_ref[...], w_ref[0:half, :]) + _dot(a2_ref[...], w_ref[half:, :])
    o_ref[...] = x_ref[...] + gate_ref[...] * mix


def _out_proj(a1, a2, col1, col2, w, x, gate, *, tm):
    s, d = x.shape
    half = w.shape[0] // 2
    return pl.pallas_call(
        _out_proj_kernel,
        grid=(s // tm,),
        in_specs=[
            pl.BlockSpec((tm, half), lambda i: (i, col1)),
            pl.BlockSpec((tm, half), lambda i: (i, col2)),
            pl.BlockSpec(w.shape, lambda i: (0, 0)),
            pl.BlockSpec((tm, d), lambda i: (i, 0)),
            pl.BlockSpec((1, d), lambda i: (0, 0)),
        ],
        out_specs=pl.BlockSpec((tm, d), lambda i: (i, 0)),
        out_shape=jax.ShapeDtypeStruct((s, d), F32),
        compiler_params=_cparams("parallel"),
        name="out_proj_residual",
    )(a1, a2, w, x, gate)


def _ffn_kernel(x_ref, g_ref, sc_ref, sh_ref, gate_ref, fin_ref, w1_ref, w3_ref, w2_ref, o_ref, h_ref, acc_ref,
                *, final_norm):
    j = pl.program_id(1)

    @pl.when(j == 0)
    def _():
        h_ref[...] = _modulated_norm(x_ref[...], g_ref[...], sc_ref[...], sh_ref[...]).astype(BF16)
        acc_ref[...] = jnp.zeros(acc_ref.shape, F32)

    h = h_ref[...]
    a = _dot(h, w1_ref[...])
    t = (_silu(a) * _dot(h, w3_ref[...])).astype(BF16)
    acc_ref[...] += _dot(t, w2_ref[...])

    @pl.when(j == pl.num_programs(1) - 1)
    def _():
        y = x_ref[...] + gate_ref[...] * acc_ref[...]
        if final_norm:
            y = _rms(y) * fin_ref[...]
        o_ref[...] = y


def _ffn(x, g, sc, sh, gate, fin, w1, w3, w2, *, tm, tf, final_norm):
    s, d = x.shape
    dff = w1.shape[1]
    vec = pl.BlockSpec((1, d), lambda i, j: (0, 0))
    return pl.pallas_call(
        functools.partial(_ffn_kernel, final_norm=final_norm),
        grid=(s // tm, dff // tf),
        in_specs=[
            pl.BlockSpec((tm, d), lambda i, j: (i, 0)),
            vec, vec, vec, vec, vec,
            pl.BlockSpec((d, tf), lambda i, j: (0, j)),
            pl.BlockSpec((d, tf), lambda i, j: (0, j)),
            pl.BlockSpec((tf, d), lambda i, j: (j, 0)),
        ],
        out_specs=pl.BlockSpec((tm, d), lambda i, j: (i, 0)),
        out_shape=jax.ShapeDtypeStruct((s, d), F32),
        scratch_shapes=[pltpu.VMEM((tm, d), BF16), pltpu.VMEM((tm, d), F32)],
        compiler_params=_cparams("parallel", "arbitrary"),
        name="swiglu_ffn",
    )(x, g, sc, sh, gate, fin, w1, w3, w2)


def _rot_cols(w):
    half = w.shape[-1] // 2
    return jnp.concatenate([-w[..., half:], w[..., :half]], axis=-1)


def _pad_cols(w, n):
    return jnp.pad(w, [(0, 0)] * (w.ndim - 1) + [(0, n - w.shape[-1])])


def _even_in_weights(w_in):
    d = w_in.shape[0]
    o_kr = MLA_Q_RANK + MLA_KV_RANK
    o_qkv = o_kr + MLA_ROPE
    o_ba = o_qkv + 4 * GDN_DIM
    main = jnp.concatenate([w_in[:, :o_kr], w_in[:, o_qkv:o_ba]], axis=1)
    kr = w_in[:, o_kr:o_qkv]
    small = jnp.concatenate([_pad_cols(kr, LANES), _pad_cols(_rot_cols(kr), LANES), _pad_cols(w_in[:, o_ba:], LANES)],
                            axis=1)
    return main.astype(BF16), small.astype(BF16)


def _mla_weights(w_uq, w_ukv):
    r = w_uq.shape[0]
    wq = w_uq.reshape(r, MLA_HEADS, MLA_QK)
    nope, rope = wq[..., :MLA_NOPE], wq[..., MLA_NOPE:]
    flat = lambda t: t.reshape(r, MLA_HEADS * LANES)
    wq_all = jnp.concatenate([flat(nope), flat(_pad_cols(rope, LANES)), flat(_pad_cols(_rot_cols(rope), LANES))], axis=1)
    wkv = w_ukv.reshape(w_ukv.shape[0], MLA_HEADS, MLA_NOPE + MLA_V)
    wkv_all = jnp.concatenate([flat(wkv[..., :MLA_NOPE]), flat(wkv[..., MLA_NOPE:])], axis=1)
    return wq_all.astype(BF16), wkv_all.astype(BF16)


def kernel(x, c, positions, norm_g, ada_w, ada_b, ab_w_in, mla_q_norm, mla_w_uq, mla_kv_norm, mla_w_ukv, gdn_conv_w,
           gdn_a_log, gdn_dt_bias, gdn_norm, ab_w_out, gla_w_in, gla_w_gk2, gla_b_gk2, gla_norm, gla_w_out, ffn_w1,
           ffn_w3, ffn_w2, final_norm):
    batch, s, d = x.shape
    assert batch == 1 and c.shape == (1, d)
    depth = norm_g.shape[0]
    key_dim = gla_w_gk2.shape[2]
    value_dim = gla_w_out.shape[1]

    mods = _ada_modulation(c.reshape(d, 1), ada_w.reshape(depth * 2, d, 3 * d), ada_b.reshape(depth * 2, 1, 3 * d))
    xs = x.reshape(s, d)
    pos_col = positions.reshape(s, 1)
    half = MLA_ROPE // 2
    inv_freq = ROPE_THETA ** (-jnp.arange(half, dtype=F32) / half)
    invf = jnp.tile(inv_freq, LANES // half).reshape(1, LANES)
    row = lambda v: v.reshape(1, -1)

    for layer in range(depth):
        i = layer // 2
        m = mods[2 * layer]
        shift, scale, gate = m[:, :d], m[:, d:2 * d], m[:, 2 * d:]
        gain = row(norm_g[layer, 0])
        if layer % 2 == 0:
            w_main, w_small = _even_in_weights(ab_w_in[i])
            main, small = _norm_proj(xs, gain, scale, shift, w_main, w_small, tm=512, tn=640)
            wq, wkv = _mla_weights(mla_w_uq[i], mla_w_ukv[i])
            q, k, v = _mla_up(main, small, pos_col, invf, row(mla_q_norm[i]), row(mla_kv_norm[i]), wq, wkv, tm=256)
            o_a = _causal_attention(q, k, v, tq=512)
            lane_pad = lambda vec: jnp.pad(vec, (GDN_HEADS, LANES - 2 * GDN_HEADS)).reshape(1, LANES)
            o_b = _gdn(main, small, gdn_conv_w[i], lane_pad(gdn_a_log[i]), lane_pad(gdn_dt_bias[i]), row(gdn_norm[i]))
            xs = _out_proj(o_a, o_b, 0, 0, ab_w_out[i].astype(BF16), xs, gate, tm=256)
        else:
            w_in = gla_w_in[i]
            n_main = 2 * key_dim + 2 * value_dim
            main, small = _norm_proj(xs, gain, scale, shift, w_in[:, :n_main].astype(BF16),
                                     _pad_cols(w_in[:, n_main:], LANES).astype(BF16), tm=512, tn=768)
            wg = jnp.pad(gla_w_gk2[i], ((0, LANES - GLA_GATE_RANK), (0, 0)))
            o_c = _gla(main, small, wg, row(gla_b_gk2[i]), row(gla_norm[i]), key_dim=key_dim, value_dim=value_dim)
            xs = _out_proj(o_c, o_c, 0, 1, gla_w_out[i].astype(BF16), xs, gate, tm=256)
        m = mods[2 * layer + 1]
        shift, scale, gate = m[:, :d], m[:, d:2 * d], m[:, 2 * d:]
        xs = _ffn(xs, row(norm_g[layer, 1]), scale, shift, gate, row(final_norm), ffn_w1[layer].astype(BF16),
                  ffn_w3[layer].astype(BF16), ffn_w2[layer].astype(BF16), tm=512, tf=512,
                  final_norm=(layer == depth - 1))
    return xs.reshape(batch, s, d)
```

```python
import functools
import math

import jax
import jax.numpy as jnp
from jax import lax
from jax.experimental import pallas as pl
from jax.experimental.pallas import tpu as pltpu

F32 = jnp.float32
BF16 = jnp.bfloat16

EPS = 1e-6
ROPE_THETA = 10000.0
MLA_HEADS = 8
MLA_NOPE = 128
MLA_ROPE = 64
MLA_V = 128
MLA_QK = MLA_NOPE + MLA_ROPE
MLA_QK_PAD = 256
MLA_Q_RANK = 512
MLA_KV_RANK = 512
GDN_HEADS = 8
GDN_HEAD_DIM = 128
GDN_DIM = GDN_HEADS * GDN_HEAD_DIM
GDN_CONV = 4
GLA_HEADS = 4
GLA_GATE_RANK = 16
GLA_GATE_NORMALIZER = 16.0
GLA_SUB = 16
CHUNK = 64
LANES = 128
VMEM_LIMIT = 48 * 1024 * 1024


def _cparams(*sem):
    return pltpu.CompilerParams(dimension_semantics=sem, vmem_limit_bytes=VMEM_LIMIT)


def _dot(a, b):
    return jnp.dot(a, b, preferred_element_type=F32)


def _dot_nt(a, b):
    return lax.dot_general(a, b, (((1,), (1,)), ((), ())), preferred_element_type=F32)


def _dot_tn(a, b):
    return lax.dot_general(a, b, (((0,), (0,)), ((), ())), preferred_element_type=F32)


def _split3(x):
    x1 = x.astype(BF16)
    r1 = x - x1.astype(F32)
    x2 = r1.astype(BF16)
    x3 = (r1 - x2.astype(F32)).astype(BF16)
    return x1, x2, x3


def _cumsum_rows(tri, x):
    x1, x2, x3 = _split3(x)
    return _dot(tri, x1) + _dot(tri, x2) + _dot(tri, x3)


def _silu(x):
    return x * jax.nn.sigmoid(x)


def _softplus(x):
    return jnp.maximum(x, 0.0) + jnp.log1p(jnp.exp(-jnp.abs(x)))


def _rms(x):
    return x * lax.rsqrt(jnp.mean(x * x, axis=-1, keepdims=True) + EPS)


def _ada_kernel(c_ref, w_ref, b_ref, o_ref):
    s = _silu(c_ref[...])
    o_ref[0] = jnp.sum(w_ref[0] * s, axis=0, keepdims=True) + b_ref[0]


def _ada_modulation(c_col, w, b, *, tn=768):
    n_mod, d, n = w.shape
    return pl.pallas_call(
        _ada_kernel,
        grid=(n_mod, n // tn),
        in_specs=[
            pl.BlockSpec((d, 1), lambda l, j: (0, 0)),
            pl.BlockSpec((1, d, tn), lambda l, j: (l, 0, j)),
            pl.BlockSpec((1, 1, tn), lambda l, j: (l, 0, j)),
        ],
        out_specs=pl.BlockSpec((1, 1, tn), lambda l, j: (l, 0, j)),
        out_shape=jax.ShapeDtypeStruct((n_mod, 1, n), F32),
        compiler_params=_cparams("parallel", "parallel"),
        name="ada_modulation",
    )(c_col, w, b)


def _modulated_norm(x, g, sc, sh):
    return _rms(x) * g * (1.0 + sc) + sh


def _norm_proj_kernel(x_ref, g_ref, sc_ref, sh_ref, w_ref, ws_ref, o_ref, os_ref, h_ref):
    @pl.when(pl.program_id(1) == 0)
    def _():
        h = _modulated_norm(x_ref[...], g_ref[...], sc_ref[...], sh_ref[...]).astype(BF16)
        h_ref[...] = h
        os_ref[...] = _dot(h, ws_ref[...])

    o_ref[...] = _dot(h_ref[...], w_ref[...]).astype(o_ref.dtype)


def _norm_proj(x, g, sc, sh, w, ws, *, tm, tn):
    s, d = x.shape
    n, ns = w.shape[1], ws.shape[1]
    vec = pl.BlockSpec((1, d), lambda i, j: (0, 0))
    return pl.pallas_call(
        _norm_proj_kernel,
        grid=(s // tm, n // tn),
        in_specs=[
            pl.BlockSpec((tm, d), lambda i, j: (i, 0)),
            vec, vec, vec,
            pl.BlockSpec((d, tn), lambda i, j: (0, j)),
            pl.BlockSpec((d, ns), lambda i, j: (0, 0)),
        ],
        out_specs=[
            pl.BlockSpec((tm, tn), lambda i, j: (i, j)),
            pl.BlockSpec((tm, ns), lambda i, j: (i, 0)),
        ],
        out_shape=[jax.ShapeDtypeStruct((s, n), BF16), jax.ShapeDtypeStruct((s, ns), F32)],
        scratch_shapes=[pltpu.VMEM((tm, d), BF16)],
        compiler_params=_cparams("parallel", "arbitrary"),
        name="norm_proj",
    )(x, g, sc, sh, w, ws)


def _mla_up_kernel(cq_ref, ckv_ref, kr_ref, krr_ref, pos_ref, invf_ref, qn_ref, kvn_ref, wq_ref, wkv_ref,
                   q_out, k_out, v_out):
    hd = MLA_HEADS * LANES
    cqn = (_rms(cq_ref[...].astype(F32)) * qn_ref[...]).astype(BF16)
    ckvn = (_rms(ckv_ref[...].astype(F32)) * kvn_ref[...]).astype(BF16)
    qall = _dot(cqn, wq_ref[...])
    kv = _dot(ckvn, wkv_ref[...])
    ang = pos_ref[...].astype(F32) * invf_ref[...]
    cos, sin = jnp.cos(ang), jnp.sin(ang)
    k_rope = kr_ref[...] * cos + krr_ref[...] * sin
    scale = math.log2(math.e) / math.sqrt(MLA_QK)
    ones = jnp.ones((cq_ref.shape[0], MLA_V), F32)
    for h in range(MLA_HEADS):
        lo, hi = h * LANES, (h + 1) * LANES
        q_rope = qall[:, hd + lo:hd + hi] * cos + qall[:, 2 * hd + lo:2 * hd + hi] * sin
        q_out[h] = (jnp.concatenate([qall[:, lo:hi], q_rope], axis=-1) * scale).astype(BF16)
        k_out[h] = jnp.concatenate([kv[:, lo:hi], k_rope], axis=-1).astype(BF16)
        v_out[h] = jnp.concatenate([kv[:, hd + lo:hd + hi], ones], axis=-1).astype(BF16)


def _mla_up(main, small, pos_col, invf, qn, kvn, wq, wkv, *, tm):
    s = main.shape[0]
    full = lambda a: pl.BlockSpec(a.shape, lambda i: (0,) * a.ndim)
    qk_shape = jax.ShapeDtypeStruct((MLA_HEADS, s, MLA_QK_PAD), BF16)
    return pl.pallas_call(
        _mla_up_kernel,
        grid=(s // tm,),
        in_specs=[
            pl.BlockSpec((tm, MLA_Q_RANK), lambda i: (i, 0)),
            pl.BlockSpec((tm, MLA_KV_RANK), lambda i: (i, 1)),
            pl.BlockSpec((tm, LANES), lambda i: (i, 0)),
            pl.BlockSpec((tm, LANES), lambda i: (i, 1)),
            pl.BlockSpec((tm, 1), lambda i: (i, 0)),
            full(invf), full(qn), full(kvn), full(wq), full(wkv),
        ],
        out_specs=[
            pl.BlockSpec((MLA_HEADS, tm, MLA_QK_PAD), lambda i: (0, i, 0)),
            pl.BlockSpec((MLA_HEADS, tm, MLA_QK_PAD), lambda i: (0, i, 0)),
            pl.BlockSpec((MLA_HEADS, tm, 2 * MLA_V), lambda i: (0, i, 0)),
        ],
        out_shape=[qk_shape, qk_shape, jax.ShapeDtypeStruct((MLA_HEADS, s, 2 * MLA_V), BF16)],
        compiler_params=_cparams("parallel"),
        name="mla_up",
    )(main, main, small, small, pos_col, invf, qn, kvn, wq, wkv)


def _attn_kernel(q_ref, k_ref, v_ref, o_ref, s0_ref, s1_ref, m_ref, acc_ref, *, t):
    i = pl.program_id(1)
    q = q_ref[0]
    dv2 = v_ref.shape[-1]
    dv = dv2 // 2

    def tile(j):
        return pl.ds(pl.multiple_of(j * t, t), t)

    def scores(s_ref, j):
        s_ref[...] = _dot_nt(q, k_ref[0, tile(j), :])

    def update(s_ref, j, diagonal):
        s = s_ref[...]
        if diagonal:
            row = lax.broadcasted_iota(jnp.int32, (t, t), 0)
            col = lax.broadcasted_iota(jnp.int32, (t, t), 1)
            s = jnp.where(col <= row, s, -jnp.inf)
        m = m_ref[...]
        m_new = jnp.maximum(m, jnp.max(s, axis=1, keepdims=True))
        p = jnp.exp2(s - jnp.concatenate([m_new] * (t // LANES), axis=1))
        alpha = jnp.exp2(m - m_new)
        acc_ref[...] = (jnp.concatenate([alpha] * (dv2 // LANES), axis=1) * acc_ref[...]
                        + _dot(p.astype(BF16), v_ref[0, tile(j), :]))
        m_ref[...] = m_new

    m_ref[...] = jnp.full(m_ref.shape, -jnp.inf, F32)
    acc_ref[...] = jnp.zeros(acc_ref.shape, F32)
    scores(s0_ref, 0)

    def pair(jj, carry):
        j = 2 * jj
        scores(s1_ref, j + 1)
        update(s0_ref, j, False)
        scores(s0_ref, j + 2)
        update(s1_ref, j + 1, False)
        return carry

    lax.fori_loop(0, i // 2, pair, 0)

    @pl.when(i % 2 == 0)
    def _():
        update(s0_ref, i, True)

    @pl.when(i % 2 == 1)
    def _():
        scores(s1_ref, i)
        update(s0_ref, i - 1, False)
        update(s1_ref, i, True)

    acc = acc_ref[...]
    o_ref[...] = (acc[:, :dv] / acc[:, dv:]).astype(o_ref.dtype)


def _causal_attention(q, k, v, *, t):
    h, s, dqk = q.shape
    dv2 = v.shape[-1]
    return pl.pallas_call(
        functools.partial(_attn_kernel, t=t),
        grid=(h, s // t),
        in_specs=[
            pl.BlockSpec((1, t, dqk), lambda hh, i: (hh, i, 0)),
            pl.BlockSpec((1, s, dqk), lambda hh, i: (hh, 0, 0)),
            pl.BlockSpec((1, s, dv2), lambda hh, i: (hh, 0, 0)),
        ],
        out_specs=pl.BlockSpec((t, dv2 // 2), lambda hh, i: (i, hh)),
        out_shape=jax.ShapeDtypeStruct((s, h * dv2 // 2), BF16),
        scratch_shapes=[pltpu.VMEM((t, t), F32), pltpu.VMEM((t, t), F32), pltpu.VMEM((t, LANES), F32),
                        pltpu.VMEM((t, dv2), F32)],
        compiler_params=_cparams("parallel", "arbitrary"),
        name="mla_attention",
    )(q, k, v)


def _bdot(a, b):
    return lax.dot_general(a, b, (((2,), (1,)), ((0,), (0,))), preferred_element_type=F32)


def _bdot_nt(a, b):
    return lax.dot_general(a, b, (((2,), (2,)), ((0,), (0,))), preferred_element_type=F32)


def _bdot_tn(a, b):
    return lax.dot_general(a, b, (((1,), (1,)), ((0,), (0,))), preferred_element_type=F32)


def _unit_lower_inverse(low, eye):
    n = low.shape[-1]
    p = low
    r = eye - low
    for _ in range(int(math.log2(n)) - 1):
        pb = p.astype(BF16)
        p = _bdot(pb, pb)
        r = r + _bdot(r.astype(BF16), p.astype(BF16))
    return r


def _gdn_prep_kernel(q_ref, k_ref, v_ref, qh_ref, kh_ref, vh_ref, sm_ref, cw_ref, alog_ref, dtb_ref,
                     qd_out, kd_out, u_out, w_out, a_out, gc_out, hist_ref):
    c, d, nh = CHUNK, GDN_HEAD_DIM, GDN_HEADS
    tm = q_ref.shape[0]
    nc = tm // c
    halo = qh_ref.shape[0]

    first = pl.program_id(0) == 0
    for n, (cur, prev) in enumerate(((q_ref, qh_ref), (k_ref, kh_ref), (v_ref, vh_ref))):
        lanes = slice(n * GDN_DIM, (n + 1) * GDN_DIM)
        hist_ref[pl.ds(0, halo), lanes] = jnp.where(first, 0.0, prev[...].astype(F32))
        hist_ref[pl.ds(halo, tm), lanes] = cur[...].astype(F32)
    base = halo - (GDN_CONV - 1)
    y = hist_ref[pl.ds(base, tm), :] * cw_ref[0:1, :]
    for j in range(1, GDN_CONV):
        y = y + hist_ref[pl.ds(base + j, tm), :] * cw_ref[j:j + 1, :]
    y = _silu(y)

    sm = sm_ref[...]
    beta_all = jax.nn.sigmoid(sm)
    g_all = -jnp.exp(alog_ref[...]) * _softplus(sm + dtb_ref[...])
    row = lax.broadcasted_iota(jnp.int32, (tm, tm), 0)
    col = lax.broadcasted_iota(jnp.int32, (tm, tm), 1)
    tri = jnp.where((col <= row) & (row // c == col // c), 1.0, 0.0).astype(BF16)
    gc_all = _cumsum_rows(tri, g_all)
    gc_out[...] = gc_all
    zpad = jnp.zeros((LANES - c, LANES), F32)
    gc_t = [jnp.concatenate([gc_all[ci * c:(ci + 1) * c], zpad], axis=0).T for ci in range(nc)]

    def heads(arr, off):
        return jnp.stack([arr[ci * c:(ci + 1) * c, off + h * d:off + (h + 1) * d]
                          for ci in range(nc) for h in range(nh)])

    def cols(arr, off):
        return jnp.stack([arr[ci * c:(ci + 1) * c, off + h:off + h + 1] for ci in range(nc) for h in range(nh)])

    q = heads(y, 0)
    k = heads(y, GDN_DIM)
    v = heads(y, 2 * GDN_DIM)
    q = q * lax.rsqrt(jnp.sum(q * q, axis=-1, keepdims=True) + EPS) * (1.0 / math.sqrt(d))
    k = k * lax.rsqrt(jnp.sum(k * k, axis=-1, keepdims=True) + EPS)
    beta = cols(beta_all, 0)
    gc = cols(gc_all, nh)
    gcr = jnp.stack([gc_t[ci][nh + h:nh + h + 1, 0:c] for ci in range(nc) for h in range(nh)])
    r2 = lax.broadcasted_iota(jnp.int32, (c, c), 0)
    c2 = lax.broadcasted_iota(jnp.int32, (c, c), 1)
    decay = jnp.exp(jnp.where(c2 <= r2, gc - gcr, -jnp.inf))
    eye = jnp.where(c2 == r2, 1.0, 0.0).astype(F32)
    kbeta = k * beta
    kb16 = k.astype(BF16)
    low = jnp.where(c2 < r2, _bdot_nt(kbeta.astype(BF16), kb16) * decay, 0.0)
    t_inv = _unit_lower_inverse(low, eye)
    egc = jnp.exp(gc)
    rhs = jnp.concatenate([v * beta, kbeta * egc], axis=-1)
    sol = _bdot(t_inv.astype(BF16), rhs.astype(BF16))
    intra = _bdot_nt(q.astype(BF16), kb16) * decay
    g_last = gcr[:, :, c - 1:c]
    k_dec = k * jnp.exp(g_last - gc)
    q_dec = q * egc
    for ci in range(nc):
        rows = slice(ci * c, (ci + 1) * c)
        for h in range(nh):
            b = ci * nh + h
            lanes = slice(h * d, (h + 1) * d)
            qd_out[rows, lanes] = q_dec[b].astype(BF16)
            kd_out[rows, lanes] = k_dec[b].astype(BF16)
            u_out[rows, lanes] = sol[b, :, :d]
            w_out[rows, lanes] = sol[b, :, d:].astype(BF16)
            a_out[h, rows, :] = intra[b].astype(BF16)


def _gdn_prep(main, small, conv_w, alog, dtb, *, tm, halo=8):
    s = main.shape[0]
    full = lambda a: pl.BlockSpec(a.shape, lambda i: (0,) * a.ndim)
    blk = lambda jcol: pl.BlockSpec((tm, GDN_DIM), lambda i: (i, jcol))
    prev = lambda jcol: pl.BlockSpec((halo, GDN_DIM), lambda i: (jnp.maximum(i * (tm // halo) - 1, 0), jcol))
    act = jax.ShapeDtypeStruct((s, GDN_DIM), BF16)
    return pl.pallas_call(
        _gdn_prep_kernel,
        grid=(s // tm,),
        in_specs=[blk(1), blk(2), blk(3), prev(1), prev(2), prev(3), pl.BlockSpec((tm, LANES), lambda i: (i, 2)),
                  full(conv_w), full(alog), full(dtb)],
        out_specs=[blk(0), blk(0), blk(0), blk(0),
                   pl.BlockSpec((GDN_HEADS, tm, CHUNK), lambda i: (0, i, 0)),
                   pl.BlockSpec((tm, LANES), lambda i: (i, 0))],
        out_shape=[act, act, jax.ShapeDtypeStruct((s, GDN_DIM), F32), act,
                   jax.ShapeDtypeStruct((GDN_HEADS, s, CHUNK), BF16), jax.ShapeDtypeStruct((s, LANES), F32)],
        scratch_shapes=[pltpu.VMEM((tm + halo, 3 * GDN_DIM), F32)],
        compiler_params=_cparams("parallel"),
        name="gdn_prep",
    )(main, main, main, main, main, main, small, conv_w, alog, dtb)


def _gdn_scan_kernel(qd_ref, kd_ref, u_ref, w_ref, a_ref, gc_ref, z_ref, gn_ref, o_ref, state_ref):
    c, d, nh = CHUNK, GDN_HEAD_DIM, GDN_HEADS

    @pl.when(pl.program_id(0) == 0)
    def _():
        state_ref[...] = jnp.zeros(state_ref.shape, F32)

    def heads(ref, rows):
        return jnp.stack([ref[rows, h * d:(h + 1) * d] for h in range(nh)])

    for ci in range(qd_ref.shape[0] // c):
        rows = slice(ci * c, (ci + 1) * c)
        last = ci * c + c - 1
        g_last = jnp.stack([gc_ref[last:last + 1, nh + h:nh + h + 1] for h in range(nh)])
        st = state_ref[...]
        stb = st.astype(BF16)
        v_new = heads(u_ref, rows) - _bdot(heads(w_ref, rows), stb)
        v_newb = v_new.astype(BF16)
        o = _bdot(heads(qd_ref, rows), stb) + _bdot(a_ref[:, rows, :], v_newb)
        state_ref[...] = st * jnp.exp(g_last) + _bdot_tn(heads(kd_ref, rows), v_newb)
        o = _rms(o) * gn_ref[...]
        for h in range(nh):
            lanes = slice(h * d, (h + 1) * d)
            o_ref[rows, lanes] = (o[h] * _silu(z_ref[rows, lanes].astype(F32))).astype(o_ref.dtype)


def _gdn_scan(qd, kd, u, w, a, gc, main, gnorm, *, tm):
    s = qd.shape[0]
    blk = lambda jcol: pl.BlockSpec((tm, GDN_DIM), lambda i: (i, jcol))
    return pl.pallas_call(
        _gdn_scan_kernel,
        grid=(s // tm,),
        in_specs=[blk(0), blk(0), blk(0), blk(0),
                  pl.BlockSpec((GDN_HEADS, tm, CHUNK), lambda i: (0, i, 0)),
                  pl.BlockSpec((tm, LANES), lambda i: (i, 0)),
                  blk(4), pl.BlockSpec(gnorm.shape, lambda i: (0, 0))],
        out_specs=blk(0),
        out_shape=jax.ShapeDtypeStruct((s, GDN_DIM), BF16),
        scratch_shapes=[pltpu.VMEM((GDN_HEADS, GDN_HEAD_DIM, GDN_HEAD_DIM), F32)],
        compiler_params=_cparams("arbitrary"),
        name="gdn_scan",
    )(qd, kd, u, w, a, gc, main, gnorm)


def _gla_kernel(q_ref, k_ref, v_ref, r_ref, sm_ref, wg_ref, bg_ref, gn_ref, o_ref, state_ref):
    c = CHUNK
    dk = q_ref.shape[1] // GLA_HEADS
    dv = v_ref.shape[1] // GLA_HEADS

    @pl.when(pl.program_id(0) == 0)
    def _():
        state_ref[...] = jnp.zeros(state_ref.shape, F32)

    a1, a2, _ = _split3(sm_ref[...])
    w1, w2, _ = _split3(wg_ref[...])
    pre = _dot(a1, w1) + _dot(a1, w2) + _dot(a2, w1) + bg_ref[...]
    gk = (jnp.minimum(pre, 0.0) - jnp.log1p(jnp.exp(-jnp.abs(pre)))) * (1.0 / GLA_GATE_NORMALIZER)
    row = lax.broadcasted_iota(jnp.int32, (c, c), 0)
    col = lax.broadcasted_iota(jnp.int32, (c, c), 1)
    tri = jnp.where(col <= row, 1.0, 0.0).astype(BF16)
    b_all = _cumsum_rows(tri, gk)
    sub_row = lax.broadcasted_iota(jnp.int32, (GLA_SUB, 1), 0)
    sub_col = lax.broadcasted_iota(jnp.int32, (GLA_SUB, GLA_SUB), 1)

    for h in range(GLA_HEADS):
        bh = b_all[:, h * dk:(h + 1) * dk]
        qh = q_ref[:, h * dk:(h + 1) * dk].astype(F32) * (1.0 / math.sqrt(dk))
        kh = k_ref[:, h * dk:(h + 1) * dk].astype(F32)
        vh = v_ref[:, h * dv:(h + 1) * dv]
        st = state_ref[h]
        o_inter = _dot_nt((qh * jnp.exp(bh)).astype(BF16), st.astype(BF16))
        o_rows = []
        for blk in range(c // GLA_SUB):
            r0 = blk * GLA_SUB
            qs, bs = qh[r0:r0 + GLA_SUB], bh[r0:r0 + GLA_SUB]
            ks = kh[r0:r0 + GLA_SUB]
            a_diag = jnp.zeros((GLA_SUB, GLA_SUB), F32)
            for j in range(GLA_SUB):
                e = jnp.exp(jnp.where(sub_row >= j, bs - bs[j:j + 1], -jnp.inf))
                dj = jnp.sum(qs * ks[j:j + 1] * e, axis=-1, keepdims=True)
                a_diag = jnp.where(sub_col == j, dj, a_diag)
            o_blk = _dot(a_diag.astype(BF16), vh[r0:r0 + GLA_SUB])
            if blk > 0:
                b_ref0 = bs[0:1]
                q_t = qs * jnp.exp(bs - b_ref0)
                k_t = kh[0:r0] * jnp.exp(b_ref0 - bh[0:r0])
                a_off = _dot_nt(q_t.astype(BF16), k_t.astype(BF16))
                o_blk = o_blk + _dot(a_off.astype(BF16), vh[0:r0])
            o_rows.append(o_blk)
        o = o_inter + jnp.concatenate(o_rows, axis=0)
        b_last = bh[c - 1:c]
        k_dec = kh * jnp.exp(b_last - bh)
        state_ref[h] = st * jnp.exp(b_last) + _dot_tn(vh, k_dec.astype(BF16))
        o = _rms(o) * gn_ref[...] * _silu(r_ref[:, h * dv:(h + 1) * dv].astype(F32))
        o_ref[:, h * dv:(h + 1) * dv] = o.astype(o_ref.dtype)


def _gla(main, small, wg, bg, gnorm, *, key_dim, value_dim):
    s = main.shape[0]
    full = lambda a: pl.BlockSpec(a.shape, lambda i: (0,) * a.ndim)
    return pl.pallas_call(
        _gla_kernel,
        grid=(s // CHUNK,),
        in_specs=[
            pl.BlockSpec((CHUNK, key_dim), lambda i: (i, 0)),
            pl.BlockSpec((CHUNK, key_dim), lambda i: (i, 1)),
            pl.BlockSpec((CHUNK, value_dim), lambda i: (i, 1)),
            pl.BlockSpec((CHUNK, value_dim), lambda i: (i, 2)),
            pl.BlockSpec((CHUNK, LANES), lambda i: (i, 0)),
            full(wg), full(bg), full(gnorm),
        ],
        out_specs=pl.BlockSpec((CHUNK, value_dim), lambda i: (i, 0)),
        out_shape=jax.ShapeDtypeStruct((s, value_dim), BF16),
        scratch_shapes=[pltpu.VMEM((GLA_HEADS, value_dim // GLA_HEADS, key_dim // GLA_HEADS), F32)],
        compiler_params=_cparams("arbitrary"),
        name="gated_linear_attention",
    )(main, main, main, main, small, wg, bg, gnorm)


def _out_proj_kernel(a1_ref, a2_ref, w_ref, x_ref, gate_ref, o_ref):
    half = a1_ref.shape[1]
    mix = _dot(a1_ref[...], w_ref[0:half, :]) + _dot(a2_ref[...], w_ref[half:, :])
    o_ref[...] = x_ref[...] + gate_ref[...] * mix


def _out_proj(a1, a2, col1, col2, w, x, gate, *, tm):
    s, d = x.shape
    half = w.shape[0] // 2
    return pl.pallas_call(
        _out_proj_kernel,
        grid=(s // tm,),
        in_specs=[
            pl.BlockSpec((tm, half), lambda i: (i, col1)),
            pl.BlockSpec((tm, half), lambda i: (i, col2)),
            pl.BlockSpec(w.shape, lambda i: (0, 0)),
            pl.BlockSpec((tm, d), lambda i: (i, 0)),
            pl.BlockSpec((1, d), lambda i: (0, 0)),
        ],
        out_specs=pl.BlockSpec((tm, d), lambda i: (i, 0)),
        out_shape=jax.ShapeDtypeStruct((s, d), F32),
        compiler_params=_cparams("parallel"),
        name="out_proj_residual",
    )(a1, a2, w, x, gate)


def _ffn_kernel(x_ref, g_ref, sc_ref, sh_ref, gate_ref, fin_ref, w1_ref, w3_ref, w2_ref, o_ref, h_ref, acc_ref,
                *, final_norm):
    j = pl.program_id(1)

    @pl.when(j == 0)
    def _():
        h_ref[...] = _modulated_norm(x_ref[...], g_ref[...], sc_ref[...], sh_ref[...]).astype(BF16)
        acc_ref[...] = jnp.zeros(acc_ref.shape, F32)

    h = h_ref[...]
    a = _dot(h, w1_ref[...])
    t = (_silu(a) * _dot(h, w3_ref[...])).astype(BF16)
    acc_ref[...] += _dot(t, w2_ref[...])

    @pl.when(j == pl.num_programs(1) - 1)
    def _():
        y = x_ref[...] + gate_ref[...] * acc_ref[...]
        if final_norm:
            y = _rms(y) * fin_ref[...]
        o_ref[...] = y


def _ffn(x, g, sc, sh, gate, fin, w1, w3, w2, *, tm, tf, final_norm):
    s, d = x.shape
    dff = w1.shape[1]
    vec = pl.BlockSpec((1, d), lambda i, j: (0, 0))
    return pl.pallas_call(
        functools.partial(_ffn_kernel, final_norm=final_norm),
        grid=(s // tm, dff // tf),
        in_specs=[
            pl.BlockSpec((tm, d), lambda i, j: (i, 0)),
            vec, vec, vec, vec, vec,
            pl.BlockSpec((d, tf), lambda i, j: (0, j)),
            pl.BlockSpec((d, tf), lambda i, j: (0, j)),
            pl.BlockSpec((tf, d), lambda i, j: (j, 0)),
        ],
        out_specs=pl.BlockSpec((tm, d), lambda i, j: (i, 0)),
        out_shape=jax.ShapeDtypeStruct((s, d), F32),
        scratch_shapes=[pltpu.VMEM((tm, d), BF16), pltpu.VMEM((tm, d), F32)],
        compiler_params=_cparams("parallel", "arbitrary"),
        name="swiglu_ffn",
    )(x, g, sc, sh, gate, fin, w1, w3, w2)


def _rot_cols(w):
    half = w.shape[-1] // 2
    return jnp.concatenate([-w[..., half:], w[..., :half]], axis=-1)


def _pad_cols(w, n):
    return jnp.pad(w, [(0, 0)] * (w.ndim - 1) + [(0, n - w.shape[-1])])


def _even_in_weights(w_in):
    d = w_in.shape[0]
    o_kr = MLA_Q_RANK + MLA_KV_RANK
    o_qkv = o_kr + MLA_ROPE
    o_ba = o_qkv + 4 * GDN_DIM
    main = jnp.concatenate([w_in[:, :o_kr], w_in[:, o_qkv:o_ba]], axis=1)
    kr = w_in[:, o_kr:o_qkv]
    small = jnp.concatenate([_pad_cols(kr, LANES), _pad_cols(_rot_cols(kr), LANES), _pad_cols(w_in[:, o_ba:], LANES)],
                            axis=1)
    return main.astype(BF16), small.astype(BF16)


def _mla_weights(w_uq, w_ukv):
    r = w_uq.shape[0]
    wq = w_uq.reshape(r, MLA_HEADS, MLA_QK)
    nope, rope = wq[..., :MLA_NOPE], wq[..., MLA_NOPE:]
    flat = lambda t: t.reshape(r, MLA_HEADS * LANES)
    wq_all = jnp.concatenate([flat(nope), flat(_pad_cols(rope, LANES)), flat(_pad_cols(_rot_cols(rope), LANES))], axis=1)
    wkv = w_ukv.reshape(w_ukv.shape[0], MLA_HEADS, MLA_NOPE + MLA_V)
    wkv_all = jnp.concatenate([flat(wkv[..., :MLA_NOPE]), flat(wkv[..., MLA_NOPE:])], axis=1)
    return wq_all.astype(BF16), wkv_all.astype(BF16)


def kernel(x, c, positions, norm_g, ada_w, ada_b, ab_w_in, mla_q_norm, mla_w_uq, mla_kv_norm, mla_w_ukv, gdn_conv_w,
           gdn_a_log, gdn_dt_bias, gdn_norm, ab_w_out, gla_w_in, gla_w_gk2, gla_b_gk2, gla_norm, gla_w_out, ffn_w1,
           ffn_w3, ffn_w2, final_norm):
    batch, s, d = x.shape
    assert batch == 1 and c.shape == (1, d)
    depth = norm_g.shape[0]
    key_dim = gla_w_gk2.shape[2]
    value_dim = gla_w_out.shape[1]

    mods = _ada_modulation(c.reshape(d, 1), ada_w.reshape(depth * 2, d, 3 * d), ada_b.reshape(depth * 2, 1, 3 * d))
    xs = x.reshape(s, d)
    pos_col = positions.reshape(s, 1)
    half = MLA_ROPE // 2
    inv_freq = ROPE_THETA ** (-jnp.arange(half, dtype=F32) / half)
    invf = jnp.tile(inv_freq, LANES // half).reshape(1, LANES)
    row = lambda v: v.reshape(1, -1)

    for layer in range(depth):
        i = layer // 2
        m = mods[2 * layer]
        shift, scale, gate = m[:, :d], m[:, d:2 * d], m[:, 2 * d:]
        gain = row(norm_g[layer, 0])
        if layer % 2 == 0:
            w_main, w_small = _even_in_weights(ab_w_in[i])
            main, small = _norm_proj(xs, gain, scale, shift, w_main, w_small, tm=512, tn=640)
            wq, wkv = _mla_weights(mla_w_uq[i], mla_w_ukv[i])
            q, k, v = _mla_up(main, small, pos_col, invf, row(mla_q_norm[i]), row(mla_kv_norm[i]), wq, wkv, tm=256)
            o_a = _causal_attention(q, k, v, t=512)
            lane_pad = lambda vec: jnp.pad(vec, (GDN_HEADS, LANES - 2 * GDN_HEADS)).reshape(1, LANES)
            qd, kd, u, w, a, gc = _gdn_prep(main, small, gdn_conv_w[i], lane_pad(gdn_a_log[i]),
                                            lane_pad(gdn_dt_bias[i]), tm=256)
            o_b = _gdn_scan(qd, kd, u, w, a, gc, main, row(gdn_norm[i]), tm=256)
            xs = _out_proj(o_a, o_b, 0, 0, ab_w_out[i].astype(BF16), xs, gate, tm=256)
        else:
            w_in = gla_w_in[i]
            n_main = 2 * key_dim + 2 * value_dim
            main, small = _norm_proj(xs, gain, scale, shift, w_in[:, :n_main].astype(BF16),
                                     _pad_cols(w_in[:, n_main:], LANES).astype(BF16), tm=512, tn=768)
            wg = jnp.pad(gla_w_gk2[i], ((0, LANES - GLA_GATE_RANK), (0, 0)))
            o_c = _gla(main, small, wg, row(gla_b_gk2[i]), row(gla_norm[i]), key_dim=key_dim, value_dim=value_dim)
            xs = _out_proj(o_c, o_c, 0, 1, gla_w_out[i].astype(BF16), xs, gate, tm=256)
        m = mods[2 * layer + 1]
        shift, scale, gate = m[:, :d], m[:, d:2 * d], m[:, 2 * d:]
        xs = _ffn(xs, row(norm_g[layer, 1]), scale, shift, gate, row(final_norm), ffn_w1[layer].astype(BF16),
                  ffn_w3[layer].astype(BF16), ffn_w2[layer].astype(BF16), tm=512, tf=512,
                  final_norm=(layer == depth - 1))
    return xs.reshape(batch, s, d)
```

```python
import functools
import math

import jax
import jax.numpy as jnp
from jax import lax
from jax.experimental import pallas as pl
from jax.experimental.pallas import tpu as pltpu

F32 = jnp.float32
BF16 = jnp.bfloat16

EPS = 1e-6
ROPE_THETA = 10000.0
MLA_HEADS = 8
MLA_NOPE = 128
MLA_ROPE = 64
MLA_V = 128
MLA_QK = MLA_NOPE + MLA_ROPE
MLA_QK_PAD = 256
MLA_Q_RANK = 512
MLA_KV_RANK = 512
GDN_HEADS = 8
GDN_HEAD_DIM = 128
GDN_DIM = GDN_HEADS * GDN_HEAD_DIM
GDN_CONV = 4
GLA_HEADS = 4
GLA_GATE_RANK = 16
GLA_GATE_NORMALIZER = 16.0
GLA_SUB = 16
GLA_FACTOR_RANGE = 60.0
CHUNK = 64
LANES = 128
VMEM_LIMIT = 48 * 1024 * 1024


def _cparams(*sem):
    return pltpu.CompilerParams(dimension_semantics=sem, vmem_limit_bytes=VMEM_LIMIT)


def _dot(a, b):
    return jnp.dot(a, b, preferred_element_type=F32)


def _dot_nt(a, b):
    return lax.dot_general(a, b, (((1,), (1,)), ((), ())), preferred_element_type=F32)


def _dot_tn(a, b):
    return lax.dot_general(a, b, (((0,), (0,)), ((), ())), preferred_element_type=F32)


def _split3(x):
    x1 = x.astype(BF16)
    r1 = x - x1.astype(F32)
    x2 = r1.astype(BF16)
    x3 = (r1 - x2.astype(F32)).astype(BF16)
    return x1, x2, x3


def _cumsum_rows(tri, x):
    x1, x2, x3 = _split3(x)
    return _dot(tri, x1) + _dot(tri, x2) + _dot(tri, x3)


def _silu(x):
    return x * jax.nn.sigmoid(x)


def _softplus(x):
    return jnp.maximum(x, 0.0) + jnp.log1p(jnp.exp(-jnp.abs(x)))


def _rms(x):
    return x * lax.rsqrt(jnp.mean(x * x, axis=-1, keepdims=True) + EPS)


def _ada_kernel(c_ref, w_ref, b_ref, o_ref):
    s = _silu(c_ref[...])
    o_ref[0] = jnp.sum(w_ref[0] * s, axis=0, keepdims=True) + b_ref[0]


def _ada_modulation(c_col, w, b, *, tn=768):
    n_mod, d, n = w.shape
    return pl.pallas_call(
        _ada_kernel,
        grid=(n_mod, n // tn),
        in_specs=[
            pl.BlockSpec((d, 1), lambda l, j: (0, 0)),
            pl.BlockSpec((1, d, tn), lambda l, j: (l, 0, j)),
            pl.BlockSpec((1, 1, tn), lambda l, j: (l, 0, j)),
        ],
        out_specs=pl.BlockSpec((1, 1, tn), lambda l, j: (l, 0, j)),
        out_shape=jax.ShapeDtypeStruct((n_mod, 1, n), F32),
        compiler_params=_cparams("parallel", "parallel"),
        name="ada_modulation",
    )(c_col, w, b)


def _modulated_norm(x, g, sc, sh):
    return _rms(x) * g * (1.0 + sc) + sh


def _norm_proj_kernel(x_ref, g_ref, sc_ref, sh_ref, w_ref, ws_ref, o_ref, os_ref, h_ref):
    @pl.when(pl.program_id(1) == 0)
    def _():
        h = _modulated_norm(x_ref[...], g_ref[...], sc_ref[...], sh_ref[...]).astype(BF16)
        h_ref[...] = h
        os_ref[...] = _dot(h, ws_ref[...])

    o_ref[...] = _dot(h_ref[...], w_ref[...]).astype(o_ref.dtype)


def _norm_proj(x, g, sc, sh, w, ws, layer, n, *, tm, tn):
    s, d = x.shape
    ns = ws.shape[2]
    vec = pl.BlockSpec((1, d), lambda i, j: (0, 0))
    return pl.pallas_call(
        _norm_proj_kernel,
        grid=(s // tm, n // tn),
        in_specs=[
            pl.BlockSpec((tm, d), lambda i, j: (i, 0)),
            vec, vec, vec,
            pl.BlockSpec((None, d, tn), lambda i, j: (layer, 0, j)),
            pl.BlockSpec((None, d, ns), lambda i, j: (layer, 0, 0)),
        ],
        out_specs=[
            pl.BlockSpec((tm, tn), lambda i, j: (i, j)),
            pl.BlockSpec((tm, ns), lambda i, j: (i, 0)),
        ],
        out_shape=[jax.ShapeDtypeStruct((s, n), BF16), jax.ShapeDtypeStruct((s, ns), F32)],
        scratch_shapes=[pltpu.VMEM((tm, d), BF16)],
        compiler_params=_cparams("parallel", "arbitrary"),
        name="norm_proj",
    )(x, g, sc, sh, w, ws)


def _mla_up_kernel(cq_ref, ckv_ref, kr_ref, krr_ref, pos_ref, invf_ref, qn_ref, kvn_ref, wq_ref, wkv_ref,
                   q_out, k_out, v_out):
    hd = MLA_HEADS * LANES
    cqn = (_rms(cq_ref[...].astype(F32)) * qn_ref[...]).astype(BF16)
    ckvn = (_rms(ckv_ref[...].astype(F32)) * kvn_ref[...]).astype(BF16)
    qall = _dot(cqn, wq_ref[...])
    kv = _dot(ckvn, wkv_ref[...])
    ang = pos_ref[...].astype(F32) * invf_ref[...]
    cos, sin = jnp.cos(ang), jnp.sin(ang)
    k_rope = kr_ref[...] * cos + krr_ref[...] * sin
    scale = math.log2(math.e) / math.sqrt(MLA_QK)
    ones = jnp.ones((cq_ref.shape[0], MLA_V), F32)
    for h in range(MLA_HEADS):
        lo, hi = h * LANES, (h + 1) * LANES
        q_rope = qall[:, hd + lo:hd + hi] * cos + qall[:, 2 * hd + lo:2 * hd + hi] * sin
        q_out[h] = (jnp.concatenate([qall[:, lo:hi], q_rope], axis=-1) * scale).astype(BF16)
        k_out[h] = jnp.concatenate([kv[:, lo:hi], k_rope], axis=-1).astype(BF16)
        v_out[h] = jnp.concatenate([kv[:, hd + lo:hd + hi], ones], axis=-1).astype(BF16)


def _mla_up(main, small, pos_col, invf, qn, kvn, wq, wkv, *, tm):
    s = main.shape[0]
    full = lambda a: pl.BlockSpec(a.shape, lambda i: (0,) * a.ndim)
    qk_shape = jax.ShapeDtypeStruct((MLA_HEADS, s, MLA_QK_PAD), BF16)
    return pl.pallas_call(
        _mla_up_kernel,
        grid=(s // tm,),
        in_specs=[
            pl.BlockSpec((tm, MLA_Q_RANK), lambda i: (i, 0)),
            pl.BlockSpec((tm, MLA_KV_RANK), lambda i: (i, 1)),
            pl.BlockSpec((tm, LANES), lambda i: (i, 0)),
            pl.BlockSpec((tm, LANES), lambda i: (i, 1)),
            pl.BlockSpec((tm, 1), lambda i: (i, 0)),
            full(invf), full(qn), full(kvn), full(wq), full(wkv),
        ],
        out_specs=[
            pl.BlockSpec((MLA_HEADS, tm, MLA_QK_PAD), lambda i: (0, i, 0)),
            pl.BlockSpec((MLA_HEADS, tm, MLA_QK_PAD), lambda i: (0, i, 0)),
            pl.BlockSpec((MLA_HEADS, tm, 2 * MLA_V), lambda i: (0, i, 0)),
        ],
        out_shape=[qk_shape, qk_shape, jax.ShapeDtypeStruct((MLA_HEADS, s, 2 * MLA_V), BF16)],
        compiler_params=_cparams("parallel"),
        name="mla_up",
    )(main, main, small, small, pos_col, invf, qn, kvn, wq, wkv)


def _attn_kernel(q_ref, k_ref, v_ref, o_ref, s0_ref, s1_ref, m_ref, acc_ref, *, t):
    i = pl.program_id(1)
    q = q_ref[0]
    dv2 = v_ref.shape[-1]
    dv = dv2 // 2

    def tile(j):
        return pl.ds(pl.multiple_of(j * t, t), t)

    def scores(s_ref, j):
        s_ref[...] = _dot_nt(q, k_ref[0, tile(j), :])

    def update(s_ref, j, diagonal):
        s = s_ref[...]
        if diagonal:
            row = lax.broadcasted_iota(jnp.int32, (t, t), 0)
            col = lax.broadcasted_iota(jnp.int32, (t, t), 1)
            s = jnp.where(col <= row, s, -jnp.inf)
        m = m_ref[...]
        m_new = jnp.maximum(m, jnp.max(s, axis=1, keepdims=True))
        p = jnp.exp2(s - jnp.concatenate([m_new] * (t // LANES), axis=1))
        alpha = jnp.exp2(m - m_new)
        acc_ref[...] = (jnp.concatenate([alpha] * (dv2 // LANES), axis=1) * acc_ref[...]
                        + _dot(p.astype(BF16), v_ref[0, tile(j), :]))
        m_ref[...] = m_new

    m_ref[...] = jnp.full(m_ref.shape, -jnp.inf, F32)
    acc_ref[...] = jnp.zeros(acc_ref.shape, F32)
    scores(s0_ref, 0)

    def pair(jj, carry):
        j = 2 * jj
        scores(s1_ref, j + 1)
        update(s0_ref, j, False)
        scores(s0_ref, j + 2)
        update(s1_ref, j + 1, False)
        return carry

    lax.fori_loop(0, i // 2, pair, 0)

    @pl.when(i % 2 == 0)
    def _():
        update(s0_ref, i, True)

    @pl.when(i % 2 == 1)
    def _():
        scores(s1_ref, i)
        update(s0_ref, i - 1, False)
        update(s1_ref, i, True)

    acc = acc_ref[...]
    o_ref[...] = (acc[:, :dv] / acc[:, dv:]).astype(o_ref.dtype)


def _causal_attention(q, k, v, *, t):
    h, s, dqk = q.shape
    dv2 = v.shape[-1]
    return pl.pallas_call(
        functools.partial(_attn_kernel, t=t),
        grid=(h, s // t),
        in_specs=[
            pl.BlockSpec((1, t, dqk), lambda hh, i: (hh, i, 0)),
            pl.BlockSpec((1, s, dqk), lambda hh, i: (hh, 0, 0)),
            pl.BlockSpec((1, s, dv2), lambda hh, i: (hh, 0, 0)),
        ],
        out_specs=pl.BlockSpec((t, dv2 // 2), lambda hh, i: (i, hh)),
        out_shape=jax.ShapeDtypeStruct((s, h * dv2 // 2), BF16),
        scratch_shapes=[pltpu.VMEM((t, t), F32), pltpu.VMEM((t, t), F32), pltpu.VMEM((t, LANES), F32),
                        pltpu.VMEM((t, dv2), F32)],
        compiler_params=_cparams("parallel", "arbitrary"),
        name="mla_attention",
    )(q, k, v)


def _bdot(a, b):
    return lax.dot_general(a, b, (((2,), (1,)), ((0,), (0,))), preferred_element_type=F32)


def _bdot_nt(a, b):
    return lax.dot_general(a, b, (((2,), (2,)), ((0,), (0,))), preferred_element_type=F32)


def _bdot_tn(a, b):
    return lax.dot_general(a, b, (((1,), (1,)), ((0,), (0,))), preferred_element_type=F32)


def _unit_lower_inverse(low, eye):
    n = low.shape[-1]
    p = low
    r = eye - low
    for _ in range(int(math.log2(n)) - 1):
        pb = p.astype(BF16)
        p = _bdot(pb, pb)
        r = r + _bdot(r.astype(BF16), p.astype(BF16))
    return r


def _gdn_prep_kernel(q_ref, k_ref, v_ref, qh_ref, kh_ref, vh_ref, sm_ref, cw_ref, alog_ref, dtb_ref,
                     qd_out, kd_out, u_out, w_out, a_out, gc_out, hist_ref):
    c, d, nh = CHUNK, GDN_HEAD_DIM, GDN_HEADS
    tm = q_ref.shape[0]
    nc = tm // c
    halo = qh_ref.shape[0]

    first = pl.program_id(0) == 0
    for n, (cur, prev) in enumerate(((q_ref, qh_ref), (k_ref, kh_ref), (v_ref, vh_ref))):
        lanes = slice(n * GDN_DIM, (n + 1) * GDN_DIM)
        hist_ref[pl.ds(0, halo), lanes] = jnp.where(first, 0.0, prev[...].astype(F32))
        hist_ref[pl.ds(halo, tm), lanes] = cur[...].astype(F32)
    base = halo - (GDN_CONV - 1)
    y = hist_ref[pl.ds(base, tm), :] * cw_ref[0:1, :]
    for j in range(1, GDN_CONV):
        y = y + hist_ref[pl.ds(base + j, tm), :] * cw_ref[j:j + 1, :]
    y = _silu(y)

    sm = sm_ref[...]
    beta_all = jax.nn.sigmoid(sm)
    g_all = -jnp.exp(alog_ref[...]) * _softplus(sm + dtb_ref[...])
    row = lax.broadcasted_iota(jnp.int32, (tm, tm), 0)
    col = lax.broadcasted_iota(jnp.int32, (tm, tm), 1)
    tri = jnp.where((col <= row) & (row // c == col // c), 1.0, 0.0).astype(BF16)
    gc_all = _cumsum_rows(tri, g_all)
    gc_out[...] = gc_all
    zpad = jnp.zeros((LANES - c, LANES), F32)
    gc_t = [jnp.concatenate([gc_all[ci * c:(ci + 1) * c], zpad], axis=0).T for ci in range(nc)]

    def heads(arr, off):
        return jnp.stack([arr[ci * c:(ci + 1) * c, off + h * d:off + (h + 1) * d]
                          for ci in range(nc) for h in range(nh)])

    def cols(arr, off):
        return jnp.stack([arr[ci * c:(ci + 1) * c, off + h:off + h + 1] for ci in range(nc) for h in range(nh)])

    q = heads(y, 0)
    k = heads(y, GDN_DIM)
    v = heads(y, 2 * GDN_DIM)
    q = q * lax.rsqrt(jnp.sum(q * q, axis=-1, keepdims=True) + EPS) * (1.0 / math.sqrt(d))
    k = k * lax.rsqrt(jnp.sum(k * k, axis=-1, keepdims=True) + EPS)
    beta = cols(beta_all, 0)
    gc = cols(gc_all, nh)
    gcr = jnp.stack([gc_t[ci][nh + h:nh + h + 1, 0:c] for ci in range(nc) for h in range(nh)])
    r2 = lax.broadcasted_iota(jnp.int32, (c, c), 0)
    c2 = lax.broadcasted_iota(jnp.int32, (c, c), 1)
    decay = jnp.exp(jnp.where(c2 <= r2, gc - gcr, -jnp.inf))
    eye = jnp.where(c2 == r2, 1.0, 0.0).astype(F32)
    kbeta = k * beta
    kb16 = k.astype(BF16)
    low = jnp.where(c2 < r2, _bdot_nt(kbeta.astype(BF16), kb16) * decay, 0.0)
    t_inv = _unit_lower_inverse(low, eye)
    egc = jnp.exp(gc)
    rhs = jnp.concatenate([v * beta, kbeta * egc], axis=-1)
    sol = _bdot(t_inv.astype(BF16), rhs.astype(BF16))
    intra = _bdot_nt(q.astype(BF16), kb16) * decay
    g_last = gcr[:, :, c - 1:c]
    k_dec = k * jnp.exp(g_last - gc)
    q_dec = q * egc
    for ci in range(nc):
        rows = slice(ci * c, (ci + 1) * c)
        for h in range(nh):
            b = ci * nh + h
            lanes = slice(h * d, (h + 1) * d)
            qd_out[rows, lanes] = q_dec[b].astype(BF16)
            kd_out[rows, lanes] = k_dec[b].astype(BF16)
            u_out[rows, lanes] = sol[b, :, :d]
            w_out[rows, lanes] = sol[b, :, d:].astype(BF16)
            a_out[h, rows, :] = intra[b].astype(BF16)


def _gdn_prep(main, small, conv_w, alog, dtb, *, tm, halo=8):
    s = main.shape[0]
    full = lambda a: pl.BlockSpec(a.shape, lambda i: (0,) * a.ndim)
    blk = lambda jcol: pl.BlockSpec((tm, GDN_DIM), lambda i: (i, jcol))
    prev = lambda jcol: pl.BlockSpec((halo, GDN_DIM), lambda i: (jnp.maximum(i * (tm // halo) - 1, 0), jcol))
    act = jax.ShapeDtypeStruct((s, GDN_DIM), BF16)
    return pl.pallas_call(
        _gdn_prep_kernel,
        grid=(s // tm,),
        in_specs=[blk(1), blk(2), blk(3), prev(1), prev(2), prev(3), pl.BlockSpec((tm, LANES), lambda i: (i, 2)),
                  full(conv_w), full(alog), full(dtb)],
        out_specs=[blk(0), blk(0), blk(0), blk(0),
                   pl.BlockSpec((GDN_HEADS, tm, CHUNK), lambda i: (0, i, 0)),
                   pl.BlockSpec((tm, LANES), lambda i: (i, 0))],
        out_shape=[act, act, jax.ShapeDtypeStruct((s, GDN_DIM), F32), act,
                   jax.ShapeDtypeStruct((GDN_HEADS, s, CHUNK), BF16), jax.ShapeDtypeStruct((s, LANES), F32)],
        scratch_shapes=[pltpu.VMEM((tm + halo, 3 * GDN_DIM), F32)],
        compiler_params=_cparams("parallel"),
        name="gdn_prep",
    )(main, main, main, main, main, main, small, conv_w, alog, dtb)


def _gdn_scan_kernel(qd_ref, kd_ref, u_ref, w_ref, a_ref, gc_ref, z_ref, gn_ref, o_ref, state_ref):
    c, d, nh = CHUNK, GDN_HEAD_DIM, GDN_HEADS

    @pl.when(pl.program_id(0) == 0)
    def _():
        state_ref[...] = jnp.zeros(state_ref.shape, F32)

    def heads(ref, rows):
        return jnp.stack([ref[rows, h * d:(h + 1) * d] for h in range(nh)])

    for ci in range(qd_ref.shape[0] // c):
        rows = slice(ci * c, (ci + 1) * c)
        last = ci * c + c - 1
        g_last = jnp.stack([gc_ref[last:last + 1, nh + h:nh + h + 1] for h in range(nh)])
        st = state_ref[...]
        stb = st.astype(BF16)
        v_new = heads(u_ref, rows) - _bdot(heads(w_ref, rows), stb)
        v_newb = v_new.astype(BF16)
        o = _bdot(heads(qd_ref, rows), stb) + _bdot(a_ref[:, rows, :], v_newb)
        state_ref[...] = st * jnp.exp(g_last) + _bdot_tn(heads(kd_ref, rows), v_newb)
        o = _rms(o) * gn_ref[...]
        for h in range(nh):
            lanes = slice(h * d, (h + 1) * d)
            o_ref[rows, lanes] = (o[h] * _silu(z_ref[rows, lanes].astype(F32))).astype(o_ref.dtype)


def _gdn_scan(qd, kd, u, w, a, gc, main, gnorm, *, tm):
    s = qd.shape[0]
    blk = lambda jcol: pl.BlockSpec((tm, GDN_DIM), lambda i: (i, jcol))
    return pl.pallas_call(
        _gdn_scan_kernel,
        grid=(s // tm,),
        in_specs=[blk(0), blk(0), blk(0), blk(0),
                  pl.BlockSpec((GDN_HEADS, tm, CHUNK), lambda i: (0, i, 0)),
                  pl.BlockSpec((tm, LANES), lambda i: (i, 0)),
                  blk(4), pl.BlockSpec(gnorm.shape, lambda i: (0, 0))],
        out_specs=blk(0),
        out_shape=jax.ShapeDtypeStruct((s, GDN_DIM), BF16),
        scratch_shapes=[pltpu.VMEM((GDN_HEADS, GDN_HEAD_DIM, GDN_HEAD_DIM), F32)],
        compiler_params=_cparams("arbitrary"),
        name="gdn_scan",
    )(qd, kd, u, w, a, gc, main, gnorm)


def _gla_kernel(q_ref, k_ref, v_ref, r_ref, sm_ref, wg1_ref, wg2_ref, bg_ref, gn_ref, o_ref, state_ref):
    c = CHUNK
    dk = q_ref.shape[1] // GLA_HEADS
    dv = v_ref.shape[1] // GLA_HEADS

    @pl.when(pl.program_id(0) == 0)
    def _():
        state_ref[...] = jnp.zeros(state_ref.shape, F32)

    a1, a2, _ = _split3(sm_ref[...])
    w1, w2 = wg1_ref[...], wg2_ref[...]
    pre = _dot(a1, w1) + _dot(a1, w2) + _dot(a2, w1) + bg_ref[...]
    gk = (jnp.minimum(pre, 0.0) - jnp.log1p(jnp.exp(-jnp.abs(pre)))) * (1.0 / GLA_GATE_NORMALIZER)
    row = lax.broadcasted_iota(jnp.int32, (c, c), 0)
    col = lax.broadcasted_iota(jnp.int32, (c, c), 1)
    tri = jnp.where(col <= row, 1.0, 0.0).astype(BF16)
    b_all = _cumsum_rows(tri, gk)
    sub_row = lax.broadcasted_iota(jnp.int32, (GLA_SUB, 1), 0)
    sub_col = lax.broadcasted_iota(jnp.int32, (GLA_SUB, GLA_SUB), 1)

    def intra_factored(qh, kh, bh, vh):
        b0 = bh[0:1]
        q_t = qh * jnp.exp(bh - b0)
        k_t = kh * jnp.exp(b0 - bh)
        a = jnp.where(col <= row, _dot_nt(q_t.astype(BF16), k_t.astype(BF16)), 0.0)
        q_dec = q_t * jnp.exp(b0)
        k_dec = k_t * jnp.exp(bh[c - 1:c] - b0)
        return _dot(a.astype(BF16), vh), q_dec, k_dec

    def intra_exact(qh, kh, bh, vh):
        o_rows = []
        for blk in range(c // GLA_SUB):
            r0 = blk * GLA_SUB
            qs, bs = qh[r0:r0 + GLA_SUB], bh[r0:r0 + GLA_SUB]
            ks = kh[r0:r0 + GLA_SUB]
            a_diag = jnp.zeros((GLA_SUB, GLA_SUB), F32)
            for j in range(GLA_SUB):
                e = jnp.exp(jnp.where(sub_row >= j, bs - bs[j:j + 1], -jnp.inf))
                dj = jnp.sum(qs * ks[j:j + 1] * e, axis=-1, keepdims=True)
                a_diag = jnp.where(sub_col == j, dj, a_diag)
            o_blk = _dot(a_diag.astype(BF16), vh[r0:r0 + GLA_SUB])
            if blk > 0:
                b_ref0 = bs[0:1]
                q_t = qs * jnp.exp(bs - b_ref0)
                k_t = kh[0:r0] * jnp.exp(b_ref0 - bh[0:r0])
                a_off = _dot_nt(q_t.astype(BF16), k_t.astype(BF16))
                o_blk = o_blk + _dot(a_off.astype(BF16), vh[0:r0])
            o_rows.append(o_blk)
        return jnp.concatenate(o_rows, axis=0), qh * jnp.exp(bh), kh * jnp.exp(bh[c - 1:c] - bh)

    def chunk_step(intra):
        for h in range(GLA_HEADS):
            bh = b_all[:, h * dk:(h + 1) * dk]
            qh = q_ref[:, h * dk:(h + 1) * dk].astype(F32) * (1.0 / math.sqrt(dk))
            kh = k_ref[:, h * dk:(h + 1) * dk].astype(F32)
            vh = v_ref[:, h * dv:(h + 1) * dv]
            st = state_ref[h]
            o_intra, q_dec, k_dec = intra(qh, kh, bh, vh)
            o = _dot_nt(q_dec.astype(BF16), st.astype(BF16)) + o_intra
            state_ref[h] = st * jnp.exp(bh[c - 1:c]) + _dot_tn(vh, k_dec.astype(BF16))
            o = _rms(o) * gn_ref[...] * _silu(r_ref[:, h * dv:(h + 1) * dv].astype(F32))
            o_ref[:, h * dv:(h + 1) * dv] = o.astype(o_ref.dtype)

    decay_range = jnp.max(b_all[0:1, :] - b_all[c - 1:c, :])
    factorable = decay_range <= GLA_FACTOR_RANGE

    @pl.when(factorable)
    def _():
        chunk_step(intra_factored)

    @pl.when(jnp.logical_not(factorable))
    def _():
        chunk_step(intra_exact)


def _gla(main, small, wg, bg, gnorm, *, key_dim, value_dim):
    s = main.shape[0]
    full = lambda a: pl.BlockSpec(a.shape, lambda i: (0,) * a.ndim)
    wg1 = wg.astype(BF16)
    wg2 = (wg - wg1.astype(F32)).astype(BF16)
    return pl.pallas_call(
        _gla_kernel,
        grid=(s // CHUNK,),
        in_specs=[
            pl.BlockSpec((CHUNK, key_dim), lambda i: (i, 0)),
            pl.BlockSpec((CHUNK, key_dim), lambda i: (i, 1)),
            pl.BlockSpec((CHUNK, value_dim), lambda i: (i, 1)),
            pl.BlockSpec((CHUNK, value_dim), lambda i: (i, 2)),
            pl.BlockSpec((CHUNK, LANES), lambda i: (i, 0)),
            full(wg1), full(wg2), full(bg), full(gnorm),
        ],
        out_specs=pl.BlockSpec((CHUNK, value_dim), lambda i: (i, 0)),
        out_shape=jax.ShapeDtypeStruct((s, value_dim), BF16),
        scratch_shapes=[pltpu.VMEM((GLA_HEADS, value_dim // GLA_HEADS, key_dim // GLA_HEADS), F32)],
        compiler_params=_cparams("arbitrary"),
        name="gated_linear_attention",
    )(main, main, main, main, small, wg1, wg2, bg, gnorm)


def _out_proj_kernel(a1_ref, a2_ref, w_ref, x_ref, gate_ref, o_ref):
    half = a1_ref.shape[1]
    mix = _dot(a1_ref[...], w_ref[0:half, :]) + _dot(a2_ref[...], w_ref[half:, :])
    o_ref[...] = x_ref[...] + gate_ref[...] * mix


def _out_proj(a1, a2, col1, col2, w, layer, x, gate, *, tm):
    s, d = x.shape
    half = w.shape[1] // 2
    return pl.pallas_call(
        _out_proj_kernel,
        grid=(s // tm,),
        in_specs=[
            pl.BlockSpec((tm, half), lambda i: (i, col1)),
            pl.BlockSpec((tm, half), lambda i: (i, col2)),
            pl.BlockSpec((None,) + w.shape[1:], lambda i: (layer, 0, 0)),
            pl.BlockSpec((tm, d), lambda i: (i, 0)),
            pl.BlockSpec((1, d), lambda i: (0, 0)),
        ],
        out_specs=pl.BlockSpec((tm, d), lambda i: (i, 0)),
        out_shape=jax.ShapeDtypeStruct((s, d), F32),
        compiler_params=_cparams("parallel"),
        name="out_proj_residual",
    )(a1, a2, w, x, gate)


def _ffn_kernel(x_ref, g_ref, sc_ref, sh_ref, gate_ref, fin_ref, w1_ref, w3_ref, w2_ref, o_ref, h_ref, acc_ref,
                *, final_norm):
    j = pl.program_id(1)

    @pl.when(j == 0)
    def _():
        h_ref[...] = _modulated_norm(x_ref[...], g_ref[...], sc_ref[...], sh_ref[...]).astype(BF16)
        acc_ref[...] = jnp.zeros(acc_ref.shape, F32)

    h = h_ref[...]
    a = _dot(h, w1_ref[...])
    t = (_silu(a) * _dot(h, w3_ref[...])).astype(BF16)
    acc_ref[...] += _dot(t, w2_ref[...])

    @pl.when(j == pl.num_programs(1) - 1)
    def _():
        y = x_ref[...] + gate_ref[...] * acc_ref[...]
        if final_norm:
            y = _rms(y) * fin_ref[...]
        o_ref[...] = y


def _ffn(x, g, sc, sh, gate, fin, w1, w3, w2, layer, *, tm, tf, final_norm):
    s, d = x.shape
    dff = w1.shape[2]
    vec = pl.BlockSpec((1, d), lambda i, j: (0, 0))
    return pl.pallas_call(
        functools.partial(_ffn_kernel, final_norm=final_norm),
        grid=(s // tm, dff // tf),
        in_specs=[
            pl.BlockSpec((tm, d), lambda i, j: (i, 0)),
            vec, vec, vec, vec, vec,
            pl.BlockSpec((None, d, tf), lambda i, j: (layer, 0, j)),
            pl.BlockSpec((None, d, tf), lambda i, j: (layer, 0, j)),
            pl.BlockSpec((None, tf, d), lambda i, j: (layer, j, 0)),
        ],
        out_specs=pl.BlockSpec((tm, d), lambda i, j: (i, 0)),
        out_shape=jax.ShapeDtypeStruct((s, d), F32),
        scratch_shapes=[pltpu.VMEM((tm, d), BF16), pltpu.VMEM((tm, d), F32)],
        compiler_params=_cparams("parallel", "arbitrary"),
        name="swiglu_ffn",
    )(x, g, sc, sh, gate, fin, w1, w3, w2)


def _rot_cols(w):
    half = w.shape[-1] // 2
    return jnp.concatenate([-w[..., half:], w[..., :half]], axis=-1)


def _pad_cols(w, n):
    return jnp.pad(w, [(0, 0)] * (w.ndim - 1) + [(0, n - w.shape[-1])])


def _even_in_weights(w_in):
    o_kr = MLA_Q_RANK + MLA_KV_RANK
    o_qkv = o_kr + MLA_ROPE
    o_ba = o_qkv + 4 * GDN_DIM
    main = jnp.concatenate([w_in[..., :o_kr], w_in[..., o_qkv:o_ba]], axis=-1)
    kr = w_in[..., o_kr:o_qkv]
    small = jnp.concatenate([_pad_cols(kr, LANES), _pad_cols(_rot_cols(kr), LANES), _pad_cols(w_in[..., o_ba:], LANES)],
                            axis=-1)
    return main.astype(BF16), small.astype(BF16)


def _mla_weights(w_uq, w_ukv):
    r = w_uq.shape[0]
    wq = w_uq.reshape(r, MLA_HEADS, MLA_QK)
    nope, rope = wq[..., :MLA_NOPE], wq[..., MLA_NOPE:]
    flat = lambda t: t.reshape(r, MLA_HEADS * LANES)
    wq_all = jnp.concatenate([flat(nope), flat(_pad_cols(rope, LANES)), flat(_pad_cols(_rot_cols(rope), LANES))], axis=1)
    wkv = w_ukv.reshape(w_ukv.shape[0], MLA_HEADS, MLA_NOPE + MLA_V)
    wkv_all = jnp.concatenate([flat(wkv[..., :MLA_NOPE]), flat(wkv[..., MLA_NOPE:])], axis=1)
    return wq_all.astype(BF16), wkv_all.astype(BF16)


def kernel(x, c, positions, norm_g, ada_w, ada_b, ab_w_in, mla_q_norm, mla_w_uq, mla_kv_norm, mla_w_ukv, gdn_conv_w,
           gdn_a_log, gdn_dt_bias, gdn_norm, ab_w_out, gla_w_in, gla_w_gk2, gla_b_gk2, gla_norm, gla_w_out, ffn_w1,
           ffn_w3, ffn_w2, final_norm):
    batch, s, d = x.shape
    assert batch == 1 and c.shape == (1, d)
    depth = norm_g.shape[0]
    key_dim = gla_w_gk2.shape[2]
    value_dim = gla_w_out.shape[1]

    mods = _ada_modulation(c.reshape(d, 1), ada_w.reshape(depth * 2, d, 3 * d), ada_b.reshape(depth * 2, 1, 3 * d))
    xs = x.reshape(s, d)
    pos_col = positions.reshape(s, 1)
    half = MLA_ROPE // 2
    inv_freq = ROPE_THETA ** (-jnp.arange(half, dtype=F32) / half)
    invf = jnp.tile(inv_freq, LANES // half).reshape(1, LANES)
    row = lambda v: v.reshape(1, -1)

    n_gla = 2 * key_dim + 2 * value_dim
    ab_main_w, ab_small_w = _even_in_weights(ab_w_in)
    gla_main_w = gla_w_in.astype(BF16)
    gla_small_w = _pad_cols(gla_w_in[..., n_gla:], LANES).astype(BF16)
    ab_out_w, gla_out_w = ab_w_out.astype(BF16), gla_w_out.astype(BF16)
    w1, w3, w2 = ffn_w1.astype(BF16), ffn_w3.astype(BF16), ffn_w2.astype(BF16)

    for layer in range(depth):
        i = layer // 2
        m = mods[2 * layer]
        shift, scale, gate = m[:, :d], m[:, d:2 * d], m[:, 2 * d:]
        gain = row(norm_g[layer, 0])
        if layer % 2 == 0:
            main, small = _norm_proj(xs, gain, scale, shift, ab_main_w, ab_small_w, i, ab_main_w.shape[2],
                                     tm=512, tn=1024)
            wq, wkv = _mla_weights(mla_w_uq[i], mla_w_ukv[i])
            q, k, v = _mla_up(main, small, pos_col, invf, row(mla_q_norm[i]), row(mla_kv_norm[i]), wq, wkv, tm=256)
            o_a = _causal_attention(q, k, v, t=512)
            lane_pad = lambda vec: jnp.pad(vec, (GDN_HEADS, LANES - 2 * GDN_HEADS)).reshape(1, LANES)
            qd, kd, u, w, a, gc = _gdn_prep(main, small, gdn_conv_w[i], lane_pad(gdn_a_log[i]),
                                            lane_pad(gdn_dt_bias[i]), tm=256)
            o_b = _gdn_scan(qd, kd, u, w, a, gc, main, row(gdn_norm[i]), tm=256)
            xs = _out_proj(o_a, o_b, 0, 0, ab_out_w, i, xs, gate, tm=256)
        else:
            main, small = _norm_proj(xs, gain, scale, shift, gla_main_w, gla_small_w, i, n_gla, tm=512, tn=1024)
            wg = jnp.pad(gla_w_gk2[i], ((0, LANES - GLA_GATE_RANK), (0, 0)))
            o_c = _gla(main, small, wg, row(gla_b_gk2[i]), row(gla_norm[i]), key_dim=key_dim, value_dim=value_dim)
            xs = _out_proj(o_c, o_c, 0, 1, gla_out_w, i, xs, gate, tm=256)
        m = mods[2 * layer + 1]
        shift, scale, gate = m[:, :d], m[:, d:2 * d], m[:, 2 * d:]
        xs = _ffn(xs, row(norm_g[layer, 1]), scale, shift, gate, row(final_norm), w1, w3, w2, layer, tm=512, tf=512,
                  final_norm=(layer == depth - 1))
    return xs.reshape(batch, s, d)
```

```python
import functools
import math

import jax
import jax.numpy as jnp
from jax import lax
from jax.experimental import pallas as pl
from jax.experimental.pallas import tpu as pltpu

F32 = jnp.float32
BF16 = jnp.bfloat16

EPS = 1e-6
ROPE_THETA = 10000.0
MLA_HEADS = 8
MLA_NOPE = 128
MLA_ROPE = 64
MLA_V = 128
MLA_QK = MLA_NOPE + MLA_ROPE
MLA_QK_PAD = 256
MLA_Q_RANK = 512
MLA_KV_RANK = 512
GDN_HEADS = 8
GDN_HEAD_DIM = 128
GDN_DIM = GDN_HEADS * GDN_HEAD_DIM
GDN_CONV = 4
GLA_HEADS = 4
GLA_GATE_RANK = 16
GLA_GATE_NORMALIZER = 16.0
GLA_SUB = 16
GLA_FACTOR_RANGE = 60.0
CHUNK = 64
LANES = 128
VMEM_LIMIT = 48 * 1024 * 1024


def _cparams(*sem):
    return pltpu.CompilerParams(dimension_semantics=sem, vmem_limit_bytes=VMEM_LIMIT)


def _dot(a, b):
    return jnp.dot(a, b, preferred_element_type=F32)


def _dot_nt(a, b):
    return lax.dot_general(a, b, (((1,), (1,)), ((), ())), preferred_element_type=F32)


def _dot_tn(a, b):
    return lax.dot_general(a, b, (((0,), (0,)), ((), ())), preferred_element_type=F32)


def _split3(x):
    x1 = x.astype(BF16)
    r1 = x - x1.astype(F32)
    x2 = r1.astype(BF16)
    x3 = (r1 - x2.astype(F32)).astype(BF16)
    return x1, x2, x3


def _cumsum_rows(tri, x):
    x1, x2, x3 = _split3(x)
    return _dot(tri, x1) + _dot(tri, x2) + _dot(tri, x3)


def _silu(x):
    return x * jax.nn.sigmoid(x)


def _softplus(x):
    return jnp.maximum(x, 0.0) + jnp.log1p(jnp.exp(-jnp.abs(x)))


def _rms(x):
    return x * lax.rsqrt(jnp.mean(x * x, axis=-1, keepdims=True) + EPS)


def _ada_kernel(c_ref, w_ref, b_ref, o_ref):
    s = _silu(c_ref[...])
    o_ref[0] = jnp.sum(w_ref[0] * s, axis=0, keepdims=True) + b_ref[0]


def _ada_modulation(c_col, w, b, *, tn=768):
    n_mod, d, n = w.shape
    return pl.pallas_call(
        _ada_kernel,
        grid=(n_mod, n // tn),
        in_specs=[
            pl.BlockSpec((d, 1), lambda l, j: (0, 0)),
            pl.BlockSpec((1, d, tn), lambda l, j: (l, 0, j)),
            pl.BlockSpec((1, 1, tn), lambda l, j: (l, 0, j)),
        ],
        out_specs=pl.BlockSpec((1, 1, tn), lambda l, j: (l, 0, j)),
        out_shape=jax.ShapeDtypeStruct((n_mod, 1, n), F32),
        compiler_params=_cparams("parallel", "parallel"),
        name="ada_modulation",
    )(c_col, w, b)


def _modulated_norm(x, g, sc, sh):
    return _rms(x) * g * (1.0 + sc) + sh


def _norm_proj_kernel(x_ref, g_ref, sc_ref, sh_ref, w_ref, ws_ref, o_ref, os_ref, h_ref):
    @pl.when(pl.program_id(1) == 0)
    def _():
        h = _modulated_norm(x_ref[...], g_ref[...], sc_ref[...], sh_ref[...]).astype(BF16)
        h_ref[...] = h
        os_ref[...] = _dot(h, ws_ref[...])

    o_ref[...] = _dot(h_ref[...], w_ref[...]).astype(o_ref.dtype)


def _norm_proj(x, g, sc, sh, w, ws, layer, n, *, tm, tn):
    s, d = x.shape
    ns = ws.shape[2]
    vec = pl.BlockSpec((1, d), lambda i, j: (0, 0))
    return pl.pallas_call(
        _norm_proj_kernel,
        grid=(s // tm, n // tn),
        in_specs=[
            pl.BlockSpec((tm, d), lambda i, j: (i, 0)),
            vec, vec, vec,
            pl.BlockSpec((None, d, tn), lambda i, j: (layer, 0, j)),
            pl.BlockSpec((None, d, ns), lambda i, j: (layer, 0, 0)),
        ],
        out_specs=[
            pl.BlockSpec((tm, tn), lambda i, j: (i, j)),
            pl.BlockSpec((tm, ns), lambda i, j: (i, 0)),
        ],
        out_shape=[jax.ShapeDtypeStruct((s, n), BF16), jax.ShapeDtypeStruct((s, ns), F32)],
        scratch_shapes=[pltpu.VMEM((tm, d), BF16)],
        compiler_params=_cparams("parallel", "arbitrary"),
        name="norm_proj",
    )(x, g, sc, sh, w, ws)


def _mla_up_kernel(cq_ref, ckv_ref, kr_ref, krr_ref, pos_ref, invf_ref, qn_ref, kvn_ref, wq_ref, wkv_ref,
                   q_out, k_out, v_out):
    hd = MLA_HEADS * LANES
    cqn = (_rms(cq_ref[...].astype(F32)) * qn_ref[...]).astype(BF16)
    ckvn = (_rms(ckv_ref[...].astype(F32)) * kvn_ref[...]).astype(BF16)
    qall = _dot(cqn, wq_ref[...])
    kv = _dot(ckvn, wkv_ref[...])
    ang = pos_ref[...].astype(F32) * invf_ref[...]
    cos, sin = jnp.cos(ang), jnp.sin(ang)
    k_rope = kr_ref[...] * cos + krr_ref[...] * sin
    scale = math.log2(math.e) / math.sqrt(MLA_QK)
    ones = jnp.ones((cq_ref.shape[0], MLA_V), F32)
    for h in range(MLA_HEADS):
        lo, hi = h * LANES, (h + 1) * LANES
        q_rope = qall[:, hd + lo:hd + hi] * cos + qall[:, 2 * hd + lo:2 * hd + hi] * sin
        q_out[h] = (jnp.concatenate([qall[:, lo:hi], q_rope], axis=-1) * scale).astype(BF16)
        k_out[h] = jnp.concatenate([kv[:, lo:hi], k_rope], axis=-1).astype(BF16)
        v_out[h] = jnp.concatenate([kv[:, hd + lo:hd + hi], ones], axis=-1).astype(BF16)


def _mla_up(main, small, pos_col, invf, qn, kvn, wq, wkv, *, tm):
    s = main.shape[0]
    full = lambda a: pl.BlockSpec(a.shape, lambda i: (0,) * a.ndim)
    qk_shape = jax.ShapeDtypeStruct((MLA_HEADS, s, MLA_QK_PAD), BF16)
    return pl.pallas_call(
        _mla_up_kernel,
        grid=(s // tm,),
        in_specs=[
            pl.BlockSpec((tm, MLA_Q_RANK), lambda i: (i, 0)),
            pl.BlockSpec((tm, MLA_KV_RANK), lambda i: (i, 1)),
            pl.BlockSpec((tm, LANES), lambda i: (i, 0)),
            pl.BlockSpec((tm, LANES), lambda i: (i, 1)),
            pl.BlockSpec((tm, 1), lambda i: (i, 0)),
            full(invf), full(qn), full(kvn), full(wq), full(wkv),
        ],
        out_specs=[
            pl.BlockSpec((MLA_HEADS, tm, MLA_QK_PAD), lambda i: (0, i, 0)),
            pl.BlockSpec((MLA_HEADS, tm, MLA_QK_PAD), lambda i: (0, i, 0)),
            pl.BlockSpec((MLA_HEADS, tm, 2 * MLA_V), lambda i: (0, i, 0)),
        ],
        out_shape=[qk_shape, qk_shape, jax.ShapeDtypeStruct((MLA_HEADS, s, 2 * MLA_V), BF16)],
        compiler_params=_cparams("parallel"),
        name="mla_up",
    )(main, main, small, small, pos_col, invf, qn, kvn, wq, wkv)


def _attn_kernel(q_ref, k_ref, v_ref, o_ref, s0_ref, s1_ref, m_ref, acc_ref, *, t):
    i = pl.program_id(1)
    q = q_ref[0]
    dv2 = v_ref.shape[-1]
    dv = dv2 // 2

    def tile(j):
        return pl.ds(pl.multiple_of(j * t, t), t)

    def scores(s_ref, j):
        s_ref[...] = _dot_nt(q, k_ref[0, tile(j), :])

    def update(s_ref, j, diagonal):
        s = s_ref[...]
        if diagonal:
            row = lax.broadcasted_iota(jnp.int32, (t, t), 0)
            col = lax.broadcasted_iota(jnp.int32, (t, t), 1)
            s = jnp.where(col <= row, s, -jnp.inf)
        m = m_ref[...]
        m_new = jnp.maximum(m, jnp.max(s, axis=1, keepdims=True))
        p = jnp.exp2(s - jnp.concatenate([m_new] * (t // LANES), axis=1))
        alpha = jnp.exp2(m - m_new)
        acc_ref[...] = (jnp.concatenate([alpha] * (dv2 // LANES), axis=1) * acc_ref[...]
                        + _dot(p.astype(BF16), v_ref[0, tile(j), :]))
        m_ref[...] = m_new

    m_ref[...] = jnp.full(m_ref.shape, -jnp.inf, F32)
    acc_ref[...] = jnp.zeros(acc_ref.shape, F32)
    scores(s0_ref, 0)

    def pair(jj, carry):
        j = 2 * jj
        scores(s1_ref, j + 1)
        update(s0_ref, j, False)
        scores(s0_ref, j + 2)
        update(s1_ref, j + 1, False)
        return carry

    lax.fori_loop(0, i // 2, pair, 0)

    @pl.when(i % 2 == 0)
    def _():
        update(s0_ref, i, True)

    @pl.when(i % 2 == 1)
    def _():
        scores(s1_ref, i)
        update(s0_ref, i - 1, False)
        update(s1_ref, i, True)

    acc = acc_ref[...]
    o_ref[...] = (acc[:, :dv] / acc[:, dv:]).astype(o_ref.dtype)


def _causal_attention(q, k, v, *, t):
    h, s, dqk = q.shape
    dv2 = v.shape[-1]
    return pl.pallas_call(
        functools.partial(_attn_kernel, t=t),
        grid=(h, s // t),
        in_specs=[
            pl.BlockSpec((1, t, dqk), lambda hh, i: (hh, i, 0)),
            pl.BlockSpec((1, s, dqk), lambda hh, i: (hh, 0, 0)),
            pl.BlockSpec((1, s, dv2), lambda hh, i: (hh, 0, 0)),
        ],
        out_specs=pl.BlockSpec((t, dv2 // 2), lambda hh, i: (i, hh)),
        out_shape=jax.ShapeDtypeStruct((s, h * dv2 // 2), BF16),
        scratch_shapes=[pltpu.VMEM((t, t), F32), pltpu.VMEM((t, t), F32), pltpu.VMEM((t, LANES), F32),
                        pltpu.VMEM((t, dv2), F32)],
        compiler_params=_cparams("parallel", "arbitrary"),
        name="mla_attention",
    )(q, k, v)


def _bdot(a, b):
    return lax.dot_general(a, b, (((2,), (1,)), ((0,), (0,))), preferred_element_type=F32)


def _bdot_nt(a, b):
    return lax.dot_general(a, b, (((2,), (2,)), ((0,), (0,))), preferred_element_type=F32)


def _bdot_tn(a, b):
    return lax.dot_general(a, b, (((1,), (1,)), ((0,), (0,))), preferred_element_type=F32)


def _unit_lower_inverse(low, eye):
    n = low.shape[-1]
    p = low
    r = eye - low
    for _ in range(int(math.log2(n)) - 1):
        pb = p.astype(BF16)
        p = _bdot(pb, pb)
        r = r + _bdot(r.astype(BF16), p.astype(BF16))
    return r


def _gdn_prep_kernel(q_ref, k_ref, v_ref, qh_ref, kh_ref, vh_ref, sm_ref, cw_ref, alog_ref, dtb_ref,
                     qd_out, kd_out, u_out, w_out, a_out, gc_out, hist_ref):
    c, d, nh = CHUNK, GDN_HEAD_DIM, GDN_HEADS
    tm = q_ref.shape[0]
    nc = tm // c
    halo = qh_ref.shape[0]

    first = pl.program_id(0) == 0
    for n, (cur, prev) in enumerate(((q_ref, qh_ref), (k_ref, kh_ref), (v_ref, vh_ref))):
        lanes = slice(n * GDN_DIM, (n + 1) * GDN_DIM)
        hist_ref[pl.ds(0, halo), lanes] = jnp.where(first, 0.0, prev[...].astype(F32))
        hist_ref[pl.ds(halo, tm), lanes] = cur[...].astype(F32)
    base = halo - (GDN_CONV - 1)
    y = hist_ref[pl.ds(base, tm), :] * cw_ref[0:1, :]
    for j in range(1, GDN_CONV):
        y = y + hist_ref[pl.ds(base + j, tm), :] * cw_ref[j:j + 1, :]
    y = _silu(y)

    sm = sm_ref[...]
    beta_all = jax.nn.sigmoid(sm)
    g_all = -jnp.exp(alog_ref[...]) * _softplus(sm + dtb_ref[...])
    row = lax.broadcasted_iota(jnp.int32, (tm, tm), 0)
    col = lax.broadcasted_iota(jnp.int32, (tm, tm), 1)
    tri = jnp.where((col <= row) & (row // c == col // c), 1.0, 0.0).astype(BF16)
    gc_all = _cumsum_rows(tri, g_all)
    gc_out[...] = gc_all
    zpad = jnp.zeros((LANES - c, LANES), F32)
    gc_t = [jnp.concatenate([gc_all[ci * c:(ci + 1) * c], zpad], axis=0).T for ci in range(nc)]

    def heads(arr, off):
        return jnp.stack([arr[ci * c:(ci + 1) * c, off + h * d:off + (h + 1) * d]
                          for ci in range(nc) for h in range(nh)])

    def cols(arr, off):
        return jnp.stack([arr[ci * c:(ci + 1) * c, off + h:off + h + 1] for ci in range(nc) for h in range(nh)])

    q = heads(y, 0)
    k = heads(y, GDN_DIM)
    v = heads(y, 2 * GDN_DIM)
    q = q * lax.rsqrt(jnp.sum(q * q, axis=-1, keepdims=True) + EPS) * (1.0 / math.sqrt(d))
    k = k * lax.rsqrt(jnp.sum(k * k, axis=-1, keepdims=True) + EPS)
    beta = cols(beta_all, 0)
    gc = cols(gc_all, nh)
    gcr = jnp.stack([gc_t[ci][nh + h:nh + h + 1, 0:c] for ci in range(nc) for h in range(nh)])
    r2 = lax.broadcasted_iota(jnp.int32, (c, c), 0)
    c2 = lax.broadcasted_iota(jnp.int32, (c, c), 1)
    decay = jnp.exp(jnp.where(c2 <= r2, gc - gcr, -jnp.inf))
    eye = jnp.where(c2 == r2, 1.0, 0.0).astype(F32)
    kbeta = k * beta
    kb16 = k.astype(BF16)
    low = jnp.where(c2 < r2, _bdot_nt(kbeta.astype(BF16), kb16) * decay, 0.0)
    t_inv = _unit_lower_inverse(low, eye)
    egc = jnp.exp(gc)
    rhs = jnp.concatenate([v * beta, kbeta * egc], axis=-1)
    sol = _bdot(t_inv.astype(BF16), rhs.astype(BF16))
    intra = _bdot_nt(q.astype(BF16), kb16) * decay
    g_last = gcr[:, :, c - 1:c]
    k_dec = k * jnp.exp(g_last - gc)
    q_dec = q * egc
    for ci in range(nc):
        rows = slice(ci * c, (ci + 1) * c)
        for h in range(nh):
            b = ci * nh + h
            lanes = slice(h * d, (h + 1) * d)
            qd_out[rows, lanes] = q_dec[b].astype(BF16)
            kd_out[rows, lanes] = k_dec[b].astype(BF16)
            u_out[rows, lanes] = sol[b, :, :d]
            w_out[rows, lanes] = sol[b, :, d:].astype(BF16)
            a_out[h, rows, :] = intra[b].astype(BF16)


def _gdn_prep(main, small, conv_w, alog, dtb, *, tm, halo=8):
    s = main.shape[0]
    full = lambda a: pl.BlockSpec(a.shape, lambda i: (0,) * a.ndim)
    blk = lambda jcol: pl.BlockSpec((tm, GDN_DIM), lambda i: (i, jcol))
    prev = lambda jcol: pl.BlockSpec((halo, GDN_DIM), lambda i: (jnp.maximum(i * (tm // halo) - 1, 0), jcol))
    act = jax.ShapeDtypeStruct((s, GDN_DIM), BF16)
    return pl.pallas_call(
        _gdn_prep_kernel,
        grid=(s // tm,),
        in_specs=[blk(1), blk(2), blk(3), prev(1), prev(2), prev(3), pl.BlockSpec((tm, LANES), lambda i: (i, 2)),
                  full(conv_w), full(alog), full(dtb)],
        out_specs=[blk(0), blk(0), blk(0), blk(0),
                   pl.BlockSpec((GDN_HEADS, tm, CHUNK), lambda i: (0, i, 0)),
                   pl.BlockSpec((tm, LANES), lambda i: (i, 0))],
        out_shape=[act, act, jax.ShapeDtypeStruct((s, GDN_DIM), F32), act,
                   jax.ShapeDtypeStruct((GDN_HEADS, s, CHUNK), BF16), jax.ShapeDtypeStruct((s, LANES), F32)],
        scratch_shapes=[pltpu.VMEM((tm + halo, 3 * GDN_DIM), F32)],
        compiler_params=_cparams("parallel"),
        name="gdn_prep",
    )(main, main, main, main, main, main, small, conv_w, alog, dtb)


def _gdn_scan_kernel(qd_ref, kd_ref, u_ref, w_ref, a_ref, gc_ref, z_ref, gn_ref, o_ref, state_ref):
    c, d, nh = CHUNK, GDN_HEAD_DIM, GDN_HEADS

    @pl.when(pl.program_id(0) == 0)
    def _():
        state_ref[...] = jnp.zeros(state_ref.shape, F32)

    def heads(ref, rows):
        return jnp.stack([ref[rows, h * d:(h + 1) * d] for h in range(nh)])

    for ci in range(qd_ref.shape[0] // c):
        rows = slice(ci * c, (ci + 1) * c)
        last = ci * c + c - 1
        g_last = jnp.stack([gc_ref[last:last + 1, nh + h:nh + h + 1] for h in range(nh)])
        st = state_ref[...]
        stb = st.astype(BF16)
        v_new = heads(u_ref, rows) - _bdot(heads(w_ref, rows), stb)
        v_newb = v_new.astype(BF16)
        o = _bdot(heads(qd_ref, rows), stb) + _bdot(a_ref[:, rows, :], v_newb)
        state_ref[...] = st * jnp.exp(g_last) + _bdot_tn(heads(kd_ref, rows), v_newb)
        o = _rms(o) * gn_ref[...]
        for h in range(nh):
            lanes = slice(h * d, (h + 1) * d)
            o_ref[rows, lanes] = (o[h] * _silu(z_ref[rows, lanes].astype(F32))).astype(o_ref.dtype)


def _gdn_scan(qd, kd, u, w, a, gc, main, gnorm, *, tm):
    s = qd.shape[0]
    blk = lambda jcol: pl.BlockSpec((tm, GDN_DIM), lambda i: (i, jcol))
    return pl.pallas_call(
        _gdn_scan_kernel,
        grid=(s // tm,),
        in_specs=[blk(0), blk(0), blk(0), blk(0),
                  pl.BlockSpec((GDN_HEADS, tm, CHUNK), lambda i: (0, i, 0)),
                  pl.BlockSpec((tm, LANES), lambda i: (i, 0)),
                  blk(4), pl.BlockSpec(gnorm.shape, lambda i: (0, 0))],
        out_specs=blk(0),
        out_shape=jax.ShapeDtypeStruct((s, GDN_DIM), BF16),
        scratch_shapes=[pltpu.VMEM((GDN_HEADS, GDN_HEAD_DIM, GDN_HEAD_DIM), F32)],
        compiler_params=_cparams("arbitrary"),
        name="gdn_scan",
    )(qd, kd, u, w, a, gc, main, gnorm)


def _gla_kernel(q_ref, k_ref, v_ref, r_ref, sm_ref, wg1_ref, wg2_ref, bg_ref, gn_ref, o_ref, state_ref):
    c = CHUNK
    dk = q_ref.shape[1] // GLA_HEADS
    dv = v_ref.shape[1] // GLA_HEADS

    @pl.when(pl.program_id(0) == 0)
    def _():
        state_ref[...] = jnp.zeros(state_ref.shape, F32)

    a1, a2, _ = _split3(sm_ref[...])
    w1, w2 = wg1_ref[...], wg2_ref[...]
    pre = _dot(a1, w1) + _dot(a1, w2) + _dot(a2, w1) + bg_ref[...]
    gk = (jnp.minimum(pre, 0.0) - jnp.log1p(jnp.exp(-jnp.abs(pre)))) * (1.0 / GLA_GATE_NORMALIZER)
    tm = q_ref.shape[0]
    trow = lax.broadcasted_iota(jnp.int32, (tm, tm), 0)
    tcol = lax.broadcasted_iota(jnp.int32, (tm, tm), 1)
    tri = jnp.where((tcol <= trow) & (trow // c == tcol // c), 1.0, 0.0).astype(BF16)
    b_tile = _cumsum_rows(tri, gk)
    row = lax.broadcasted_iota(jnp.int32, (c, c), 0)
    col = lax.broadcasted_iota(jnp.int32, (c, c), 1)
    sub_row = lax.broadcasted_iota(jnp.int32, (GLA_SUB, 1), 0)
    sub_col = lax.broadcasted_iota(jnp.int32, (GLA_SUB, GLA_SUB), 1)

    def intra_factored(qh, kh, bh, vh):
        b0 = bh[0:1]
        q_t = qh * jnp.exp(bh - b0)
        k_t = kh * jnp.exp(b0 - bh)
        a = jnp.where(col <= row, _dot_nt(q_t.astype(BF16), k_t.astype(BF16)), 0.0)
        q_dec = q_t * jnp.exp(b0)
        k_dec = k_t * jnp.exp(bh[c - 1:c] - b0)
        return _dot(a.astype(BF16), vh), q_dec, k_dec

    def intra_exact(qh, kh, bh, vh):
        o_rows = []
        for blk in range(c // GLA_SUB):
            r0 = blk * GLA_SUB
            qs, bs = qh[r0:r0 + GLA_SUB], bh[r0:r0 + GLA_SUB]
            ks = kh[r0:r0 + GLA_SUB]
            a_diag = jnp.zeros((GLA_SUB, GLA_SUB), F32)
            for j in range(GLA_SUB):
                e = jnp.exp(jnp.where(sub_row >= j, bs - bs[j:j + 1], -jnp.inf))
                dj = jnp.sum(qs * ks[j:j + 1] * e, axis=-1, keepdims=True)
                a_diag = jnp.where(sub_col == j, dj, a_diag)
            o_blk = _dot(a_diag.astype(BF16), vh[r0:r0 + GLA_SUB])
            if blk > 0:
                b_ref0 = bs[0:1]
                q_t = qs * jnp.exp(bs - b_ref0)
                k_t = kh[0:r0] * jnp.exp(b_ref0 - bh[0:r0])
                a_off = _dot_nt(q_t.astype(BF16), k_t.astype(BF16))
                o_blk = o_blk + _dot(a_off.astype(BF16), vh[0:r0])
            o_rows.append(o_blk)
        return jnp.concatenate(o_rows, axis=0), qh * jnp.exp(bh), kh * jnp.exp(bh[c - 1:c] - bh)

    def chunk_step(intra, rows, b_all):
        for h in range(GLA_HEADS):
            bh = b_all[:, h * dk:(h + 1) * dk]
            qh = q_ref[rows, h * dk:(h + 1) * dk].astype(F32) * (1.0 / math.sqrt(dk))
            kh = k_ref[rows, h * dk:(h + 1) * dk].astype(F32)
            vh = v_ref[rows, h * dv:(h + 1) * dv]
            st = state_ref[h]
            o_intra, q_dec, k_dec = intra(qh, kh, bh, vh)
            o = _dot_nt(q_dec.astype(BF16), st.astype(BF16)) + o_intra
            state_ref[h] = st * jnp.exp(bh[c - 1:c]) + _dot_tn(vh, k_dec.astype(BF16))
            o = _rms(o) * gn_ref[...] * _silu(r_ref[rows, h * dv:(h + 1) * dv].astype(F32))
            o_ref[rows, h * dv:(h + 1) * dv] = o.astype(o_ref.dtype)

    for ci in range(tm // c):
        rows = slice(ci * c, (ci + 1) * c)
        b_all = b_tile[rows]
        decay_range = jnp.max(b_all[0:1, :] - b_all[c - 1:c, :])
        factorable = decay_range <= GLA_FACTOR_RANGE
        pl.when(factorable)(functools.partial(chunk_step, intra_factored, rows, b_all))
        pl.when(jnp.logical_not(factorable))(functools.partial(chunk_step, intra_exact, rows, b_all))


def _gla(main, small, wg, bg, gnorm, *, key_dim, value_dim, tm):
    s = main.shape[0]
    full = lambda a: pl.BlockSpec(a.shape, lambda i: (0,) * a.ndim)
    wg1 = wg.astype(BF16)
    wg2 = (wg - wg1.astype(F32)).astype(BF16)
    return pl.pallas_call(
        _gla_kernel,
        grid=(s // tm,),
        in_specs=[
            pl.BlockSpec((tm, key_dim), lambda i: (i, 0)),
            pl.BlockSpec((tm, key_dim), lambda i: (i, 1)),
            pl.BlockSpec((tm, value_dim), lambda i: (i, 1)),
            pl.BlockSpec((tm, value_dim), lambda i: (i, 2)),
            pl.BlockSpec((tm, LANES), lambda i: (i, 0)),
            full(wg1), full(wg2), full(bg), full(gnorm),
        ],
        out_specs=pl.BlockSpec((tm, value_dim), lambda i: (i, 0)),
        out_shape=jax.ShapeDtypeStruct((s, value_dim), BF16),
        scratch_shapes=[pltpu.VMEM((GLA_HEADS, value_dim // GLA_HEADS, key_dim // GLA_HEADS), F32)],
        compiler_params=_cparams("arbitrary"),
        name="gated_linear_attention",
    )(main, main, main, main, small, wg1, wg2, bg, gnorm)


def _out_proj_kernel(a1_ref, a2_ref, w_ref, x_ref, gate_ref, o_ref):
    half = a1_ref.shape[1]
    mix = _dot(a1_ref[...], w_ref[0:half, :]) + _dot(a2_ref[...], w_ref[half:, :])
    o_ref[...] = x_ref[...] + gate_ref[...] * mix


def _out_proj(a1, a2, col1, col2, w, layer, x, gate, *, tm):
    s, d = x.shape
    half = w.shape[1] // 2
    return pl.pallas_call(
        _out_proj_kernel,
        grid=(s // tm,),
        in_specs=[
            pl.BlockSpec((tm, half), lambda i: (i, col1)),
            pl.BlockSpec((tm, half), lambda i: (i, col2)),
            pl.BlockSpec((None,) + w.shape[1:], lambda i: (layer, 0, 0)),
            pl.BlockSpec((tm, d), lambda i: (i, 0)),
            pl.BlockSpec((1, d), lambda i: (0, 0)),
        ],
        out_specs=pl.BlockSpec((tm, d), lambda i: (i, 0)),
        out_shape=jax.ShapeDtypeStruct((s, d), F32),
        compiler_params=_cparams("parallel"),
        name="out_proj_residual",
    )(a1, a2, w, x, gate)


def _ffn_kernel(x_ref, g_ref, sc_ref, sh_ref, gate_ref, fin_ref, w1_ref, w3_ref, w2_ref, o_ref, h_ref, acc_ref,
                *, final_norm):
    j = pl.program_id(1)

    @pl.when(j == 0)
    def _():
        h_ref[...] = _modulated_norm(x_ref[...], g_ref[...], sc_ref[...], sh_ref[...]).astype(BF16)
        acc_ref[...] = jnp.zeros(acc_ref.shape, F32)

    h = h_ref[...]
    a = _dot(h, w1_ref[...])
    t = (_silu(a) * _dot(h, w3_ref[...])).astype(BF16)
    acc_ref[...] += _dot(t, w2_ref[...])

    @pl.when(j == pl.num_programs(1) - 1)
    def _():
        y = x_ref[...] + gate_ref[...] * acc_ref[...]
        if final_norm:
            y = _rms(y) * fin_ref[...]
        o_ref[...] = y


def _ffn(x, g, sc, sh, gate, fin, w1, w3, w2, layer, *, tm, tf, final_norm):
    s, d = x.shape
    dff = w1.shape[2]
    vec = pl.BlockSpec((1, d), lambda i, j: (0, 0))
    return pl.pallas_call(
        functools.partial(_ffn_kernel, final_norm=final_norm),
        grid=(s // tm, dff // tf),
        in_specs=[
            pl.BlockSpec((tm, d), lambda i, j: (i, 0)),
            vec, vec, vec, vec, vec,
            pl.BlockSpec((None, d, tf), lambda i, j: (layer, 0, j)),
            pl.BlockSpec((None, d, tf), lambda i, j: (layer, 0, j)),
            pl.BlockSpec((None, tf, d), lambda i, j: (layer, j, 0)),
        ],
        out_specs=pl.BlockSpec((tm, d), lambda i, j: (i, 0)),
        out_shape=jax.ShapeDtypeStruct((s, d), F32),
        scratch_shapes=[pltpu.VMEM((tm, d), BF16), pltpu.VMEM((tm, d), F32)],
        compiler_params=_cparams("parallel", "arbitrary"),
        name="swiglu_ffn",
    )(x, g, sc, sh, gate, fin, w1, w3, w2)


def _rot_cols(w):
    half = w.shape[-1] // 2
    return jnp.concatenate([-w[..., half:], w[..., :half]], axis=-1)


def _pad_cols(w, n):
    return jnp.pad(w, [(0, 0)] * (w.ndim - 1) + [(0, n - w.shape[-1])])


def _pad_lanes(p):
    return jnp.concatenate([p, jnp.zeros((p.shape[0], LANES - p.shape[1]), p.dtype)], axis=-1)


def _even_in_layout_kernel(w_ref, main_ref, small_ref):
    o_kr = MLA_Q_RANK + MLA_KV_RANK
    o_qkv = o_kr + MLA_ROPE
    o_ba = o_qkv + 4 * GDN_DIM
    w = w_ref[...]
    main_ref[:, :o_kr] = w[:, :o_kr].astype(BF16)
    main_ref[:, o_kr:] = w[:, o_qkv:o_ba].astype(BF16)
    small = [_pad_lanes(p) for p in (w[:, o_kr:o_qkv], _rot_cols(w[:, o_kr:o_qkv]), w[:, o_ba:])]
    small_ref[...] = jnp.concatenate(small, axis=-1).astype(BF16)


def _gla_in_layout_kernel(w_ref, main_ref, small_ref):
    n = main_ref.shape[1]
    w = w_ref[...]
    main_ref[...] = w[:, :n].astype(BF16)
    small_ref[...] = _pad_lanes(w[:, n:]).astype(BF16)


def _in_weight_layout(body, w_in, n_main, n_small, *, tr=256):
    nl, d, n = w_in.shape
    return pl.pallas_call(
        body,
        grid=(nl, d // tr),
        in_specs=[pl.BlockSpec((None, tr, n), lambda l, i: (l, i, 0))],
        out_specs=[pl.BlockSpec((None, tr, n_main), lambda l, i: (l, i, 0)),
                   pl.BlockSpec((None, tr, n_small), lambda l, i: (l, i, 0))],
        out_shape=[jax.ShapeDtypeStruct((nl, d, n_main), BF16), jax.ShapeDtypeStruct((nl, d, n_small), BF16)],
        compiler_params=_cparams("parallel", "parallel"),
        name="in_weight_layout",
    )(w_in)


def _mla_weights(w_uq, w_ukv):
    r = w_uq.shape[0]
    wq = w_uq.reshape(r, MLA_HEADS, MLA_QK)
    nope, rope = wq[..., :MLA_NOPE], wq[..., MLA_NOPE:]
    flat = lambda t: t.reshape(r, MLA_HEADS * LANES)
    wq_all = jnp.concatenate([flat(nope), flat(_pad_cols(rope, LANES)), flat(_pad_cols(_rot_cols(rope), LANES))], axis=1)
    wkv = w_ukv.reshape(w_ukv.shape[0], MLA_HEADS, MLA_NOPE + MLA_V)
    wkv_all = jnp.concatenate([flat(wkv[..., :MLA_NOPE]), flat(wkv[..., MLA_NOPE:])], axis=1)
    return wq_all.astype(BF16), wkv_all.astype(BF16)


def kernel(x, c, positions, norm_g, ada_w, ada_b, ab_w_in, mla_q_norm, mla_w_uq, mla_kv_norm, mla_w_ukv, gdn_conv_w,
           gdn_a_log, gdn_dt_bias, gdn_norm, ab_w_out, gla_w_in, gla_w_gk2, gla_b_gk2, gla_norm, gla_w_out, ffn_w1,
           ffn_w3, ffn_w2, final_norm):
    batch, s, d = x.shape
    assert batch == 1 and c.shape == (1, d)
    depth = norm_g.shape[0]
    key_dim = gla_w_gk2.shape[2]
    value_dim = gla_w_out.shape[1]

    mods = _ada_modulation(c.reshape(d, 1), ada_w.reshape(depth * 2, d, 3 * d), ada_b.reshape(depth * 2, 1, 3 * d))
    xs = x.reshape(s, d)
    pos_col = positions.reshape(s, 1)
    half = MLA_ROPE // 2
    inv_freq = ROPE_THETA ** (-jnp.arange(half, dtype=F32) / half)
    invf = jnp.tile(inv_freq, LANES // half).reshape(1, LANES)
    row = lambda v: v.reshape(1, -1)

    n_gla = 2 * key_dim + 2 * value_dim
    n_even = MLA_Q_RANK + MLA_KV_RANK + 4 * GDN_DIM
    ab_main_w, ab_small_w = _in_weight_layout(_even_in_layout_kernel, ab_w_in, n_even, 3 * LANES)
    gla_main_w, gla_small_w = _in_weight_layout(_gla_in_layout_kernel, gla_w_in, n_gla, LANES)
    ab_out_w, gla_out_w = ab_w_out.astype(BF16), gla_w_out.astype(BF16)
    w1, w3, w2 = ffn_w1.astype(BF16), ffn_w3.astype(BF16), ffn_w2.astype(BF16)

    for layer in range(depth):
        i = layer // 2
        m = mods[2 * layer]
        shift, scale, gate = m[:, :d], m[:, d:2 * d], m[:, 2 * d:]
        gain = row(norm_g[layer, 0])
        if layer % 2 == 0:
            main, small = _norm_proj(xs, gain, scale, shift, ab_main_w, ab_small_w, i, ab_main_w.shape[2],
                                     tm=1024, tn=1024)
            wq, wkv = _mla_weights(mla_w_uq[i], mla_w_ukv[i])
            q, k, v = _mla_up(main, small, pos_col, invf, row(mla_q_norm[i]), row(mla_kv_norm[i]), wq, wkv, tm=256)
            o_a = _causal_attention(q, k, v, t=1024)
            lane_pad = lambda vec: jnp.pad(vec, (GDN_HEADS, LANES - 2 * GDN_HEADS)).reshape(1, LANES)
            qd, kd, u, w, a, gc = _gdn_prep(main, small, gdn_conv_w[i], lane_pad(gdn_a_log[i]),
                                            lane_pad(gdn_dt_bias[i]), tm=256)
            o_b = _gdn_scan(qd, kd, u, w, a, gc, main, row(gdn_norm[i]), tm=256)
            xs = _out_proj(o_a, o_b, 0, 0, ab_out_w, i, xs, gate, tm=256)
        else:
            main, small = _norm_proj(xs, gain, scale, shift, gla_main_w, gla_small_w, i, n_gla, tm=1024, tn=1024)
            wg = jnp.pad(gla_w_gk2[i], ((0, LANES - GLA_GATE_RANK), (0, 0)))
            o_c = _gla(main, small, wg, row(gla_b_gk2[i]), row(gla_norm[i]), key_dim=key_dim, value_dim=value_dim,
                       tm=256)
            xs = _out_proj(o_c, o_c, 0, 1, gla_out_w, i, xs, gate, tm=256)
        m = mods[2 * layer + 1]
        shift, scale, gate = m[:, :d], m[:, d:2 * d], m[:, 2 * d:]
        xs = _ffn(xs, row(norm_g[layer, 1]), scale, shift, gate, row(final_norm), w1, w3, w2, layer, tm=512, tf=512,
                  final_norm=(layer == depth - 1))
    return xs.reshape(batch, s, d)
```

```python
import functools
import math

import jax
import jax.numpy as jnp
from jax import lax
from jax.experimental import pallas as pl
from jax.experimental.pallas import tpu as pltpu

F32 = jnp.float32
BF16 = jnp.bfloat16

EPS = 1e-6
ROPE_THETA = 10000.0
MLA_HEADS = 8
MLA_NOPE = 128
MLA_ROPE = 64
MLA_V = 128
MLA_QK = MLA_NOPE + MLA_ROPE
MLA_QK_PAD = 256
MLA_Q_RANK = 512
MLA_KV_RANK = 512
GDN_HEADS = 8
GDN_HEAD_DIM = 128
GDN_DIM = GDN_HEADS * GDN_HEAD_DIM
GDN_CONV = 4
GLA_HEADS = 4
GLA_GATE_RANK = 16
GLA_GATE_NORMALIZER = 16.0
GLA_SUB = 16
GLA_FACTOR_RANGE = 60.0
CHUNK = 64
LANES = 128
VMEM_LIMIT = 48 * 1024 * 1024


def _cparams(*sem):
    return pltpu.CompilerParams(dimension_semantics=sem, vmem_limit_bytes=VMEM_LIMIT)


def _dot(a, b):
    return jnp.dot(a, b, preferred_element_type=F32)


def _dot_nt(a, b):
    return lax.dot_general(a, b, (((1,), (1,)), ((), ())), preferred_element_type=F32)


def _dot_tn(a, b):
    return lax.dot_general(a, b, (((0,), (0,)), ((), ())), preferred_element_type=F32)


def _split3(x):
    x1 = x.astype(BF16)
    r1 = x - x1.astype(F32)
    x2 = r1.astype(BF16)
    x3 = (r1 - x2.astype(F32)).astype(BF16)
    return x1, x2, x3


def _cumsum_rows(tri, x):
    x1, x2, x3 = _split3(x)
    return _dot(tri, x1) + _dot(tri, x2) + _dot(tri, x3)


def _silu(x):
    return x * jax.nn.sigmoid(x)


def _softplus(x):
    return jnp.maximum(x, 0.0) + jnp.log1p(jnp.exp(-jnp.abs(x)))


def _rms(x):
    return x * lax.rsqrt(jnp.mean(x * x, axis=-1, keepdims=True) + EPS)


def _ada_kernel(c_ref, w_ref, b_ref, o_ref):
    s = _silu(c_ref[...])
    o_ref[0] = jnp.sum(w_ref[0] * s, axis=0, keepdims=True) + b_ref[0]


def _ada_modulation(c_col, w, b, *, tn=768):
    n_mod, d, n = w.shape
    return pl.pallas_call(
        _ada_kernel,
        grid=(n_mod, n // tn),
        in_specs=[
            pl.BlockSpec((d, 1), lambda l, j: (0, 0)),
            pl.BlockSpec((1, d, tn), lambda l, j: (l, 0, j)),
            pl.BlockSpec((1, 1, tn), lambda l, j: (l, 0, j)),
        ],
        out_specs=pl.BlockSpec((1, 1, tn), lambda l, j: (l, 0, j)),
        out_shape=jax.ShapeDtypeStruct((n_mod, 1, n), F32),
        compiler_params=_cparams("parallel", "parallel"),
        name="ada_modulation",
    )(c_col, w, b)


def _modulated_norm(x, g, sc, sh):
    return _rms(x) * g * (1.0 + sc) + sh


def _norm_proj_kernel(x_ref, g_ref, sc_ref, sh_ref, w_ref, ws_ref, o_ref, os_ref, h_ref):
    @pl.when(pl.program_id(1) == 0)
    def _():
        h = _modulated_norm(x_ref[...], g_ref[...], sc_ref[...], sh_ref[...]).astype(BF16)
        h_ref[...] = h
        os_ref[...] = _dot_nt(h, ws_ref[...])

    o_ref[...] = _dot_nt(h_ref[...], w_ref[...]).astype(o_ref.dtype)


def _norm_proj(x, g, sc, sh, w, ws, layer, *, tm, tn):
    s, d = x.shape
    n, ns = w.shape[1], ws.shape[1]
    vec = pl.BlockSpec((1, d), lambda i, j: (0, 0))
    return pl.pallas_call(
        _norm_proj_kernel,
        grid=(s // tm, n // tn),
        in_specs=[
            pl.BlockSpec((tm, d), lambda i, j: (i, 0)),
            vec, vec, vec,
            pl.BlockSpec((None, tn, d), lambda i, j: (layer, j, 0)),
            pl.BlockSpec((None, ns, d), lambda i, j: (layer, 0, 0)),
        ],
        out_specs=[
            pl.BlockSpec((tm, tn), lambda i, j: (i, j)),
            pl.BlockSpec((tm, ns), lambda i, j: (i, 0)),
        ],
        out_shape=[jax.ShapeDtypeStruct((s, n), BF16), jax.ShapeDtypeStruct((s, ns), F32)],
        scratch_shapes=[pltpu.VMEM((tm, d), BF16)],
        compiler_params=_cparams("parallel", "arbitrary"),
        name="norm_proj",
    )(x, g, sc, sh, w, ws)


def _mla_up_kernel(cq_ref, ckv_ref, kr_ref, krr_ref, pos_ref, invf_ref, qn_ref, kvn_ref, wq_ref, wkv_ref,
                   q_out, k_out, v_out):
    hd = MLA_HEADS * LANES
    cqn = (_rms(cq_ref[...].astype(F32)) * qn_ref[...]).astype(BF16)
    ckvn = (_rms(ckv_ref[...].astype(F32)) * kvn_ref[...]).astype(BF16)
    qall = _dot(cqn, wq_ref[...])
    kv = _dot(ckvn, wkv_ref[...])
    ang = pos_ref[...].astype(F32) * invf_ref[...]
    cos, sin = jnp.cos(ang), jnp.sin(ang)
    k_rope = kr_ref[...] * cos + krr_ref[...] * sin
    scale = math.log2(math.e) / math.sqrt(MLA_QK)
    ones = jnp.ones((cq_ref.shape[0], MLA_V), F32)
    for h in range(MLA_HEADS):
        lo, hi = h * LANES, (h + 1) * LANES
        q_rope = qall[:, hd + lo:hd + hi] * cos + qall[:, 2 * hd + lo:2 * hd + hi] * sin
        q_out[h] = (jnp.concatenate([qall[:, lo:hi], q_rope], axis=-1) * scale).astype(BF16)
        k_out[h] = jnp.concatenate([kv[:, lo:hi], k_rope], axis=-1).astype(BF16)
        v_out[h] = jnp.concatenate([kv[:, hd + lo:hd + hi], ones], axis=-1).astype(BF16)


def _mla_up(main, small, pos_col, invf, qn, kvn, wq, wkv, *, tm):
    s = main.shape[0]
    full = lambda a: pl.BlockSpec(a.shape, lambda i: (0,) * a.ndim)
    qk_shape = jax.ShapeDtypeStruct((MLA_HEADS, s, MLA_QK_PAD), BF16)
    return pl.pallas_call(
        _mla_up_kernel,
        grid=(s // tm,),
        in_specs=[
            pl.BlockSpec((tm, MLA_Q_RANK), lambda i: (i, 0)),
            pl.BlockSpec((tm, MLA_KV_RANK), lambda i: (i, 1)),
            pl.BlockSpec((tm, LANES), lambda i: (i, 0)),
            pl.BlockSpec((tm, LANES), lambda i: (i, 1)),
            pl.BlockSpec((tm, 1), lambda i: (i, 0)),
            full(invf), full(qn), full(kvn), full(wq), full(wkv),
        ],
        out_specs=[
            pl.BlockSpec((MLA_HEADS, tm, MLA_QK_PAD), lambda i: (0, i, 0)),
            pl.BlockSpec((MLA_HEADS, tm, MLA_QK_PAD), lambda i: (0, i, 0)),
            pl.BlockSpec((MLA_HEADS, tm, 2 * MLA_V), lambda i: (0, i, 0)),
        ],
        out_shape=[qk_shape, qk_shape, jax.ShapeDtypeStruct((MLA_HEADS, s, 2 * MLA_V), BF16)],
        compiler_params=_cparams("parallel"),
        name="mla_up",
    )(main, main, small, small, pos_col, invf, qn, kvn, wq, wkv)


def _attn_kernel(q_ref, k_ref, v_ref, *rest, t, n_side):
    side_in, (o_ref, *side_out) = rest[:n_side], rest[n_side:2 * n_side + 1]
    s0_ref, s1_ref, m_ref, acc_ref = rest[2 * n_side + 1:]
    for src, dst in zip(side_in, side_out):
        dst[...] = src[...].astype(dst.dtype)
    i = pl.program_id(1)
    q = q_ref[0]
    dv2 = v_ref.shape[-1]
    dv = dv2 // 2

    def tile(j):
        return pl.ds(pl.multiple_of(j * t, t), t)

    def scores(s_ref, j):
        s_ref[...] = _dot_nt(q, k_ref[0, tile(j), :])

    def update(s_ref, j, diagonal):
        s = s_ref[...]
        if diagonal:
            row = lax.broadcasted_iota(jnp.int32, (t, t), 0)
            col = lax.broadcasted_iota(jnp.int32, (t, t), 1)
            s = jnp.where(col <= row, s, -jnp.inf)
        m = m_ref[...]
        m_new = jnp.maximum(m, jnp.max(s, axis=1, keepdims=True))
        p = jnp.exp2(s - jnp.concatenate([m_new] * (t // LANES), axis=1))
        alpha = jnp.exp2(m - m_new)
        acc_ref[...] = (jnp.concatenate([alpha] * (dv2 // LANES), axis=1) * acc_ref[...]
                        + _dot(p.astype(BF16), v_ref[0, tile(j), :]))
        m_ref[...] = m_new

    m_ref[...] = jnp.full(m_ref.shape, -jnp.inf, F32)
    acc_ref[...] = jnp.zeros(acc_ref.shape, F32)
    scores(s0_ref, 0)

    def pair(jj, carry):
        j = 2 * jj
        scores(s1_ref, j + 1)
        update(s0_ref, j, False)
        scores(s0_ref, j + 2)
        update(s1_ref, j + 1, False)
        return carry

    lax.fori_loop(0, i // 2, pair, 0)

    @pl.when(i % 2 == 0)
    def _():
        update(s0_ref, i, True)

    @pl.when(i % 2 == 1)
    def _():
        scores(s1_ref, i)
        update(s0_ref, i - 1, False)
        update(s1_ref, i, True)

    acc = acc_ref[...]
    o_ref[...] = (acc[:, :dv] / acc[:, dv:]).astype(o_ref.dtype)


def _causal_attention(q, k, v, side, layer0, n_layers, *, t):
    h, s, dqk = q.shape
    dv2 = v.shape[-1]
    nq = s // t
    assert s % t == 0 and (h * nq) % n_layers == 0
    per_layer = h * nq // n_layers
    side_in, side_out, side_shapes = [], [], []
    for w in side:
        rows = w.shape[1] // per_layer
        assert rows * per_layer == w.shape[1] and rows % 16 == 0
        blk = (None, rows) + w.shape[2:]
        side_in.append(pl.BlockSpec(blk, lambda hh, i: (layer0 + (hh * nq + i) // per_layer, (hh * nq + i) % per_layer, 0)))
        side_out.append(pl.BlockSpec(blk, lambda hh, i: ((hh * nq + i) // per_layer, (hh * nq + i) % per_layer, 0)))
        side_shapes.append(jax.ShapeDtypeStruct((n_layers,) + w.shape[1:], BF16))
    return pl.pallas_call(
        functools.partial(_attn_kernel, t=t, n_side=len(side)),
        grid=(h, nq),
        in_specs=[
            pl.BlockSpec((1, t, dqk), lambda hh, i: (hh, i, 0)),
            pl.BlockSpec((1, s, dqk), lambda hh, i: (hh, 0, 0)),
            pl.BlockSpec((1, s, dv2), lambda hh, i: (hh, 0, 0)),
        ] + side_in,
        out_specs=[pl.BlockSpec((t, dv2 // 2), lambda hh, i: (i, hh))] + side_out,
        out_shape=[jax.ShapeDtypeStruct((s, h * dv2 // 2), BF16)] + side_shapes,
        scratch_shapes=[pltpu.VMEM((t, t), F32), pltpu.VMEM((t, t), F32), pltpu.VMEM((t, LANES), F32),
                        pltpu.VMEM((t, dv2), F32)],
        compiler_params=_cparams("arbitrary", "arbitrary"),
        name="mla_attention",
    )(q, k, v, *side)


def _bdot(a, b):
    return lax.dot_general(a, b, (((2,), (1,)), ((0,), (0,))), preferred_element_type=F32)


def _bdot_nt(a, b):
    return lax.dot_general(a, b, (((2,), (2,)), ((0,), (0,))), preferred_element_type=F32)


def _bdot_tn(a, b):
    return lax.dot_general(a, b, (((1,), (1,)), ((0,), (0,))), preferred_element_type=F32)


def _unit_lower_inverse(low, eye):
    n = low.shape[-1]
    p = low
    r = eye - low
    for _ in range(int(math.log2(n)) - 1):
        pb = p.astype(BF16)
        p = _bdot(pb, pb)
        r = r + _bdot(r.astype(BF16), p.astype(BF16))
    return r


def _gdn_prep_kernel(q_ref, k_ref, v_ref, qh_ref, kh_ref, vh_ref, sm_ref, cw_ref, alog_ref, dtb_ref,
                     qd_out, kd_out, u_out, w_out, a_out, gc_out, hist_ref):
    c, d, nh = CHUNK, GDN_HEAD_DIM, GDN_HEADS
    tm = q_ref.shape[0]
    nc = tm // c
    halo = qh_ref.shape[0]

    first = pl.program_id(0) == 0
    for n, (cur, prev) in enumerate(((q_ref, qh_ref), (k_ref, kh_ref), (v_ref, vh_ref))):
        lanes = slice(n * GDN_DIM, (n + 1) * GDN_DIM)
        hist_ref[pl.ds(0, halo), lanes] = jnp.where(first, 0.0, prev[...].astype(F32))
        hist_ref[pl.ds(halo, tm), lanes] = cur[...].astype(F32)
    base = halo - (GDN_CONV - 1)
    y = hist_ref[pl.ds(base, tm), :] * cw_ref[0:1, :]
    for j in range(1, GDN_CONV):
        y = y + hist_ref[pl.ds(base + j, tm), :] * cw_ref[j:j + 1, :]
    y = _silu(y)

    sm = sm_ref[...]
    beta_all = jax.nn.sigmoid(sm)
    g_all = -jnp.exp(alog_ref[...]) * _softplus(sm + dtb_ref[...])
    row = lax.broadcasted_iota(jnp.int32, (tm, tm), 0)
    col = lax.broadcasted_iota(jnp.int32, (tm, tm), 1)
    tri = jnp.where((col <= row) & (row // c == col // c), 1.0, 0.0).astype(BF16)
    gc_all = _cumsum_rows(tri, g_all)
    gc_out[...] = gc_all
    zpad = jnp.zeros((LANES - c, LANES), F32)
    gc_t = [jnp.concatenate([gc_all[ci * c:(ci + 1) * c], zpad], axis=0).T for ci in range(nc)]

    def heads(arr, off):
        return jnp.stack([arr[ci * c:(ci + 1) * c, off + h * d:off + (h + 1) * d]
                          for ci in range(nc) for h in range(nh)])

    def cols(arr, off):
        return jnp.stack([arr[ci * c:(ci + 1) * c, off + h:off + h + 1] for ci in range(nc) for h in range(nh)])

    q = heads(y, 0)
    k = heads(y, GDN_DIM)
    v = heads(y, 2 * GDN_DIM)
    q = q * lax.rsqrt(jnp.sum(q * q, axis=-1, keepdims=True) + EPS) * (1.0 / math.sqrt(d))
    k = k * lax.rsqrt(jnp.sum(k * k, axis=-1, keepdims=True) + EPS)
    beta = cols(beta_all, 0)
    gc = cols(gc_all, nh)
    gcr = jnp.stack([gc_t[ci][nh + h:nh + h + 1, 0:c] for ci in range(nc) for h in range(nh)])
    r2 = lax.broadcasted_iota(jnp.int32, (c, c), 0)
    c2 = lax.broadcasted_iota(jnp.int32, (c, c), 1)
    decay = jnp.exp(jnp.where(c2 <= r2, gc - gcr, -jnp.inf))
    eye = jnp.where(c2 == r2, 1.0, 0.0).astype(F32)
    kbeta = k * beta
    kb16 = k.astype(BF16)
    low = jnp.where(c2 < r2, _bdot_nt(kbeta.astype(BF16), kb16) * decay, 0.0)
    t_inv = _unit_lower_inverse(low, eye)
    egc = jnp.exp(gc)
    rhs = jnp.concatenate([v * beta, kbeta * egc], axis=-1)
    sol = _bdot(t_inv.astype(BF16), rhs.astype(BF16))
    intra = _bdot_nt(q.astype(BF16), kb16) * decay
    g_last = gcr[:, :, c - 1:c]
    k_dec = k * jnp.exp(g_last - gc)
    q_dec = q * egc
    for ci in range(nc):
        rows = slice(ci * c, (ci + 1) * c)
        for h in range(nh):
            b = ci * nh + h
            lanes = slice(h * d, (h + 1) * d)
            qd_out[rows, lanes] = q_dec[b].astype(BF16)
            kd_out[rows, lanes] = k_dec[b].astype(BF16)
            u_out[rows, lanes] = sol[b, :, :d]
            w_out[rows, lanes] = sol[b, :, d:].astype(BF16)
            a_out[h, rows, :] = intra[b].astype(BF16)


def _gdn_prep(main, small, conv_w, alog, dtb, *, tm, halo=8):
    s = main.shape[0]
    full = lambda a: pl.BlockSpec(a.shape, lambda i: (0,) * a.ndim)
    blk = lambda jcol: pl.BlockSpec((tm, GDN_DIM), lambda i: (i, jcol))
    prev = lambda jcol: pl.BlockSpec((halo, GDN_DIM), lambda i: (jnp.maximum(i * (tm // halo) - 1, 0), jcol))
    act = jax.ShapeDtypeStruct((s, GDN_DIM), BF16)
    return pl.pallas_call(
        _gdn_prep_kernel,
        grid=(s // tm,),
        in_specs=[blk(1), blk(2), blk(3), prev(1), prev(2), prev(3), pl.BlockSpec((tm, LANES), lambda i: (i, 2)),
                  full(conv_w), full(alog), full(dtb)],
        out_specs=[blk(0), blk(0), blk(0), blk(0),
                   pl.BlockSpec((GDN_HEADS, tm, CHUNK), lambda i: (0, i, 0)),
                   pl.BlockSpec((tm, LANES), lambda i: (i, 0))],
        out_shape=[act, act, jax.ShapeDtypeStruct((s, GDN_DIM), F32), act,
                   jax.ShapeDtypeStruct((GDN_HEADS, s, CHUNK), BF16), jax.ShapeDtypeStruct((s, LANES), F32)],
        scratch_shapes=[pltpu.VMEM((tm + halo, 3 * GDN_DIM), F32)],
        compiler_params=_cparams("parallel"),
        name="gdn_prep",
    )(main, main, main, main, main, main, small, conv_w, alog, dtb)


def _gdn_scan_kernel(qd_ref, kd_ref, u_ref, w_ref, a_ref, gc_ref, z_ref, gn_ref, o_ref, state_ref):
    c, d, nh = CHUNK, GDN_HEAD_DIM, GDN_HEADS

    @pl.when(pl.program_id(0) == 0)
    def _():
        state_ref[...] = jnp.zeros(state_ref.shape, F32)

    def heads(ref, rows):
        return jnp.stack([ref[rows, h * d:(h + 1) * d] for h in range(nh)])

    for ci in range(qd_ref.shape[0] // c):
        rows = slice(ci * c, (ci + 1) * c)
        last = ci * c + c - 1
        g_last = jnp.stack([gc_ref[last:last + 1, nh + h:nh + h + 1] for h in range(nh)])
        st = state_ref[...]
        stb = st.astype(BF16)
        v_new = heads(u_ref, rows) - _bdot(heads(w_ref, rows), stb)
        v_newb = v_new.astype(BF16)
        o = _bdot(heads(qd_ref, rows), stb) + _bdot(a_ref[:, rows, :], v_newb)
        state_ref[...] = st * jnp.exp(g_last) + _bdot_tn(heads(kd_ref, rows), v_newb)
        o = _rms(o) * gn_ref[...]
        for h in range(nh):
            lanes = slice(h * d, (h + 1) * d)
            o_ref[rows, lanes] = (o[h] * _silu(z_ref[rows, lanes].astype(F32))).astype(o_ref.dtype)


def _gdn_scan(qd, kd, u, w, a, gc, main, gnorm, *, tm):
    s = qd.shape[0]
    blk = lambda jcol: pl.BlockSpec((tm, GDN_DIM), lambda i: (i, jcol))
    return pl.pallas_call(
        _gdn_scan_kernel,
        grid=(s // tm,),
        in_specs=[blk(0), blk(0), blk(0), blk(0),
                  pl.BlockSpec((GDN_HEADS, tm, CHUNK), lambda i: (0, i, 0)),
                  pl.BlockSpec((tm, LANES), lambda i: (i, 0)),
                  blk(4), pl.BlockSpec(gnorm.shape, lambda i: (0, 0))],
        out_specs=blk(0),
        out_shape=jax.ShapeDtypeStruct((s, GDN_DIM), BF16),
        scratch_shapes=[pltpu.VMEM((GDN_HEADS, GDN_HEAD_DIM, GDN_HEAD_DIM), F32)],
        compiler_params=_cparams("arbitrary"),
        name="gdn_scan",
    )(qd, kd, u, w, a, gc, main, gnorm)


def _gla_kernel(q_ref, k_ref, v_ref, r_ref, sm_ref, wg1_ref, wg2_ref, bg_ref, gn_ref, o_ref, state_ref):
    c = CHUNK
    dk = q_ref.shape[1] // GLA_HEADS
    dv = v_ref.shape[1] // GLA_HEADS

    @pl.when(pl.program_id(0) == 0)
    def _():
        state_ref[...] = jnp.zeros(state_ref.shape, F32)

    a1, a2, _ = _split3(sm_ref[...])
    w1, w2 = wg1_ref[...], wg2_ref[...]
    pre = _dot(a1, w1) + _dot(a1, w2) + _dot(a2, w1) + bg_ref[...]
    gk = (jnp.minimum(pre, 0.0) - jnp.log1p(jnp.exp(-jnp.abs(pre)))) * (1.0 / GLA_GATE_NORMALIZER)
    tm = q_ref.shape[0]
    trow = lax.broadcasted_iota(jnp.int32, (tm, tm), 0)
    tcol = lax.broadcasted_iota(jnp.int32, (tm, tm), 1)
    tri = jnp.where((tcol <= trow) & (trow // c == tcol // c), 1.0, 0.0).astype(BF16)
    b_tile = _cumsum_rows(tri, gk)
    row = lax.broadcasted_iota(jnp.int32, (c, c), 0)
    col = lax.broadcasted_iota(jnp.int32, (c, c), 1)
    sub_row = lax.broadcasted_iota(jnp.int32, (GLA_SUB, 1), 0)
    sub_col = lax.broadcasted_iota(jnp.int32, (GLA_SUB, GLA_SUB), 1)

    def intra_factored(qh, kh, bh, vh):
        b0 = bh[0:1]
        q_t = qh * jnp.exp(bh - b0)
        k_t = kh * jnp.exp(b0 - bh)
        a = jnp.where(col <= row, _dot_nt(q_t.astype(BF16), k_t.astype(BF16)), 0.0)
        q_dec = q_t * jnp.exp(b0)
        k_dec = k_t * jnp.exp(bh[c - 1:c] - b0)
        return _dot(a.astype(BF16), vh), q_dec, k_dec

    def intra_exact(qh, kh, bh, vh):
        o_rows = []
        for blk in range(c // GLA_SUB):
            r0 = blk * GLA_SUB
            qs, bs = qh[r0:r0 + GLA_SUB], bh[r0:r0 + GLA_SUB]
            ks = kh[r0:r0 + GLA_SUB]
            a_diag = jnp.zeros((GLA_SUB, GLA_SUB), F32)
            for j in range(GLA_SUB):
                e = jnp.exp(jnp.where(sub_row >= j, bs - bs[j:j + 1], -jnp.inf))
                dj = jnp.sum(qs * ks[j:j + 1] * e, axis=-1, keepdims=True)
                a_diag = jnp.where(sub_col == j, dj, a_diag)
            o_blk = _dot(a_diag.astype(BF16), vh[r0:r0 + GLA_SUB])
            if blk > 0:
                b_ref0 = bs[0:1]
                q_t = qs * jnp.exp(bs - b_ref0)
                k_t = kh[0:r0] * jnp.exp(b_ref0 - bh[0:r0])
                a_off = _dot_nt(q_t.astype(BF16), k_t.astype(BF16))
                o_blk = o_blk + _dot(a_off.astype(BF16), vh[0:r0])
            o_rows.append(o_blk)
        return jnp.concatenate(o_rows, axis=0), qh * jnp.exp(bh), kh * jnp.exp(bh[c - 1:c] - bh)

    def chunk_step(intra, rows, b_all):
        for h in range(GLA_HEADS):
            bh = b_all[:, h * dk:(h + 1) * dk]
            qh = q_ref[rows, h * dk:(h + 1) * dk].astype(F32) * (1.0 / math.sqrt(dk))
            kh = k_ref[rows, h * dk:(h + 1) * dk].astype(F32)
            vh = v_ref[rows, h * dv:(h + 1) * dv]
            st = state_ref[h]
            o_intra, q_dec, k_dec = intra(qh, kh, bh, vh)
            o = _dot_nt(q_dec.astype(BF16), st.astype(BF16)) + o_intra
            state_ref[h] = st * jnp.exp(bh[c - 1:c]) + _dot_tn(vh, k_dec.astype(BF16))
            o = _rms(o) * gn_ref[...] * _silu(r_ref[rows, h * dv:(h + 1) * dv].astype(F32))
            o_ref[rows, h * dv:(h + 1) * dv] = o.astype(o_ref.dtype)

    for ci in range(tm // c):
        rows = slice(ci * c, (ci + 1) * c)
        b_all = b_tile[rows]
        decay_range = jnp.max(b_all[0:1, :] - b_all[c - 1:c, :])
        factorable = decay_range <= GLA_FACTOR_RANGE
        pl.when(factorable)(functools.partial(chunk_step, intra_factored, rows, b_all))
        pl.when(jnp.logical_not(factorable))(functools.partial(chunk_step, intra_exact, rows, b_all))


def _gla(main, small, wg, bg, gnorm, *, key_dim, value_dim, tm):
    s = main.shape[0]
    full = lambda a: pl.BlockSpec(a.shape, lambda i: (0,) * a.ndim)
    wg1 = wg.astype(BF16)
    wg2 = (wg - wg1.astype(F32)).astype(BF16)
    return pl.pallas_call(
        _gla_kernel,
        grid=(s // tm,),
        in_specs=[
            pl.BlockSpec((tm, key_dim), lambda i: (i, 0)),
            pl.BlockSpec((tm, key_dim), lambda i: (i, 1)),
            pl.BlockSpec((tm, value_dim), lambda i: (i, 1)),
            pl.BlockSpec((tm, value_dim), lambda i: (i, 2)),
            pl.BlockSpec((tm, LANES), lambda i: (i, 0)),
            full(wg1), full(wg2), full(bg), full(gnorm),
        ],
        out_specs=pl.BlockSpec((tm, value_dim), lambda i: (i, 0)),
        out_shape=jax.ShapeDtypeStruct((s, value_dim), BF16),
        scratch_shapes=[pltpu.VMEM((GLA_HEADS, value_dim // GLA_HEADS, key_dim // GLA_HEADS), F32)],
        compiler_params=_cparams("arbitrary"),
        name="gated_linear_attention",
    )(main, main, main, main, small, wg1, wg2, bg, gnorm)


def _out_proj_kernel(a1_ref, a2_ref, w_ref, x_ref, gate_ref, o_ref):
    half = a1_ref.shape[1]
    mix = _dot(a1_ref[...], w_ref[0:half, :]) + _dot(a2_ref[...], w_ref[half:, :])
    o_ref[...] = x_ref[...] + gate_ref[...] * mix


def _out_proj(a1, a2, col1, col2, w, layer, x, gate, *, tm):
    s, d = x.shape
    half = w.shape[1] // 2
    return pl.pallas_call(
        _out_proj_kernel,
        grid=(s // tm,),
        in_specs=[
            pl.BlockSpec((tm, half), lambda i: (i, col1)),
            pl.BlockSpec((tm, half), lambda i: (i, col2)),
            pl.BlockSpec((None,) + w.shape[1:], lambda i: (layer, 0, 0)),
            pl.BlockSpec((tm, d), lambda i: (i, 0)),
            pl.BlockSpec((1, d), lambda i: (0, 0)),
        ],
        out_specs=pl.BlockSpec((tm, d), lambda i: (i, 0)),
        out_shape=jax.ShapeDtypeStruct((s, d), F32),
        compiler_params=_cparams("parallel"),
        name="out_proj_residual",
    )(a1, a2, w, x, gate)


def _ffn_kernel(x_ref, g_ref, sc_ref, sh_ref, gate_ref, fin_ref, w1_ref, w3_ref, w2_ref, o_ref, h_ref, acc_ref,
                *, final_norm):
    j = pl.program_id(1)

    @pl.when(j == 0)
    def _():
        h_ref[...] = _modulated_norm(x_ref[...], g_ref[...], sc_ref[...], sh_ref[...]).astype(BF16)
        acc_ref[...] = jnp.zeros(acc_ref.shape, F32)

    h = h_ref[...]
    a = _dot(h, w1_ref[...])
    t = (_silu(a) * _dot(h, w3_ref[...])).astype(BF16)
    acc_ref[...] += _dot(t, w2_ref[...])

    @pl.when(j == pl.num_programs(1) - 1)
    def _():
        y = x_ref[...] + gate_ref[...] * acc_ref[...]
        if final_norm:
            y = _rms(y) * fin_ref[...]
        o_ref[...] = y


def _ffn(x, g, sc, sh, gate, fin, w1, w3, w2, layer, *, tm, tf, final_norm):
    s, d = x.shape
    dff = w1.shape[2]
    vec = pl.BlockSpec((1, d), lambda i, j: (0, 0))
    return pl.pallas_call(
        functools.partial(_ffn_kernel, final_norm=final_norm),
        grid=(s // tm, dff // tf),
        in_specs=[
            pl.BlockSpec((tm, d), lambda i, j: (i, 0)),
            vec, vec, vec, vec, vec,
            pl.BlockSpec((None, d, tf), lambda i, j: (layer, 0, j)),
            pl.BlockSpec((None, d, tf), lambda i, j: (layer, 0, j)),
            pl.BlockSpec((None, tf, d), lambda i, j: (layer, j, 0)),
        ],
        out_specs=pl.BlockSpec((tm, d), lambda i, j: (i, 0)),
        out_shape=jax.ShapeDtypeStruct((s, d), F32),
        scratch_shapes=[pltpu.VMEM((tm, d), BF16), pltpu.VMEM((tm, d), F32)],
        compiler_params=_cparams("parallel", "arbitrary"),
        name="swiglu_ffn",
    )(x, g, sc, sh, gate, fin, w1, w3, w2)


def _rot_cols(w):
    half = w.shape[-1] // 2
    return jnp.concatenate([-w[..., half:], w[..., :half]], axis=-1)


def _pad_cols(w, n):
    return jnp.pad(w, [(0, 0)] * (w.ndim - 1) + [(0, n - w.shape[-1])])


def _pad_lanes(p):
    return jnp.concatenate([p, jnp.zeros((p.shape[0], LANES - p.shape[1]), p.dtype)], axis=-1)


def _pad_rows(p):
    return jnp.concatenate([p, jnp.zeros((LANES - p.shape[0], p.shape[1]), p.dtype)], axis=0)


def _even_in_layout_kernel(w_ref, main_ref, small_ref):
    o_kr = MLA_Q_RANK + MLA_KV_RANK
    o_qkv = o_kr + MLA_ROPE
    o_ba = o_qkv + 4 * GDN_DIM
    half = MLA_ROPE // 2
    w = w_ref[...]
    main_ref[:o_kr, :] = w[:o_kr].astype(BF16)
    main_ref[o_kr:, :] = w[o_qkv:o_ba].astype(BF16)
    kr = w[o_kr:o_qkv]
    rot = jnp.concatenate([-kr[half:], kr[:half]], axis=0)
    small_ref[...] = jnp.concatenate([_pad_rows(kr), _pad_rows(rot), _pad_rows(w[o_ba:])], axis=0).astype(BF16)


def _gla_in_layout_kernel(w_ref, main_ref, small_ref):
    n = main_ref.shape[0]
    w = w_ref[...]
    main_ref[...] = w[:n].astype(BF16)
    small_ref[...] = _pad_rows(w[n:]).astype(BF16)


def _in_weight_layout(body, w_in, n_main, n_small, *, tc=256):
    w_t = jnp.swapaxes(w_in, 1, 2)
    nl, n, d = w_t.shape
    return pl.pallas_call(
        body,
        grid=(nl, d // tc),
        in_specs=[pl.BlockSpec((None, n, tc), lambda l, i: (l, 0, i))],
        out_specs=[pl.BlockSpec((None, n_main, tc), lambda l, i: (l, 0, i)),
                   pl.BlockSpec((None, n_small, tc), lambda l, i: (l, 0, i))],
        out_shape=[jax.ShapeDtypeStruct((nl, n_main, d), BF16), jax.ShapeDtypeStruct((nl, n_small, d), BF16)],
        compiler_params=_cparams("parallel", "parallel"),
        name="in_weight_layout",
    )(w_t)


def _mla_weights(w_uq, w_ukv):
    r = w_uq.shape[0]
    wq = w_uq.reshape(r, MLA_HEADS, MLA_QK)
    nope, rope = wq[..., :MLA_NOPE], wq[..., MLA_NOPE:]
    flat = lambda t: t.reshape(r, MLA_HEADS * LANES)
    wq_all = jnp.concatenate([flat(nope), flat(_pad_cols(rope, LANES)), flat(_pad_cols(_rot_cols(rope), LANES))], axis=1)
    wkv = w_ukv.reshape(w_ukv.shape[0], MLA_HEADS, MLA_NOPE + MLA_V)
    wkv_all = jnp.concatenate([flat(wkv[..., :MLA_NOPE]), flat(wkv[..., MLA_NOPE:])], axis=1)
    return wq_all.astype(BF16), wkv_all.astype(BF16)


def kernel(x, c, positions, norm_g, ada_w, ada_b, ab_w_in, mla_q_norm, mla_w_uq, mla_kv_norm, mla_w_ukv, gdn_conv_w,
           gdn_a_log, gdn_dt_bias, gdn_norm, ab_w_out, gla_w_in, gla_w_gk2, gla_b_gk2, gla_norm, gla_w_out, ffn_w1,
           ffn_w3, ffn_w2, final_norm):
    batch, s, d = x.shape
    assert batch == 1 and c.shape == (1, d)
    depth = norm_g.shape[0]
    key_dim = gla_w_gk2.shape[2]
    value_dim = gla_w_out.shape[1]

    mods = _ada_modulation(c.reshape(d, 1), ada_w.reshape(depth * 2, d, 3 * d), ada_b.reshape(depth * 2, 1, 3 * d))
    xs = x.reshape(s, d)
    pos_col = positions.reshape(s, 1)
    half = MLA_ROPE // 2
    inv_freq = ROPE_THETA ** (-jnp.arange(half, dtype=F32) / half)
    invf = jnp.tile(inv_freq, LANES // half).reshape(1, LANES)
    row = lambda v: v.reshape(1, -1)

    n_gla = 2 * key_dim + 2 * value_dim
    n_even = MLA_Q_RANK + MLA_KV_RANK + 4 * GDN_DIM
    ab_main_w, ab_small_w = _in_weight_layout(_even_in_layout_kernel, ab_w_in, n_even, 3 * LANES)
    gla_main_w, gla_small_w = _in_weight_layout(_gla_in_layout_kernel, gla_w_in, n_gla, LANES)
    ab_out_w, gla_out_w = ab_w_out.astype(BF16), gla_w_out.astype(BF16)
    assert depth % 2 == 0

    for layer in range(depth):
        i = layer // 2
        m = mods[2 * layer]
        shift, scale, gate = m[:, :d], m[:, d:2 * d], m[:, 2 * d:]
        gain = row(norm_g[layer, 0])
        if layer % 2 == 0:
            main, small = _norm_proj(xs, gain, scale, shift, ab_main_w, ab_small_w, i, tm=1024, tn=1024)
            wq, wkv = _mla_weights(mla_w_uq[i], mla_w_ukv[i])
            q, k, v = _mla_up(main, small, pos_col, invf, row(mla_q_norm[i]), row(mla_kv_norm[i]), wq, wkv, tm=256)
            o_a, w1, w3, w2 = _causal_attention(q, k, v, (ffn_w1, ffn_w3, ffn_w2), layer, 2, t=1024)
            lane_pad = lambda vec: jnp.pad(vec, (GDN_HEADS, LANES - 2 * GDN_HEADS)).reshape(1, LANES)
            qd, kd, u, w, a, gc = _gdn_prep(main, small, gdn_conv_w[i], lane_pad(gdn_a_log[i]),
                                            lane_pad(gdn_dt_bias[i]), tm=256)
            o_b = _gdn_scan(qd, kd, u, w, a, gc, main, row(gdn_norm[i]), tm=256)
            xs = _out_proj(o_a, o_b, 0, 0, ab_out_w, i, xs, gate, tm=256)
        else:
            main, small = _norm_proj(xs, gain, scale, shift, gla_main_w, gla_small_w, i, tm=1024, tn=1024)
            wg = jnp.pad(gla_w_gk2[i], ((0, LANES - GLA_GATE_RANK), (0, 0)))
            o_c = _gla(main, small, wg, row(gla_b_gk2[i]), row(gla_norm[i]), key_dim=key_dim, value_dim=value_dim,
                       tm=256)
            xs = _out_proj(o_c, o_c, 0, 1, gla_out_w, i, xs, gate, tm=256)
        m = mods[2 * layer + 1]
        shift, scale, gate = m[:, :d], m[:, d:2 * d], m[:, 2 * d:]
        xs = _ffn(xs, row(norm_g[layer, 1]), scale, shift, gate, row(final_norm), w1, w3, w2, layer % 2, tm=512,
                  tf=512,
                  final_norm=(layer == depth - 1))
    return xs.reshape(batch, s, d)
```

```python
import functools
import math

import jax
import jax.numpy as jnp
from jax import lax
from jax.experimental import pallas as pl
from jax.experimental.pallas import tpu as pltpu

F32 = jnp.float32
BF16 = jnp.bfloat16

EPS = 1e-6
ROPE_THETA = 10000.0
MLA_HEADS = 8
MLA_NOPE = 128
MLA_ROPE = 64
MLA_V = 128
MLA_QK = MLA_NOPE + MLA_ROPE
MLA_QK_PAD = 256
MLA_V_PAD = MLA_V + 16
MLA_Q_RANK = 512
MLA_KV_RANK = 512
GDN_HEADS = 8
GDN_HEAD_DIM = 128
GDN_DIM = GDN_HEADS * GDN_HEAD_DIM
GDN_CONV = 4
GLA_HEADS = 4
GLA_GATE_RANK = 16
GLA_GATE_NORMALIZER = 16.0
GLA_SUB = 16
GLA_FACTOR_RANGE = 60.0
CHUNK = 64
LANES = 128
VMEM_LIMIT = 48 * 1024 * 1024


def _cparams(*sem):
    return pltpu.CompilerParams(dimension_semantics=sem, vmem_limit_bytes=VMEM_LIMIT)


def _dot(a, b):
    return jnp.dot(a, b, preferred_element_type=F32)


def _dot_nt(a, b):
    return lax.dot_general(a, b, (((1,), (1,)), ((), ())), preferred_element_type=F32)


def _dot_tn(a, b):
    return lax.dot_general(a, b, (((0,), (0,)), ((), ())), preferred_element_type=F32)


def _split3(x):
    x1 = x.astype(BF16)
    r1 = x - x1.astype(F32)
    x2 = r1.astype(BF16)
    x3 = (r1 - x2.astype(F32)).astype(BF16)
    return x1, x2, x3


def _cumsum_rows(tri, x):
    x1, x2, x3 = _split3(x)
    return _dot(tri, x1) + _dot(tri, x2) + _dot(tri, x3)


def _silu(x):
    return x * jax.nn.sigmoid(x)


def _softplus(x):
    return jnp.maximum(x, 0.0) + jnp.log1p(jnp.exp(-jnp.abs(x)))


def _rms(x):
    return x * lax.rsqrt(jnp.mean(x * x, axis=-1, keepdims=True) + EPS)


def _ada_kernel(c_ref, w_ref, b_ref, o_ref):
    s = _silu(c_ref[...])
    o_ref[0] = jnp.sum(w_ref[0] * s, axis=0, keepdims=True) + b_ref[0]


def _ada_modulation(c_col, w, b, *, tn=768):
    n_mod, d, n = w.shape
    return pl.pallas_call(
        _ada_kernel,
        grid=(n_mod, n // tn),
        in_specs=[
            pl.BlockSpec((d, 1), lambda l, j: (0, 0)),
            pl.BlockSpec((1, d, tn), lambda l, j: (l, 0, j)),
            pl.BlockSpec((1, 1, tn), lambda l, j: (l, 0, j)),
        ],
        out_specs=pl.BlockSpec((1, 1, tn), lambda l, j: (l, 0, j)),
        out_shape=jax.ShapeDtypeStruct((n_mod, 1, n), F32),
        compiler_params=_cparams("parallel", "parallel"),
        name="ada_modulation",
    )(c_col, w, b)


def _modulated_norm(x, g, sc, sh):
    return _rms(x) * (g * (1.0 + sc)) + sh


def _norm_proj_kernel(x_ref, g_ref, sc_ref, sh_ref, w_ref, ws_ref, o_ref, os_ref, h_ref):
    @pl.when(pl.program_id(1) == 0)
    def _():
        h = _modulated_norm(x_ref[...], g_ref[...], sc_ref[...], sh_ref[...]).astype(BF16)
        h_ref[...] = h
        os_ref[...] = _dot_nt(h, ws_ref[...])

    o_ref[...] = _dot_nt(h_ref[...], w_ref[...]).astype(o_ref.dtype)


def _norm_proj(x, g, sc, sh, w, ws, layer, *, tm, tn):
    s, d = x.shape
    n, ns = w.shape[1], ws.shape[1]
    vec = pl.BlockSpec((1, d), lambda i, j: (0, 0))
    return pl.pallas_call(
        _norm_proj_kernel,
        grid=(s // tm, n // tn),
        in_specs=[
            pl.BlockSpec((tm, d), lambda i, j: (i, 0)),
            vec, vec, vec,
            pl.BlockSpec((None, tn, d), lambda i, j: (layer, j, 0)),
            pl.BlockSpec((None, ns, d), lambda i, j: (layer, 0, 0)),
        ],
        out_specs=[
            pl.BlockSpec((tm, tn), lambda i, j: (i, j)),
            pl.BlockSpec((tm, ns), lambda i, j: (i, 0)),
        ],
        out_shape=[jax.ShapeDtypeStruct((s, n), BF16), jax.ShapeDtypeStruct((s, ns), F32)],
        scratch_shapes=[pltpu.VMEM((tm, d), BF16)],
        compiler_params=_cparams("parallel", "arbitrary"),
        name="norm_proj",
    )(x, g, sc, sh, w, ws)


def _mla_up_kernel(cq_ref, ckv_ref, kr_ref, krr_ref, pos_ref, invf_ref, qn_ref, kvn_ref, wq_ref, wkv_ref,
                   q_out, k_out, v_out):
    hd = MLA_HEADS * LANES
    cqn = (_rms(cq_ref[...].astype(F32)) * qn_ref[...]).astype(BF16)
    ckvn = (_rms(ckv_ref[...].astype(F32)) * kvn_ref[...]).astype(BF16)
    qall = _dot(cqn, wq_ref[...])
    kv = _dot(ckvn, wkv_ref[...])
    ang = pos_ref[...].astype(F32) * invf_ref[...]
    cos, sin = jnp.cos(ang), jnp.sin(ang)
    k_rope = kr_ref[...] * cos + krr_ref[...] * sin
    scale = math.log2(math.e) / math.sqrt(MLA_QK)
    ones = jnp.ones((MLA_V_PAD - MLA_V, cq_ref.shape[0]), F32)
    for h in range(MLA_HEADS):
        lo, hi = h * LANES, (h + 1) * LANES
        q_rope = qall[:, hd + lo:hd + hi] * cos + qall[:, 2 * hd + lo:2 * hd + hi] * sin
        q_out[h] = (jnp.concatenate([qall[:, lo:hi], q_rope], axis=-1) * scale).astype(BF16)
        k_out[h] = jnp.concatenate([kv[:, lo:hi], k_rope], axis=-1).astype(BF16)
        v_out[h] = jnp.concatenate([kv[:, hd + lo:hd + hi].T, ones], axis=0).astype(BF16)


def _mla_up(main, small, pos_col, invf, qn, kvn, wq, wkv, *, tm):
    s = main.shape[0]
    full = lambda a: pl.BlockSpec(a.shape, lambda i: (0,) * a.ndim)
    qk_shape = jax.ShapeDtypeStruct((MLA_HEADS, s, MLA_QK_PAD), BF16)
    return pl.pallas_call(
        _mla_up_kernel,
        grid=(s // tm,),
        in_specs=[
            pl.BlockSpec((tm, MLA_Q_RANK), lambda i: (i, 0)),
            pl.BlockSpec((tm, MLA_KV_RANK), lambda i: (i, 1)),
            pl.BlockSpec((tm, LANES), lambda i: (i, 0)),
            pl.BlockSpec((tm, LANES), lambda i: (i, 1)),
            pl.BlockSpec((tm, 1), lambda i: (i, 0)),
            full(invf), full(qn), full(kvn), full(wq), full(wkv),
        ],
        out_specs=[
            pl.BlockSpec((MLA_HEADS, tm, MLA_QK_PAD), lambda i: (0, i, 0)),
            pl.BlockSpec((MLA_HEADS, tm, MLA_QK_PAD), lambda i: (0, i, 0)),
            pl.BlockSpec((MLA_HEADS, MLA_V_PAD, tm), lambda i: (0, 0, i)),
        ],
        out_shape=[qk_shape, qk_shape, jax.ShapeDtypeStruct((MLA_HEADS, MLA_V_PAD, s), BF16)],
        compiler_params=_cparams("parallel"),
        name="mla_up",
    )(main, main, small, small, pos_col, invf, qn, kvn, wq, wkv)


def _attn_kernel(q_ref, k_ref, v_ref, *rest, t, n_side):
    side_in, (o_ref, *side_out) = rest[:n_side], rest[n_side:2 * n_side + 1]
    s0_ref, s1_ref, m_ref, acc_ref = rest[2 * n_side + 1:]
    for src, dst in zip(side_in, side_out):
        dst[...] = src[...].astype(dst.dtype)
    i = pl.program_id(1)
    q = q_ref[0]
    dv = MLA_V

    def tile(j):
        return pl.ds(pl.multiple_of(j * t, t), t)

    def scores(s_ref, j):
        s_ref[...] = _dot_nt(k_ref[0, tile(j), :], q)

    def update(s_ref, j, diagonal):
        s = s_ref[...]
        if diagonal:
            key = lax.broadcasted_iota(jnp.int32, (t, t), 0)
            qry = lax.broadcasted_iota(jnp.int32, (t, t), 1)
            s = jnp.where(key <= qry, s, -jnp.inf)
        m = m_ref[...]
        m_new = jnp.maximum(m, jnp.max(s, axis=0, keepdims=True))
        p = jnp.exp2(s - m_new)
        alpha = jnp.exp2(m - m_new)
        acc_ref[...] = alpha * acc_ref[...] + _dot(v_ref[0, :, tile(j)], p.astype(BF16))
        m_ref[...] = m_new

    m_ref[...] = jnp.full(m_ref.shape, -jnp.inf, F32)
    acc_ref[...] = jnp.zeros(acc_ref.shape, F32)
    scores(s0_ref, 0)

    def pair(jj, carry):
        j = 2 * jj
        scores(s1_ref, j + 1)
        update(s0_ref, j, False)
        scores(s0_ref, j + 2)
        update(s1_ref, j + 1, False)
        return carry

    lax.fori_loop(0, i // 2, pair, 0)

    @pl.when(i % 2 == 0)
    def _():
        update(s0_ref, i, True)

    @pl.when(i % 2 == 1)
    def _():
        scores(s1_ref, i)
        update(s0_ref, i - 1, False)
        update(s1_ref, i, True)

    acc = acc_ref[...]
    o_ref[...] = (acc[:dv] / acc[dv:dv + 1]).T.astype(o_ref.dtype)


def _causal_attention(q, k, v, side, layer0, n_layers, *, t):
    h, s, dqk = q.shape
    dvp = v.shape[1]
    nq = s // t
    assert s % t == 0 and (h * nq) % n_layers == 0
    per_layer = h * nq // n_layers
    side_in, side_out, side_shapes = [], [], []
    for w in side:
        rows = w.shape[1] // per_layer
        assert rows * per_layer == w.shape[1] and rows % 16 == 0
        blk = (None, rows) + w.shape[2:]
        side_in.append(pl.BlockSpec(blk, lambda hh, i: (layer0 + (hh * nq + i) // per_layer, (hh * nq + i) % per_layer, 0)))
        side_out.append(pl.BlockSpec(blk, lambda hh, i: ((hh * nq + i) // per_layer, (hh * nq + i) % per_layer, 0)))
        side_shapes.append(jax.ShapeDtypeStruct((n_layers,) + w.shape[1:], BF16))
    return pl.pallas_call(
        functools.partial(_attn_kernel, t=t, n_side=len(side)),
        grid=(h, nq),
        in_specs=[
            pl.BlockSpec((1, t, dqk), lambda hh, i: (hh, i, 0)),
            pl.BlockSpec((1, s, dqk), lambda hh, i: (hh, 0, 0)),
            pl.BlockSpec((1, dvp, s), lambda hh, i: (hh, 0, 0)),
        ] + side_in,
        out_specs=[pl.BlockSpec((t, MLA_V), lambda hh, i: (i, hh))] + side_out,
        out_shape=[jax.ShapeDtypeStruct((s, h * MLA_V), BF16)] + side_shapes,
        scratch_shapes=[pltpu.VMEM((t, t), F32), pltpu.VMEM((t, t), F32), pltpu.VMEM((1, t), F32),
                        pltpu.VMEM((dvp, t), F32)],
        compiler_params=_cparams("arbitrary", "arbitrary"),
        name="mla_attention",
    )(q, k, v, *side)


def _bdot(a, b):
    return lax.dot_general(a, b, (((2,), (1,)), ((0,), (0,))), preferred_element_type=F32)


def _bdot_nt(a, b):
    return lax.dot_general(a, b, (((2,), (2,)), ((0,), (0,))), preferred_element_type=F32)


def _bdot_tn(a, b):
    return lax.dot_general(a, b, (((1,), (1,)), ((0,), (0,))), preferred_element_type=F32)


def _unit_lower_inverse(low, eye):
    n = low.shape[-1]
    p = low
    r = eye - low
    for _ in range(int(math.log2(n)) - 1):
        pb = p.astype(BF16)
        p = _bdot(pb, pb)
        r = r + _bdot(r.astype(BF16), p.astype(BF16))
    return r


def _gdn_prep_kernel(q_ref, k_ref, v_ref, qh_ref, kh_ref, vh_ref, sm_ref, cw_ref, alog_ref, dtb_ref,
                     qd_out, kd_out, u_out, w_out, a_out, gc_out, hist_ref):
    c, d, nh = CHUNK, GDN_HEAD_DIM, GDN_HEADS
    tm = q_ref.shape[0]
    nc = tm // c
    halo = qh_ref.shape[0]

    first = pl.program_id(0) == 0
    for n, (cur, prev) in enumerate(((q_ref, qh_ref), (k_ref, kh_ref), (v_ref, vh_ref))):
        lanes = slice(n * GDN_DIM, (n + 1) * GDN_DIM)
        hist_ref[pl.ds(0, halo), lanes] = jnp.where(first, 0.0, prev[...].astype(F32))
        hist_ref[pl.ds(halo, tm), lanes] = cur[...].astype(F32)
    base = halo - (GDN_CONV - 1)
    y = hist_ref[pl.ds(base, tm), :] * cw_ref[0:1, :]
    for j in range(1, GDN_CONV):
        y = y + hist_ref[pl.ds(base + j, tm), :] * cw_ref[j:j + 1, :]
    y = _silu(y)

    sm = sm_ref[...]
    beta_all = jax.nn.sigmoid(sm)
    g_all = -jnp.exp(alog_ref[...]) * _softplus(sm + dtb_ref[...])
    row = lax.broadcasted_iota(jnp.int32, (tm, tm), 0)
    col = lax.broadcasted_iota(jnp.int32, (tm, tm), 1)
    tri = jnp.where((col <= row) & (row // c == col // c), 1.0, 0.0).astype(BF16)
    gc_all = _cumsum_rows(tri, g_all)
    gc_out[...] = gc_all
    zpad = jnp.zeros((LANES - c, LANES), F32)
    gc_t = [jnp.concatenate([gc_all[ci * c:(ci + 1) * c], zpad], axis=0).T for ci in range(nc)]

    def heads(arr, off):
        return jnp.stack([arr[ci * c:(ci + 1) * c, off + h * d:off + (h + 1) * d]
                          for ci in range(nc) for h in range(nh)])

    def cols(arr, off):
        return jnp.stack([arr[ci * c:(ci + 1) * c, off + h:off + h + 1] for ci in range(nc) for h in range(nh)])

    q = heads(y, 0)
    k = heads(y, GDN_DIM)
    v = heads(y, 2 * GDN_DIM)
    q = q * lax.rsqrt(jnp.sum(q * q, axis=-1, keepdims=True) + EPS) * (1.0 / math.sqrt(d))
    k = k * lax.rsqrt(jnp.sum(k * k, axis=-1, keepdims=True) + EPS)
    beta = cols(beta_all, 0)
    gc = cols(gc_all, nh)
    gcr = jnp.stack([gc_t[ci][nh + h:nh + h + 1, 0:c] for ci in range(nc) for h in range(nh)])
    r2 = lax.broadcasted_iota(jnp.int32, (c, c), 0)
    c2 = lax.broadcasted_iota(jnp.int32, (c, c), 1)
    decay = jnp.exp(jnp.where(c2 <= r2, gc - gcr, -jnp.inf))
    eye = jnp.where(c2 == r2, 1.0, 0.0).astype(F32)
    kbeta = k * beta
    kb16 = k.astype(BF16)
    low = jnp.where(c2 < r2, _bdot_nt(kbeta.astype(BF16), kb16) * decay, 0.0)
    t_inv = _unit_lower_inverse(low, eye)
    egc = jnp.exp(gc)
    rhs = jnp.concatenate([v * beta, kbeta * egc], axis=-1)
    sol = _bdot(t_inv.astype(BF16), rhs.astype(BF16))
    intra = _bdot_nt(q.astype(BF16), kb16) * decay
    g_last = gcr[:, :, c - 1:c]
    k_dec = k * jnp.exp(g_last - gc)
    q_dec = q * egc
    for ci in range(nc):
        rows = slice(ci * c, (ci + 1) * c)
        for h in range(nh):
            b = ci * nh + h
            lanes = slice(h * d, (h + 1) * d)
            qd_out[rows, lanes] = q_dec[b].astype(BF16)
            kd_out[rows, lanes] = k_dec[b].astype(BF16)
            u_out[rows, lanes] = sol[b, :, :d]
            w_out[rows, lanes] = sol[b, :, d:].astype(BF16)
            a_out[h, rows, :] = intra[b].astype(BF16)


def _gdn_prep(main, small, conv_w, alog, dtb, *, tm, halo=8):
    s = main.shape[0]
    full = lambda a: pl.BlockSpec(a.shape, lambda i: (0,) * a.ndim)
    blk = lambda jcol: pl.BlockSpec((tm, GDN_DIM), lambda i: (i, jcol))
    prev = lambda jcol: pl.BlockSpec((halo, GDN_DIM), lambda i: (jnp.maximum(i * (tm // halo) - 1, 0), jcol))
    act = jax.ShapeDtypeStruct((s, GDN_DIM), BF16)
    return pl.pallas_call(
        _gdn_prep_kernel,
        grid=(s // tm,),
        in_specs=[blk(1), blk(2), blk(3), prev(1), prev(2), prev(3), pl.BlockSpec((tm, LANES), lambda i: (i, 2)),
                  full(conv_w), full(alog), full(dtb)],
        out_specs=[blk(0), blk(0), blk(0), blk(0),
                   pl.BlockSpec((GDN_HEADS, tm, CHUNK), lambda i: (0, i, 0)),
                   pl.BlockSpec((tm, LANES), lambda i: (i, 0))],
        out_shape=[act, act, jax.ShapeDtypeStruct((s, GDN_DIM), F32), act,
                   jax.ShapeDtypeStruct((GDN_HEADS, s, CHUNK), BF16), jax.ShapeDtypeStruct((s, LANES), F32)],
        scratch_shapes=[pltpu.VMEM((tm + halo, 3 * GDN_DIM), F32)],
        compiler_params=_cparams("parallel"),
        name="gdn_prep",
    )(main, main, main, main, main, main, small, conv_w, alog, dtb)


def _gdn_scan_kernel(qd_ref, kd_ref, u_ref, w_ref, a_ref, gc_ref, z_ref, gn_ref, o_ref, state_ref):
    c, d, nh = CHUNK, GDN_HEAD_DIM, GDN_HEADS

    @pl.when(pl.program_id(0) == 0)
    def _():
        state_ref[...] = jnp.zeros(state_ref.shape, F32)

    def heads(ref, rows):
        return jnp.stack([ref[rows, h * d:(h + 1) * d] for h in range(nh)])

    for ci in range(qd_ref.shape[0] // c):
        rows = slice(ci * c, (ci + 1) * c)
        last = ci * c + c - 1
        g_last = jnp.stack([gc_ref[last:last + 1, nh + h:nh + h + 1] for h in range(nh)])
        st = state_ref[...]
        stb = st.astype(BF16)
        v_new = heads(u_ref, rows) - _bdot(heads(w_ref, rows), stb)
        v_newb = v_new.astype(BF16)
        o = _bdot(heads(qd_ref, rows), stb) + _bdot(a_ref[:, rows, :], v_newb)
        state_ref[...] = st * jnp.exp(g_last) + _bdot_tn(heads(kd_ref, rows), v_newb)
        o = _rms(o) * gn_ref[...]
        for h in range(nh):
            lanes = slice(h * d, (h + 1) * d)
            o_ref[rows, lanes] = (o[h] * _silu(z_ref[rows, lanes].astype(F32))).astype(o_ref.dtype)


def _gdn_scan(qd, kd, u, w, a, gc, main, gnorm, *, tm):
    s = qd.shape[0]
    blk = lambda jcol: pl.BlockSpec((tm, GDN_DIM), lambda i: (i, jcol))
    return pl.pallas_call(
        _gdn_scan_kernel,
        grid=(s // tm,),
        in_specs=[blk(0), blk(0), blk(0), blk(0),
                  pl.BlockSpec((GDN_HEADS, tm, CHUNK), lambda i: (0, i, 0)),
                  pl.BlockSpec((tm, LANES), lambda i: (i, 0)),
                  blk(4), pl.BlockSpec(gnorm.shape, lambda i: (0, 0))],
        out_specs=blk(0),
        out_shape=jax.ShapeDtypeStruct((s, GDN_DIM), BF16),
        scratch_shapes=[pltpu.VMEM((GDN_HEADS, GDN_HEAD_DIM, GDN_HEAD_DIM), F32)],
        compiler_params=_cparams("arbitrary"),
        name="gdn_scan",
    )(qd, kd, u, w, a, gc, main, gnorm)


def _gla_kernel(q_ref, k_ref, v_ref, r_ref, sm_ref, wg1_ref, wg2_ref, bg_ref, gn_ref, o_ref, state_ref):
    c = CHUNK
    dk = q_ref.shape[1] // GLA_HEADS
    dv = v_ref.shape[1] // GLA_HEADS

    @pl.when(pl.program_id(0) == 0)
    def _():
        state_ref[...] = jnp.zeros(state_ref.shape, F32)

    a1, a2, _ = _split3(sm_ref[...])
    w1, w2 = wg1_ref[...], wg2_ref[...]
    pre = _dot(a1, w1) + _dot(a1, w2) + _dot(a2, w1) + bg_ref[...]
    gk = (jnp.minimum(pre, 0.0) - jnp.log1p(jnp.exp(-jnp.abs(pre)))) * (1.0 / GLA_GATE_NORMALIZER)
    tm = q_ref.shape[0]
    trow = lax.broadcasted_iota(jnp.int32, (tm, tm), 0)
    tcol = lax.broadcasted_iota(jnp.int32, (tm, tm), 1)
    tri = jnp.where((tcol <= trow) & (trow // c == tcol // c), 1.0, 0.0).astype(BF16)
    b_tile = _cumsum_rows(tri, gk)
    row = lax.broadcasted_iota(jnp.int32, (c, c), 0)
    col = lax.broadcasted_iota(jnp.int32, (c, c), 1)
    sub_row = lax.broadcasted_iota(jnp.int32, (GLA_SUB, 1), 0)
    sub_col = lax.broadcasted_iota(jnp.int32, (GLA_SUB, GLA_SUB), 1)

    def chunk_factored(rows, b_all):
        heads = lambda ref, w: jnp.stack([ref[rows, h * w:(h + 1) * w] for h in range(GLA_HEADS)])
        bh = jnp.stack([b_all[:, h * dk:(h + 1) * dk] for h in range(GLA_HEADS)])
        q = heads(q_ref, dk).astype(F32) * (1.0 / math.sqrt(dk))
        k = heads(k_ref, dk).astype(F32)
        v = heads(v_ref, dv)
        b0, b_last = bh[:, 0:1], bh[:, c - 1:c]
        q_t = q * jnp.exp(bh - b0)
        k_t = k * jnp.exp(b0 - bh)
        a = jnp.where(col <= row, _bdot_nt(q_t.astype(BF16), k_t.astype(BF16)), 0.0)
        q_dec = q_t * jnp.exp(b0)
        k_dec = k_t * jnp.exp(b_last - b0)
        st = state_ref[...]
        o = _bdot_nt(q_dec.astype(BF16), st.astype(BF16)) + _bdot(a.astype(BF16), v)
        state_ref[...] = st * jnp.exp(b_last) + _bdot_tn(v, k_dec.astype(BF16))
        o = _rms(o) * gn_ref[...]
        for h in range(GLA_HEADS):
            lanes = slice(h * dv, (h + 1) * dv)
            o_ref[rows, lanes] = (o[h] * _silu(r_ref[rows, lanes].astype(F32))).astype(o_ref.dtype)

    def intra_exact(qh, kh, bh, vh):
        o_rows = []
        for blk in range(c // GLA_SUB):
            r0 = blk * GLA_SUB
            qs, bs = qh[r0:r0 + GLA_SUB], bh[r0:r0 + GLA_SUB]
            ks = kh[r0:r0 + GLA_SUB]
            a_diag = jnp.zeros((GLA_SUB, GLA_SUB), F32)
            for j in range(GLA_SUB):
                e = jnp.exp(jnp.where(sub_row >= j, bs - bs[j:j + 1], -jnp.inf))
                dj = jnp.sum(qs * ks[j:j + 1] * e, axis=-1, keepdims=True)
                a_diag = jnp.where(sub_col == j, dj, a_diag)
            o_blk = _dot(a_diag.astype(BF16), vh[r0:r0 + GLA_SUB])
            if blk > 0:
                b_ref0 = bs[0:1]
                q_t = qs * jnp.exp(bs - b_ref0)
                k_t = kh[0:r0] * jnp.exp(b_ref0 - bh[0:r0])
                a_off = _dot_nt(q_t.astype(BF16), k_t.astype(BF16))
                o_blk = o_blk + _dot(a_off.astype(BF16), vh[0:r0])
            o_rows.append(o_blk)
        return jnp.concatenate(o_rows, axis=0), qh * jnp.exp(bh), kh * jnp.exp(bh[c - 1:c] - bh)

    def chunk_exact(rows, b_all):
        for h in range(GLA_HEADS):
            bh = b_all[:, h * dk:(h + 1) * dk]
            qh = q_ref[rows, h * dk:(h + 1) * dk].astype(F32) * (1.0 / math.sqrt(dk))
            kh = k_ref[rows, h * dk:(h + 1) * dk].astype(F32)
            vh = v_ref[rows, h * dv:(h + 1) * dv]
            st = state_ref[h]
            o_intra, q_dec, k_dec = intra_exact(qh, kh, bh, vh)
            o = _dot_nt(q_dec.astype(BF16), st.astype(BF16)) + o_intra
            state_ref[h] = st * jnp.exp(bh[c - 1:c]) + _dot_tn(vh, k_dec.astype(BF16))
            o = _rms(o) * gn_ref[...] * _silu(r_ref[rows, h * dv:(h + 1) * dv].astype(F32))
            o_ref[rows, h * dv:(h + 1) * dv] = o.astype(o_ref.dtype)

    for ci in range(tm // c):
        rows = slice(ci * c, (ci + 1) * c)
        b_all = b_tile[rows]
        decay_range = jnp.max(b_all[0:1, :] - b_all[c - 1:c, :])
        factorable = decay_range <= GLA_FACTOR_RANGE
        pl.when(factorable)(functools.partial(chunk_factored, rows, b_all))
        pl.when(jnp.logical_not(factorable))(functools.partial(chunk_exact, rows, b_all))


def _gla(main, small, wg, bg, gnorm, *, key_dim, value_dim, tm):
    s = main.shape[0]
    full = lambda a: pl.BlockSpec(a.shape, lambda i: (0,) * a.ndim)
    wg1 = wg.astype(BF16)
    wg2 = (wg - wg1.astype(F32)).astype(BF16)
    return pl.pallas_call(
        _gla_kernel,
        grid=(s // tm,),
        in_specs=[
            pl.BlockSpec((tm, key_dim), lambda i: (i, 0)),
            pl.BlockSpec((tm, key_dim), lambda i: (i, 1)),
            pl.BlockSpec((tm, value_dim), lambda i: (i, 1)),
            pl.BlockSpec((tm, value_dim), lambda i: (i, 2)),
            pl.BlockSpec((tm, LANES), lambda i: (i, 0)),
            full(wg1), full(wg2), full(bg), full(gnorm),
        ],
        out_specs=pl.BlockSpec((tm, value_dim), lambda i: (i, 0)),
        out_shape=jax.ShapeDtypeStruct((s, value_dim), BF16),
        scratch_shapes=[pltpu.VMEM((GLA_HEADS, value_dim // GLA_HEADS, key_dim // GLA_HEADS), F32)],
        compiler_params=_cparams("arbitrary"),
        name="gated_linear_attention",
    )(main, main, main, main, small, wg1, wg2, bg, gnorm)


def _out_proj_kernel(a1_ref, a2_ref, w_ref, x_ref, gate_ref, o_ref):
    half = a1_ref.shape[1]
    mix = _dot(a1_ref[...], w_ref[0:half, :]) + _dot(a2_ref[...], w_ref[half:, :])
    o_ref[...] = x_ref[...] + gate_ref[...] * mix


def _out_proj(a1, a2, col1, col2, w, layer, x, gate, *, tm):
    s, d = x.shape
    half = w.shape[1] // 2
    return pl.pallas_call(
        _out_proj_kernel,
        grid=(s // tm,),
        in_specs=[
            pl.BlockSpec((tm, half), lambda i: (i, col1)),
            pl.BlockSpec((tm, half), lambda i: (i, col2)),
            pl.BlockSpec((None,) + w.shape[1:], lambda i: (layer, 0, 0)),
            pl.BlockSpec((tm, d), lambda i: (i, 0)),
            pl.BlockSpec((1, d), lambda i: (0, 0)),
        ],
        out_specs=pl.BlockSpec((tm, d), lambda i: (i, 0)),
        out_shape=jax.ShapeDtypeStruct((s, d), F32),
        compiler_params=_cparams("parallel"),
        name="out_proj_residual",
    )(a1, a2, w, x, gate)


def _ffn_kernel(x_ref, g_ref, sc_ref, sh_ref, gate_ref, fin_ref, w1_ref, w3_ref, w2_ref, o_ref, h_ref, acc_ref,
                *, final_norm):
    j = pl.program_id(1)

    @pl.when(j == 0)
    def _():
        h_ref[...] = _modulated_norm(x_ref[...], g_ref[...], sc_ref[...], sh_ref[...]).astype(BF16)
        acc_ref[...] = jnp.zeros(acc_ref.shape, F32)

    h = h_ref[...]
    a = _dot(h, w1_ref[...])
    t = (_silu(a) * _dot(h, w3_ref[...])).astype(BF16)
    acc_ref[...] += _dot(t, w2_ref[...])

    @pl.when(j == pl.num_programs(1) - 1)
    def _():
        y = x_ref[...] + gate_ref[...] * acc_ref[...]
        if final_norm:
            y = _rms(y) * fin_ref[...]
        o_ref[...] = y


def _ffn(x, g, sc, sh, gate, fin, w1, w3, w2, layer, *, tm, tf, final_norm):
    s, d = x.shape
    dff = w1.shape[2]
    vec = pl.BlockSpec((1, d), lambda i, j: (0, 0))
    return pl.pallas_call(
        functools.partial(_ffn_kernel, final_norm=final_norm),
        grid=(s // tm, dff // tf),
        in_specs=[
            pl.BlockSpec((tm, d), lambda i, j: (i, 0)),
            vec, vec, vec, vec, vec,
            pl.BlockSpec((None, d, tf), lambda i, j: (layer, 0, j)),
            pl.BlockSpec((None, d, tf), lambda i, j: (layer, 0, j)),
            pl.BlockSpec((None, tf, d), lambda i, j: (layer, j, 0)),
        ],
        out_specs=pl.BlockSpec((tm, d), lambda i, j: (i, 0)),
        out_shape=jax.ShapeDtypeStruct((s, d), F32),
        scratch_shapes=[pltpu.VMEM((tm, d), BF16), pltpu.VMEM((tm, d), F32)],
        compiler_params=_cparams("parallel", "arbitrary"),
        name="swiglu_ffn",
    )(x, g, sc, sh, gate, fin, w1, w3, w2)


def _rot_cols(w):
    half = w.shape[-1] // 2
    return jnp.concatenate([-w[..., half:], w[..., :half]], axis=-1)


def _pad_cols(w, n):
    return jnp.pad(w, [(0, 0)] * (w.ndim - 1) + [(0, n - w.shape[-1])])


def _pad_lanes(p):
    return jnp.concatenate([p, jnp.zeros((p.shape[0], LANES - p.shape[1]), p.dtype)], axis=-1)


def _pad_rows(p):
    return jnp.concatenate([p, jnp.zeros((LANES - p.shape[0], p.shape[1]), p.dtype)], axis=0)


def _even_in_layout_kernel(w_ref, main_ref, small_ref):
    o_kr = MLA_Q_RANK + MLA_KV_RANK
    o_qkv = o_kr + MLA_ROPE
    o_ba = o_qkv + 4 * GDN_DIM
    half = MLA_ROPE // 2
    w = w_ref[...]
    main_ref[:o_kr, :] = w[:o_kr].astype(BF16)
    main_ref[o_kr:, :] = w[o_qkv:o_ba].astype(BF16)
    kr = w[o_kr:o_qkv]
    rot = jnp.concatenate([-kr[half:], kr[:half]], axis=0)
    small_ref[...] = jnp.concatenate([_pad_rows(kr), _pad_rows(rot), _pad_rows(w[o_ba:])], axis=0).astype(BF16)


def _gla_in_layout_kernel(w_ref, main_ref, small_ref):
    n = main_ref.shape[0]
    w = w_ref[...]
    main_ref[...] = w[:n].astype(BF16)
    small_ref[...] = _pad_rows(w[n:]).astype(BF16)


def _in_weight_layout(body, w_in, n_main, n_small, *, tc=256):
    w_t = jnp.swapaxes(w_in, 1, 2)
    nl, n, d = w_t.shape
    return pl.pallas_call(
        body,
        grid=(nl, d // tc),
        in_specs=[pl.BlockSpec((None, n, tc), lambda l, i: (l, 0, i))],
        out_specs=[pl.BlockSpec((None, n_main, tc), lambda l, i: (l, 0, i)),
                   pl.BlockSpec((None, n_small, tc), lambda l, i: (l, 0, i))],
        out_shape=[jax.ShapeDtypeStruct((nl, n_main, d), BF16), jax.ShapeDtypeStruct((nl, n_small, d), BF16)],
        compiler_params=_cparams("parallel", "parallel"),
        name="in_weight_layout",
    )(w_t)


def _mla_weights(w_uq, w_ukv):
    r = w_uq.shape[0]
    wq = w_uq.reshape(r, MLA_HEADS, MLA_QK)
    nope, rope = wq[..., :MLA_NOPE], wq[..., MLA_NOPE:]
    flat = lambda t: t.reshape(r, MLA_HEADS * LANES)
    wq_all = jnp.concatenate([flat(nope), flat(_pad_cols(rope, LANES)), flat(_pad_cols(_rot_cols(rope), LANES))], axis=1)
    wkv = w_ukv.reshape(w_ukv.shape[0], MLA_HEADS, MLA_NOPE + MLA_V)
    wkv_all = jnp.concatenate([flat(wkv[..., :MLA_NOPE]), flat(wkv[..., MLA_NOPE:])], axis=1)
    return wq_all.astype(BF16), wkv_all.astype(BF16)


def kernel(x, c, positions, norm_g, ada_w, ada_b, ab_w_in, mla_q_norm, mla_w_uq, mla_kv_norm, mla_w_ukv, gdn_conv_w,
           gdn_a_log, gdn_dt_bias, gdn_norm, ab_w_out, gla_w_in, gla_w_gk2, gla_b_gk2, gla_norm, gla_w_out, ffn_w1,
           ffn_w3, ffn_w2, final_norm):
    batch, s, d = x.shape
    assert batch == 1 and c.shape == (1, d)
    depth = norm_g.shape[0]
    key_dim = gla_w_gk2.shape[2]
    value_dim = gla_w_out.shape[1]

    mods = _ada_modulation(c.reshape(d, 1), ada_w.reshape(depth * 2, d, 3 * d), ada_b.reshape(depth * 2, 1, 3 * d))
    xs = x.reshape(s, d)
    pos_col = positions.reshape(s, 1)
    half = MLA_ROPE // 2
    inv_freq = ROPE_THETA ** (-jnp.arange(half, dtype=F32) / half)
    invf = jnp.tile(inv_freq, LANES // half).reshape(1, LANES)
    row = lambda v: v.reshape(1, -1)

    n_gla = 2 * key_dim + 2 * value_dim
    n_even = MLA_Q_RANK + MLA_KV_RANK + 4 * GDN_DIM
    ab_main_w, ab_small_w = _in_weight_layout(_even_in_layout_kernel, ab_w_in, n_even, 3 * LANES)
    gla_main_w, gla_small_w = _in_weight_layout(_gla_in_layout_kernel, gla_w_in, n_gla, LANES)
    ab_out_w, gla_out_w = ab_w_out.astype(BF16), gla_w_out.astype(BF16)
    assert depth % 2 == 0

    for layer in range(depth):
        i = layer // 2
        m = mods[2 * layer]
        shift, scale, gate = m[:, :d], m[:, d:2 * d], m[:, 2 * d:]
        gain = row(norm_g[layer, 0])
        if layer % 2 == 0:
            main, small = _norm_proj(xs, gain, scale, shift, ab_main_w, ab_small_w, i, tm=1024, tn=1024)
            wq, wkv = _mla_weights(mla_w_uq[i], mla_w_ukv[i])
            q, k, v = _mla_up(main, small, pos_col, invf, row(mla_q_norm[i]), row(mla_kv_norm[i]), wq, wkv, tm=256)
            o_a, w1, w3, w2 = _causal_attention(q, k, v, (ffn_w1, ffn_w3, ffn_w2), layer, 2, t=1024)
            lane_pad = lambda vec: jnp.pad(vec, (GDN_HEADS, LANES - 2 * GDN_HEADS)).reshape(1, LANES)
            qd, kd, u, w, a, gc = _gdn_prep(main, small, gdn_conv_w[i], lane_pad(gdn_a_log[i]),
                                            lane_pad(gdn_dt_bias[i]), tm=256)
            o_b = _gdn_scan(qd, kd, u, w, a, gc, main, row(gdn_norm[i]), tm=256)
            xs = _out_proj(o_a, o_b, 0, 0, ab_out_w, i, xs, gate, tm=512)
        else:
            main, small = _norm_proj(xs, gain, scale, shift, gla_main_w, gla_small_w, i, tm=1024, tn=1024)
            wg = jnp.pad(gla_w_gk2[i], ((0, LANES - GLA_GATE_RANK), (0, 0)))
            o_c = _gla(main, small, wg, row(gla_b_gk2[i]), row(gla_norm[i]), key_dim=key_dim, value_dim=value_dim,
                       tm=256)
            xs = _out_proj(o_c, o_c, 0, 1, gla_out_w, i, xs, gate, tm=512)
        m = mods[2 * layer + 1]
        shift, scale, gate = m[:, :d], m[:, d:2 * d], m[:, 2 * d:]
        xs = _ffn(xs, row(norm_g[layer, 1]), scale, shift, gate, row(final_norm), w1, w3, w2, layer % 2, tm=512,
                  tf=512,
                  final_norm=(layer == depth - 1))
    return xs.reshape(batch, s, d)
```

```python
import functools
import math

import jax
import jax.numpy as jnp
from jax import lax
from jax.experimental import pallas as pl
from jax.experimental.pallas import tpu as pltpu

F32 = jnp.float32
BF16 = jnp.bfloat16

EPS = 1e-6
ROPE_THETA = 10000.0
MLA_HEADS = 8
MLA_NOPE = 128
MLA_ROPE = 64
MLA_V = 128
MLA_QK = MLA_NOPE + MLA_ROPE
MLA_QK_PAD = 256
MLA_V_PAD = MLA_V + 16
MLA_Q_RANK = 512
MLA_KV_RANK = 512
GDN_HEADS = 8
GDN_HEAD_DIM = 128
GDN_DIM = GDN_HEADS * GDN_HEAD_DIM
GDN_CONV = 4
GLA_HEADS = 4
GLA_GATE_RANK = 16
GLA_GATE_NORMALIZER = 16.0
GLA_SUB = 16
GLA_FACTOR_RANGE = 60.0
CHUNK = 64
LANES = 128
VMEM_LIMIT = 48 * 1024 * 1024


def _cparams(*sem):
    return pltpu.CompilerParams(dimension_semantics=sem, vmem_limit_bytes=VMEM_LIMIT)


def _dot(a, b):
    return jnp.dot(a, b, preferred_element_type=F32)


def _dot_nt(a, b):
    return lax.dot_general(a, b, (((1,), (1,)), ((), ())), preferred_element_type=F32)


def _dot_tn(a, b):
    return lax.dot_general(a, b, (((0,), (0,)), ((), ())), preferred_element_type=F32)


def _split3(x):
    x1 = x.astype(BF16)
    r1 = x - x1.astype(F32)
    x2 = r1.astype(BF16)
    x3 = (r1 - x2.astype(F32)).astype(BF16)
    return x1, x2, x3


def _cumsum_rows(tri, x):
    x1, x2, x3 = _split3(x)
    return _dot(tri, x1) + _dot(tri, x2) + _dot(tri, x3)


def _silu(x):
    return x * jax.nn.sigmoid(x)


def _softplus(x):
    return jnp.maximum(x, 0.0) + jnp.log1p(jnp.exp(-jnp.abs(x)))


def _rms(x):
    return x * lax.rsqrt(jnp.mean(x * x, axis=-1, keepdims=True) + EPS)


def _ada_kernel(c_ref, w_ref, b_ref, o_ref, s_ref):
    s = _silu(c_ref[...])
    o_ref[0] = jnp.sum(w_ref[0] * s, axis=0, keepdims=True) + b_ref[0]
    s_ref[...] = jnp.broadcast_to(s, s_ref.shape)


def _ada_modulation(c_col, w, b, n_mod, *, tn=768):
    _, d, n = w.shape
    return pl.pallas_call(
        _ada_kernel,
        grid=(n_mod, n // tn),
        in_specs=[
            pl.BlockSpec((d, 1), lambda l, j: (0, 0)),
            pl.BlockSpec((1, d, tn), lambda l, j: (l, 0, j)),
            pl.BlockSpec((1, 1, tn), lambda l, j: (l, 0, j)),
        ],
        out_specs=[pl.BlockSpec((1, 1, tn), lambda l, j: (l, 0, j)), pl.BlockSpec((d, LANES), lambda l, j: (0, 0))],
        out_shape=[jax.ShapeDtypeStruct((n_mod, 1, n), F32), jax.ShapeDtypeStruct((d, LANES), F32)],
        compiler_params=_cparams("arbitrary", "arbitrary"),
        name="ada_modulation",
    )(c_col, w, b)


def _modulated_norm(x, g, sc, sh):
    return _rms(x) * (g * (1.0 + sc)) + sh


def _norm_proj_kernel(x_ref, g_ref, sc_ref, sh_ref, w_ref, ws_ref, o_ref, os_ref, h_ref, *, slab):
    i, j = pl.program_id(0), pl.program_id(1)
    tm = x_ref.shape[0]
    cur = pl.multiple_of(((i + 1) % 2) * tm, tm)
    nxt = pl.multiple_of((i % 2) * tm, tm)
    r0 = pl.multiple_of(jnp.minimum(j, tm // slab - 1) * slab, slab)

    def norm_slab():
        h = _modulated_norm(x_ref[pl.ds(r0, slab), :], g_ref[...], sc_ref[...], sh_ref[...])
        h_ref[pl.ds(nxt + r0, slab), :] = h.astype(BF16)

    pl.when(i == 0)(norm_slab)

    @pl.when((i > 0) & (j == 0))
    def _():
        os_ref[...] = _dot_nt(h_ref[pl.ds(cur, tm), :], ws_ref[...])

    @pl.when(i > 0)
    def _():
        o_ref[...] = _dot_nt(h_ref[pl.ds(cur, tm), :], w_ref[...]).astype(o_ref.dtype)
        norm_slab()


def _norm_proj(x, g, sc, sh, w, ws, layer, *, tm, tn, slab=256):
    s, d = x.shape
    n, ns = w.shape[1], ws.shape[1]
    ni, nj = s // tm, n // tn
    assert s % tm == 0 and n % tn == 0 and tm % slab == 0 and nj >= tm // slab
    vec = pl.BlockSpec((1, d), lambda i, j: (0, 0))
    prev = lambda i: jnp.maximum(i - 1, 0)
    return pl.pallas_call(
        functools.partial(_norm_proj_kernel, slab=slab),
        grid=(ni + 1, nj),
        in_specs=[
            pl.BlockSpec((tm, d), lambda i, j: (jnp.minimum(i, ni - 1), 0)),
            vec, vec, vec,
            pl.BlockSpec((None, tn, d), lambda i, j: (layer, j, 0)),
            pl.BlockSpec((None, ns, d), lambda i, j: (layer, 0, 0)),
        ],
        out_specs=[
            pl.BlockSpec((tm, tn), lambda i, j: (prev(i), j * jnp.minimum(i, 1))),
            pl.BlockSpec((tm, ns), lambda i, j: (prev(i), 0)),
        ],
        out_shape=[jax.ShapeDtypeStruct((s, n), BF16), jax.ShapeDtypeStruct((s, ns), F32)],
        scratch_shapes=[pltpu.VMEM((2 * tm, d), BF16)],
        compiler_params=_cparams("arbitrary", "arbitrary"),
        name="norm_proj",
    )(x, g, sc, sh, w, ws)


def _mla_up_kernel(cq_ref, ckv_ref, kr_ref, krr_ref, pos_ref, invf_ref, qn_ref, kvn_ref, wq_ref, wkv_ref,
                   q_out, k_out, v_out):
    hd = MLA_HEADS * LANES
    cqn = (_rms(cq_ref[...].astype(F32)) * qn_ref[...]).astype(BF16)
    ckvn = (_rms(ckv_ref[...].astype(F32)) * kvn_ref[...]).astype(BF16)
    qall = _dot(cqn, wq_ref[...])
    kv = _dot(ckvn, wkv_ref[...])
    ang = pos_ref[...].astype(F32) * invf_ref[...]
    cos, sin = jnp.cos(ang), jnp.sin(ang)
    k_rope = kr_ref[...] * cos + krr_ref[...] * sin
    scale = math.log2(math.e) / math.sqrt(MLA_QK)
    ones = jnp.ones((MLA_V_PAD - MLA_V, cq_ref.shape[0]), F32)
    for h in range(MLA_HEADS):
        lo, hi = h * LANES, (h + 1) * LANES
        q_rope = qall[:, hd + lo:hd + hi] * cos + qall[:, 2 * hd + lo:2 * hd + hi] * sin
        q_out[h] = (jnp.concatenate([qall[:, lo:hi], q_rope], axis=-1) * scale).astype(BF16)
        k_out[h] = jnp.concatenate([kv[:, lo:hi], k_rope], axis=-1).astype(BF16)
        v_out[h] = jnp.concatenate([kv[:, hd + lo:hd + hi].T, ones], axis=0).astype(BF16)


def _mla_up(main, small, pos_col, invf, qn, kvn, wq, wkv, *, tm):
    s = main.shape[0]
    full = lambda a: pl.BlockSpec(a.shape, lambda i: (0,) * a.ndim)
    qk_shape = jax.ShapeDtypeStruct((MLA_HEADS, s, MLA_QK_PAD), BF16)
    return pl.pallas_call(
        _mla_up_kernel,
        grid=(s // tm,),
        in_specs=[
            pl.BlockSpec((tm, MLA_Q_RANK), lambda i: (i, 0)),
            pl.BlockSpec((tm, MLA_KV_RANK), lambda i: (i, 1)),
            pl.BlockSpec((tm, LANES), lambda i: (i, 0)),
            pl.BlockSpec((tm, LANES), lambda i: (i, 1)),
            pl.BlockSpec((tm, 1), lambda i: (i, 0)),
            full(invf), full(qn), full(kvn), full(wq), full(wkv),
        ],
        out_specs=[
            pl.BlockSpec((MLA_HEADS, tm, MLA_QK_PAD), lambda i: (0, i, 0)),
            pl.BlockSpec((MLA_HEADS, tm, MLA_QK_PAD), lambda i: (0, i, 0)),
            pl.BlockSpec((MLA_HEADS, MLA_V_PAD, tm), lambda i: (0, 0, i)),
        ],
        out_shape=[qk_shape, qk_shape, jax.ShapeDtypeStruct((MLA_HEADS, MLA_V_PAD, s), BF16)],
        compiler_params=_cparams("parallel"),
        name="mla_up",
    )(main, main, small, small, pos_col, invf, qn, kvn, wq, wkv)


def _attn_kernel(q_ref, k_ref, v_ref, *rest, t, n_side):
    side_in, (sc_ref, aw_ref, ab_ref) = rest[:n_side], rest[n_side:n_side + 3]
    (o_ref, *side_out), mods_ref = rest[n_side + 3:2 * n_side + 4], rest[2 * n_side + 4]
    s0_ref, s1_ref, m_ref, acc_ref = rest[2 * n_side + 5:]
    for src, dst in zip(side_in, side_out):
        dst[...] = src[...].astype(dst.dtype)
    silu_c = jnp.concatenate([sc_ref[...]] * (aw_ref.shape[1] // LANES), axis=1)
    mods_ref[...] = jnp.sum(aw_ref[...] * silu_c, axis=0, keepdims=True) + ab_ref[...]
    i = pl.program_id(1)
    q = q_ref[0]
    dv = MLA_V

    def tile(j):
        return pl.ds(pl.multiple_of(j * t, t), t)

    def scores(s_ref, j):
        s_ref[...] = _dot_nt(k_ref[0, tile(j), :], q)

    def update(s_ref, j, diagonal):
        s = s_ref[...]
        if diagonal:
            key = lax.broadcasted_iota(jnp.int32, (t, t), 0)
            qry = lax.broadcasted_iota(jnp.int32, (t, t), 1)
            s = jnp.where(key <= qry, s, -jnp.inf)
        m = m_ref[...]
        m_new = jnp.maximum(m, jnp.max(s, axis=0, keepdims=True))
        p = jnp.exp2(s - m_new)
        alpha = jnp.exp2(m - m_new)
        acc_ref[...] = alpha * acc_ref[...] + _dot(v_ref[0, :, tile(j)], p.astype(BF16))
        m_ref[...] = m_new

    m_ref[...] = jnp.full(m_ref.shape, -jnp.inf, F32)
    acc_ref[...] = jnp.zeros(acc_ref.shape, F32)
    scores(s0_ref, 0)

    def pair(jj, carry):
        j = 2 * jj
        scores(s1_ref, j + 1)
        update(s0_ref, j, False)
        scores(s0_ref, j + 2)
        update(s1_ref, j + 1, False)
        return carry

    lax.fori_loop(0, i // 2, pair, 0)

    @pl.when(i % 2 == 0)
    def _():
        update(s0_ref, i, True)

    @pl.when(i % 2 == 1)
    def _():
        scores(s1_ref, i)
        update(s0_ref, i - 1, False)
        update(s1_ref, i, True)

    acc = acc_ref[...]
    o_ref[...] = (acc[:dv] / acc[dv:dv + 1]).T.astype(o_ref.dtype)


def _causal_attention(q, k, v, side, layer0, n_layers, ada, *, t):
    h, s, dqk = q.shape
    dvp = v.shape[1]
    nq = s // t
    steps = h * nq
    assert s % t == 0 and steps % n_layers == 0
    step = lambda hh, i: hh * nq + i
    per_layer = steps // n_layers
    side_in, side_out, side_shapes = [], [], []
    for w in side:
        rows = w.shape[1] // per_layer
        assert rows * per_layer == w.shape[1] and rows % 16 == 0
        blk = (None, rows) + w.shape[2:]
        side_in.append(pl.BlockSpec(blk, lambda hh, i: (layer0 + step(hh, i) // per_layer, step(hh, i) % per_layer, 0)))
        side_out.append(pl.BlockSpec(blk, lambda hh, i: (step(hh, i) // per_layer, step(hh, i) % per_layer, 0)))
        side_shapes.append(jax.ShapeDtypeStruct((n_layers,) + w.shape[1:], BF16))
    silu_c, ada_w, ada_b, first, count = ada
    n_mod, d, n3 = ada_w.shape
    per_mod = steps // count
    slab = n3 // per_mod
    assert per_mod * count == steps and slab * per_mod == n3 and slab % LANES == 0
    mod_idx = lambda hh, i: jnp.minimum(first + step(hh, i) // per_mod, n_mod - 1)
    ada_in = [
        pl.BlockSpec((d, LANES), lambda hh, i: (0, 0)),
        pl.BlockSpec((None, d, slab), lambda hh, i: (mod_idx(hh, i), 0, step(hh, i) % per_mod)),
        pl.BlockSpec((None, 1, slab), lambda hh, i: (mod_idx(hh, i), 0, step(hh, i) % per_mod)),
    ]
    mods_out = pl.BlockSpec((None, 1, slab), lambda hh, i: (step(hh, i) // per_mod, 0, step(hh, i) % per_mod))
    return pl.pallas_call(
        functools.partial(_attn_kernel, t=t, n_side=len(side)),
        grid=(h, nq),
        in_specs=[
            pl.BlockSpec((1, t, dqk), lambda hh, i: (hh, i, 0)),
            pl.BlockSpec((1, s, dqk), lambda hh, i: (hh, 0, 0)),
            pl.BlockSpec((1, dvp, s), lambda hh, i: (hh, 0, 0)),
        ] + side_in + ada_in,
        out_specs=[pl.BlockSpec((t, MLA_V), lambda hh, i: (i, hh))] + side_out + [mods_out],
        out_shape=[jax.ShapeDtypeStruct((s, h * MLA_V), BF16)] + side_shapes
        + [jax.ShapeDtypeStruct((count, 1, n3), F32)],
        scratch_shapes=[pltpu.VMEM((t, t), F32), pltpu.VMEM((t, t), F32), pltpu.VMEM((1, t), F32),
                        pltpu.VMEM((dvp, t), F32)],
        compiler_params=_cparams("arbitrary", "arbitrary"),
        name="mla_attention",
    )(q, k, v, *side, silu_c, ada_w, ada_b)


def _bdot(a, b):
    return lax.dot_general(a, b, (((2,), (1,)), ((0,), (0,))), preferred_element_type=F32)


def _bdot_nt(a, b):
    return lax.dot_general(a, b, (((2,), (2,)), ((0,), (0,))), preferred_element_type=F32)


def _bdot_tn(a, b):
    return lax.dot_general(a, b, (((1,), (1,)), ((0,), (0,))), preferred_element_type=F32)


def _unit_lower_inverse(low, eye):
    n = low.shape[-1]
    p = low
    r = eye - low
    for _ in range(int(math.log2(n)) - 1):
        pb = p.astype(BF16)
        p = _bdot(pb, pb)
        r = r + _bdot(r.astype(BF16), p.astype(BF16))
    return r


def _gdn_prep_kernel(q_ref, k_ref, v_ref, qh_ref, kh_ref, vh_ref, sm_ref, cw_ref, alog_ref, dtb_ref,
                     qd_out, kd_out, u_out, w_out, a_out, gc_out, hist_ref):
    c, d, nh = CHUNK, GDN_HEAD_DIM, GDN_HEADS
    tm = q_ref.shape[0]
    nc = tm // c
    halo = qh_ref.shape[0]

    first = pl.program_id(0) == 0
    for n, (cur, prev) in enumerate(((q_ref, qh_ref), (k_ref, kh_ref), (v_ref, vh_ref))):
        lanes = slice(n * GDN_DIM, (n + 1) * GDN_DIM)
        hist_ref[pl.ds(0, halo), lanes] = jnp.where(first, 0.0, prev[...].astype(F32))
        hist_ref[pl.ds(halo, tm), lanes] = cur[...].astype(F32)
    base = halo - (GDN_CONV - 1)
    y = hist_ref[pl.ds(base, tm), :] * cw_ref[0:1, :]
    for j in range(1, GDN_CONV):
        y = y + hist_ref[pl.ds(base + j, tm), :] * cw_ref[j:j + 1, :]
    y = _silu(y)

    sm = sm_ref[...]
    beta_all = jax.nn.sigmoid(sm)
    g_all = -jnp.exp(alog_ref[...]) * _softplus(sm + dtb_ref[...])
    row = lax.broadcasted_iota(jnp.int32, (tm, tm), 0)
    col = lax.broadcasted_iota(jnp.int32, (tm, tm), 1)
    tri = jnp.where((col <= row) & (row // c == col // c), 1.0, 0.0).astype(BF16)
    gc_all = _cumsum_rows(tri, g_all)
    gc_out[...] = gc_all
    zpad = jnp.zeros((LANES - c, LANES), F32)
    gc_t = [jnp.concatenate([gc_all[ci * c:(ci + 1) * c], zpad], axis=0).T for ci in range(nc)]

    def heads(arr, off):
        return jnp.stack([arr[ci * c:(ci + 1) * c, off + h * d:off + (h + 1) * d]
                          for ci in range(nc) for h in range(nh)])

    def cols(arr, off):
        return jnp.stack([arr[ci * c:(ci + 1) * c, off + h:off + h + 1] for ci in range(nc) for h in range(nh)])

    q = heads(y, 0)
    k = heads(y, GDN_DIM)
    v = heads(y, 2 * GDN_DIM)
    q = q * lax.rsqrt(jnp.sum(q * q, axis=-1, keepdims=True) + EPS) * (1.0 / math.sqrt(d))
    k = k * lax.rsqrt(jnp.sum(k * k, axis=-1, keepdims=True) + EPS)
    beta = cols(beta_all, 0)
    gc = cols(gc_all, nh)
    gcr = jnp.stack([gc_t[ci][nh + h:nh + h + 1, 0:c] for ci in range(nc) for h in range(nh)])
    r2 = lax.broadcasted_iota(jnp.int32, (c, c), 0)
    c2 = lax.broadcasted_iota(jnp.int32, (c, c), 1)
    decay = jnp.exp(jnp.where(c2 <= r2, gc - gcr, -jnp.inf))
    eye = jnp.where(c2 == r2, 1.0, 0.0).astype(F32)
    kbeta = k * beta
    kb16 = k.astype(BF16)
    low = jnp.where(c2 < r2, _bdot_nt(kbeta.astype(BF16), kb16) * decay, 0.0)
    t_inv = _unit_lower_inverse(low, eye)
    egc = jnp.exp(gc)
    rhs = jnp.concatenate([v * beta, kbeta * egc], axis=-1)
    sol = _bdot(t_inv.astype(BF16), rhs.astype(BF16))
    intra = _bdot_nt(q.astype(BF16), kb16) * decay
    g_last = gcr[:, :, c - 1:c]
    k_dec = k * jnp.exp(g_last - gc)
    q_dec = q * egc
    for ci in range(nc):
        rows = slice(ci * c, (ci + 1) * c)
        for h in range(nh):
            b = ci * nh + h
            lanes = slice(h * d, (h + 1) * d)
            qd_out[rows, lanes] = q_dec[b].astype(BF16)
            kd_out[rows, lanes] = k_dec[b].astype(BF16)
            u_out[rows, lanes] = sol[b, :, :d]
            w_out[rows, lanes] = sol[b, :, d:].astype(BF16)
            a_out[h, rows, :] = intra[b].astype(BF16)


def _gdn_prep(main, small, conv_w, alog, dtb, *, tm, halo=8):
    s = main.shape[0]
    full = lambda a: pl.BlockSpec(a.shape, lambda i: (0,) * a.ndim)
    blk = lambda jcol: pl.BlockSpec((tm, GDN_DIM), lambda i: (i, jcol))
    prev = lambda jcol: pl.BlockSpec((halo, GDN_DIM), lambda i: (jnp.maximum(i * (tm // halo) - 1, 0), jcol))
    act = jax.ShapeDtypeStruct((s, GDN_DIM), BF16)
    return pl.pallas_call(
        _gdn_prep_kernel,
        grid=(s // tm,),
        in_specs=[blk(1), blk(2), blk(3), prev(1), prev(2), prev(3), pl.BlockSpec((tm, LANES), lambda i: (i, 2)),
                  full(conv_w), full(alog), full(dtb)],
        out_specs=[blk(0), blk(0), blk(0), blk(0),
                   pl.BlockSpec((GDN_HEADS, tm, CHUNK), lambda i: (0, i, 0)),
                   pl.BlockSpec((tm, LANES), lambda i: (i, 0))],
        out_shape=[act, act, jax.ShapeDtypeStruct((s, GDN_DIM), F32), act,
                   jax.ShapeDtypeStruct((GDN_HEADS, s, CHUNK), BF16), jax.ShapeDtypeStruct((s, LANES), F32)],
        scratch_shapes=[pltpu.VMEM((tm + halo, 3 * GDN_DIM), F32)],
        compiler_params=_cparams("parallel"),
        name="gdn_prep",
    )(main, main, main, main, main, main, small, conv_w, alog, dtb)


def _gdn_scan_kernel(qd_ref, kd_ref, u_ref, w_ref, a_ref, gc_ref, z_ref, gn_ref, o_ref, state_ref):
    c, d, nh = CHUNK, GDN_HEAD_DIM, GDN_HEADS

    @pl.when(pl.program_id(0) == 0)
    def _():
        state_ref[...] = jnp.zeros(state_ref.shape, F32)

    def heads(ref, rows):
        return jnp.stack([ref[rows, h * d:(h + 1) * d] for h in range(nh)])

    for ci in range(qd_ref.shape[0] // c):
        rows = slice(ci * c, (ci + 1) * c)
        last = ci * c + c - 1
        g_last = jnp.stack([gc_ref[last:last + 1, nh + h:nh + h + 1] for h in range(nh)])
        st = state_ref[...]
        stb = st.astype(BF16)
        v_new = heads(u_ref, rows) - _bdot(heads(w_ref, rows), stb)
        v_newb = v_new.astype(BF16)
        o = _bdot(heads(qd_ref, rows), stb) + _bdot(a_ref[:, rows, :], v_newb)
        state_ref[...] = st * jnp.exp(g_last) + _bdot_tn(heads(kd_ref, rows), v_newb)
        o = _rms(o) * gn_ref[...]
        for h in range(nh):
            lanes = slice(h * d, (h + 1) * d)
            o_ref[rows, lanes] = (o[h] * _silu(z_ref[rows, lanes].astype(F32))).astype(o_ref.dtype)


def _gdn_scan(qd, kd, u, w, a, gc, main, gnorm, *, tm):
    s = qd.shape[0]
    blk = lambda jcol: pl.BlockSpec((tm, GDN_DIM), lambda i: (i, jcol))
    return pl.pallas_call(
        _gdn_scan_kernel,
        grid=(s // tm,),
        in_specs=[blk(0), blk(0), blk(0), blk(0),
                  pl.BlockSpec((GDN_HEADS, tm, CHUNK), lambda i: (0, i, 0)),
                  pl.BlockSpec((tm, LANES), lambda i: (i, 0)),
                  blk(4), pl.BlockSpec(gnorm.shape, lambda i: (0, 0))],
        out_specs=blk(0),
        out_shape=jax.ShapeDtypeStruct((s, GDN_DIM), BF16),
        scratch_shapes=[pltpu.VMEM((GDN_HEADS, GDN_HEAD_DIM, GDN_HEAD_DIM), F32)],
        compiler_params=_cparams("arbitrary"),
        name="gdn_scan",
    )(qd, kd, u, w, a, gc, main, gnorm)


def _gla_kernel(q_ref, k_ref, v_ref, r_ref, sm_ref, wg1_ref, wg2_ref, bg_ref, gn_ref, o_ref, state_ref):
    c = CHUNK
    dk = q_ref.shape[1] // GLA_HEADS
    dv = v_ref.shape[1] // GLA_HEADS

    @pl.when(pl.program_id(0) == 0)
    def _():
        state_ref[...] = jnp.zeros(state_ref.shape, F32)

    a1, a2, _ = _split3(sm_ref[...])
    w1, w2 = wg1_ref[...], wg2_ref[...]
    pre = _dot(a1, w1) + _dot(a1, w2) + _dot(a2, w1) + bg_ref[...]
    gk = (jnp.minimum(pre, 0.0) - jnp.log1p(jnp.exp(-jnp.abs(pre)))) * (1.0 / GLA_GATE_NORMALIZER)
    tm = q_ref.shape[0]
    trow = lax.broadcasted_iota(jnp.int32, (tm, tm), 0)
    tcol = lax.broadcasted_iota(jnp.int32, (tm, tm), 1)
    tri = jnp.where((tcol <= trow) & (trow // c == tcol // c), 1.0, 0.0).astype(BF16)
    b_tile = _cumsum_rows(tri, gk)
    row = lax.broadcasted_iota(jnp.int32, (c, c), 0)
    col = lax.broadcasted_iota(jnp.int32, (c, c), 1)
    sub_row = lax.broadcasted_iota(jnp.int32, (GLA_SUB, 1), 0)
    sub_col = lax.broadcasted_iota(jnp.int32, (GLA_SUB, GLA_SUB), 1)

    def chunk_factored(rows, b_all):
        heads = lambda ref, w: jnp.stack([ref[rows, h * w:(h + 1) * w] for h in range(GLA_HEADS)])
        bh = jnp.stack([b_all[:, h * dk:(h + 1) * dk] for h in range(GLA_HEADS)])
        q = heads(q_ref, dk).astype(F32) * (1.0 / math.sqrt(dk))
        k = heads(k_ref, dk).astype(F32)
        v = heads(v_ref, dv)
        b0, b_last = bh[:, 0:1], bh[:, c - 1:c]
        q_t = q * jnp.exp(bh - b0)
        k_t = k * jnp.exp(b0 - bh)
        a = jnp.where(col <= row, _bdot_nt(q_t.astype(BF16), k_t.astype(BF16)), 0.0)
        q_dec = q_t * jnp.exp(b0)
        k_dec = k_t * jnp.exp(b_last - b0)
        st = state_ref[...]
        o = _bdot_nt(q_dec.astype(BF16), st.astype(BF16)) + _bdot(a.astype(BF16), v)
        state_ref[...] = st * jnp.exp(b_last) + _bdot_tn(v, k_dec.astype(BF16))
        o = _rms(o) * gn_ref[...]
        for h in range(GLA_HEADS):
            lanes = slice(h * dv, (h + 1) * dv)
            o_ref[rows, lanes] = (o[h] * _silu(r_ref[rows, lanes].astype(F32))).astype(o_ref.dtype)

    def intra_exact(qh, kh, bh, vh):
        o_rows = []
        for blk in range(c // GLA_SUB):
            r0 = blk * GLA_SUB
            qs, bs = qh[r0:r0 + GLA_SUB], bh[r0:r0 + GLA_SUB]
            ks = kh[r0:r0 + GLA_SUB]
            a_diag = jnp.zeros((GLA_SUB, GLA_SUB), F32)
            for j in range(GLA_SUB):
                e = jnp.exp(jnp.where(sub_row >= j, bs - bs[j:j + 1], -jnp.inf))
                dj = jnp.sum(qs * ks[j:j + 1] * e, axis=-1, keepdims=True)
                a_diag = jnp.where(sub_col == j, dj, a_diag)
            o_blk = _dot(a_diag.astype(BF16), vh[r0:r0 + GLA_SUB])
            if blk > 0:
                b_ref0 = bs[0:1]
                q_t = qs * jnp.exp(bs - b_ref0)
                k_t = kh[0:r0] * jnp.exp(b_ref0 - bh[0:r0])
                a_off = _dot_nt(q_t.astype(BF16), k_t.astype(BF16))
                o_blk = o_blk + _dot(a_off.astype(BF16), vh[0:r0])
            o_rows.append(o_blk)
        return jnp.concatenate(o_rows, axis=0), qh * jnp.exp(bh), kh * jnp.exp(bh[c - 1:c] - bh)

    def chunk_exact(rows, b_all):
        for h in range(GLA_HEADS):
            bh = b_all[:, h * dk:(h + 1) * dk]
            qh = q_ref[rows, h * dk:(h + 1) * dk].astype(F32) * (1.0 / math.sqrt(dk))
            kh = k_ref[rows, h * dk:(h + 1) * dk].astype(F32)
            vh = v_ref[rows, h * dv:(h + 1) * dv]
            st = state_ref[h]
            o_intra, q_dec, k_dec = intra_exact(qh, kh, bh, vh)
            o = _dot_nt(q_dec.astype(BF16), st.astype(BF16)) + o_intra
            state_ref[h] = st * jnp.exp(bh[c - 1:c]) + _dot_tn(vh, k_dec.astype(BF16))
            o = _rms(o) * gn_ref[...] * _silu(r_ref[rows, h * dv:(h + 1) * dv].astype(F32))
            o_ref[rows, h * dv:(h + 1) * dv] = o.astype(o_ref.dtype)

    for ci in range(tm // c):
        rows = slice(ci * c, (ci + 1) * c)
        b_all = b_tile[rows]
        decay_range = jnp.max(b_all[0:1, :] - b_all[c - 1:c, :])
        factorable = decay_range <= GLA_FACTOR_RANGE
        pl.when(factorable)(functools.partial(chunk_factored, rows, b_all))
        pl.when(jnp.logical_not(factorable))(functools.partial(chunk_exact, rows, b_all))


def _gla(main, small, wg, bg, gnorm, *, key_dim, value_dim, tm):
    s = main.shape[0]
    full = lambda a: pl.BlockSpec(a.shape, lambda i: (0,) * a.ndim)
    wg1 = wg.astype(BF16)
    wg2 = (wg - wg1.astype(F32)).astype(BF16)
    return pl.pallas_call(
        _gla_kernel,
        grid=(s // tm,),
        in_specs=[
            pl.BlockSpec((tm, key_dim), lambda i: (i, 0)),
            pl.BlockSpec((tm, key_dim), lambda i: (i, 1)),
            pl.BlockSpec((tm, value_dim), lambda i: (i, 1)),
            pl.BlockSpec((tm, value_dim), lambda i: (i, 2)),
            pl.BlockSpec((tm, LANES), lambda i: (i, 0)),
            full(wg1), full(wg2), full(bg), full(gnorm),
        ],
        out_specs=pl.BlockSpec((tm, value_dim), lambda i: (i, 0)),
        out_shape=jax.ShapeDtypeStruct((s, value_dim), BF16),
        scratch_shapes=[pltpu.VMEM((GLA_HEADS, value_dim // GLA_HEADS, key_dim // GLA_HEADS), F32)],
        compiler_params=_cparams("arbitrary"),
        name="gated_linear_attention",
    )(main, main, main, main, small, wg1, wg2, bg, gnorm)


def _out_proj_kernel(a1_ref, a2_ref, w_ref, x_ref, gate_ref, o_ref):
    half = a1_ref.shape[1]
    mix = _dot(a1_ref[...], w_ref[0:half, :]) + _dot(a2_ref[...], w_ref[half:, :])
    o_ref[...] = x_ref[...] + gate_ref[...] * mix


def _out_proj(a1, a2, col1, col2, w, layer, x, gate, *, tm):
    s, d = x.shape
    half = w.shape[1] // 2
    return pl.pallas_call(
        _out_proj_kernel,
        grid=(s // tm,),
        in_specs=[
            pl.BlockSpec((tm, half), lambda i: (i, col1)),
            pl.BlockSpec((tm, half), lambda i: (i, col2)),
            pl.BlockSpec((None,) + w.shape[1:], lambda i: (layer, 0, 0)),
            pl.BlockSpec((tm, d), lambda i: (i, 0)),
            pl.BlockSpec((1, d), lambda i: (0, 0)),
        ],
        out_specs=pl.BlockSpec((tm, d), lambda i: (i, 0)),
        out_shape=jax.ShapeDtypeStruct((s, d), F32),
        compiler_params=_cparams("parallel"),
        name="out_proj_residual",
    )(a1, a2, w, x, gate)


def _ffn_kernel(x_ref, g_ref, sc_ref, sh_ref, gate_ref, fin_ref, w1_ref, w3_ref, w2_ref, o_ref, h_ref, acc_ref,
                *, final_norm):
    j = pl.program_id(1)

    @pl.when(j == 0)
    def _():
        h_ref[...] = _modulated_norm(x_ref[...], g_ref[...], sc_ref[...], sh_ref[...]).astype(BF16)
        acc_ref[...] = jnp.zeros(acc_ref.shape, F32)

    h = h_ref[...]
    a = _dot(h, w1_ref[...])
    t = (_silu(a) * _dot(h, w3_ref[...])).astype(BF16)
    acc_ref[...] += _dot(t, w2_ref[...])

    @pl.when(j == pl.num_programs(1) - 1)
    def _():
        y = x_ref[...] + gate_ref[...] * acc_ref[...]
        if final_norm:
            y = _rms(y) * fin_ref[...]
        o_ref[...] = y


def _ffn(x, g, sc, sh, gate, fin, w1, w3, w2, layer, *, tm, tf, final_norm):
    s, d = x.shape
    dff = w1.shape[2]
    vec = pl.BlockSpec((1, d), lambda i, j: (0, 0))
    return pl.pallas_call(
        functools.partial(_ffn_kernel, final_norm=final_norm),
        grid=(s // tm, dff // tf),
        in_specs=[
            pl.BlockSpec((tm, d), lambda i, j: (i, 0)),
            vec, vec, vec, vec, vec,
            pl.BlockSpec((None, d, tf), lambda i, j: (layer, 0, j)),
            pl.BlockSpec((None, d, tf), lambda i, j: (layer, 0, j)),
            pl.BlockSpec((None, tf, d), lambda i, j: (layer, j, 0)),
        ],
        out_specs=pl.BlockSpec((tm, d), lambda i, j: (i, 0)),
        out_shape=jax.ShapeDtypeStruct((s, d), F32),
        scratch_shapes=[pltpu.VMEM((tm, d), BF16), pltpu.VMEM((tm, d), F32)],
        compiler_params=_cparams("parallel", "arbitrary"),
        name="swiglu_ffn",
    )(x, g, sc, sh, gate, fin, w1, w3, w2)


def _rot_cols(w):
    half = w.shape[-1] // 2
    return jnp.concatenate([-w[..., half:], w[..., :half]], axis=-1)


def _pad_cols(w, n):
    return jnp.pad(w, [(0, 0)] * (w.ndim - 1) + [(0, n - w.shape[-1])])


def _pad_lanes(p):
    return jnp.concatenate([p, jnp.zeros((p.shape[0], LANES - p.shape[1]), p.dtype)], axis=-1)


def _pad_rows(p):
    return jnp.concatenate([p, jnp.zeros((LANES - p.shape[0], p.shape[1]), p.dtype)], axis=0)


def _even_in_layout_kernel(w_ref, main_ref, small_ref):
    o_kr = MLA_Q_RANK + MLA_KV_RANK
    o_qkv = o_kr + MLA_ROPE
    o_ba = o_qkv + 4 * GDN_DIM
    half = MLA_ROPE // 2
    w = w_ref[...]
    main_ref[:o_kr, :] = w[:o_kr].astype(BF16)
    main_ref[o_kr:, :] = w[o_qkv:o_ba].astype(BF16)
    kr = w[o_kr:o_qkv]
    rot = jnp.concatenate([-kr[half:], kr[:half]], axis=0)
    small_ref[...] = jnp.concatenate([_pad_rows(kr), _pad_rows(rot), _pad_rows(w[o_ba:])], axis=0).astype(BF16)


def _gla_in_layout_kernel(w_ref, main_ref, small_ref):
    n = main_ref.shape[0]
    w = w_ref[...]
    main_ref[...] = w[:n].astype(BF16)
    small_ref[...] = _pad_rows(w[n:]).astype(BF16)


def _in_weight_layout(body, w_in, n_main, n_small, *, tc=256):
    w_t = jnp.swapaxes(w_in, 1, 2)
    nl, n, d = w_t.shape
    return pl.pallas_call(
        body,
        grid=(nl, d // tc),
        in_specs=[pl.BlockSpec((None, n, tc), lambda l, i: (l, 0, i))],
        out_specs=[pl.BlockSpec((None, n_main, tc), lambda l, i: (l, 0, i)),
                   pl.BlockSpec((None, n_small, tc), lambda l, i: (l, 0, i))],
        out_shape=[jax.ShapeDtypeStruct((nl, n_main, d), BF16), jax.ShapeDtypeStruct((nl, n_small, d), BF16)],
        compiler_params=_cparams("parallel", "parallel"),
        name="in_weight_layout",
    )(w_t)


def _mla_weights(w_uq, w_ukv):
    r = w_uq.shape[0]
    wq = w_uq.reshape(r, MLA_HEADS, MLA_QK)
    nope, rope = wq[..., :MLA_NOPE], wq[..., MLA_NOPE:]
    flat = lambda t: t.reshape(r, MLA_HEADS * LANES)
    wq_all = jnp.concatenate([flat(nope), flat(_pad_cols(rope, LANES)), flat(_pad_cols(_rot_cols(rope), LANES))], axis=1)
    wkv = w_ukv.reshape(w_ukv.shape[0], MLA_HEADS, MLA_NOPE + MLA_V)
    wkv_all = jnp.concatenate([flat(wkv[..., :MLA_NOPE]), flat(wkv[..., MLA_NOPE:])], axis=1)
    return wq_all.astype(BF16), wkv_all.astype(BF16)


def kernel(x, c, positions, norm_g, ada_w, ada_b, ab_w_in, mla_q_norm, mla_w_uq, mla_kv_norm, mla_w_ukv, gdn_conv_w,
           gdn_a_log, gdn_dt_bias, gdn_norm, ab_w_out, gla_w_in, gla_w_gk2, gla_b_gk2, gla_norm, gla_w_out, ffn_w1,
           ffn_w3, ffn_w2, final_norm):
    batch, s, d = x.shape
    assert batch == 1 and c.shape == (1, d)
    depth = norm_g.shape[0]
    key_dim = gla_w_gk2.shape[2]
    value_dim = gla_w_out.shape[1]

    ada_w2, ada_b2 = ada_w.reshape(depth * 2, d, 3 * d), ada_b.reshape(depth * 2, 1, 3 * d)
    mods0, silu_c = _ada_modulation(c.reshape(d, 1), ada_w2, ada_b2, 1)
    mods = {0: mods0[0]}
    xs = x.reshape(s, d)
    pos_col = positions.reshape(s, 1)
    half = MLA_ROPE // 2
    inv_freq = ROPE_THETA ** (-jnp.arange(half, dtype=F32) / half)
    invf = jnp.tile(inv_freq, LANES // half).reshape(1, LANES)
    row = lambda v: v.reshape(1, -1)

    n_gla = 2 * key_dim + 2 * value_dim
    n_even = MLA_Q_RANK + MLA_KV_RANK + 4 * GDN_DIM
    ab_main_w, ab_small_w = _in_weight_layout(_even_in_layout_kernel, ab_w_in, n_even, 3 * LANES)
    gla_main_w, gla_small_w = _in_weight_layout(_gla_in_layout_kernel, gla_w_in, n_gla, LANES)
    ab_out_w, gla_out_w = ab_w_out.astype(BF16), gla_w_out.astype(BF16)
    assert depth % 2 == 0

    for layer in range(depth):
        i = layer // 2
        m = mods[2 * layer]
        shift, scale, gate = m[:, :d], m[:, d:2 * d], m[:, 2 * d:]
        gain = row(norm_g[layer, 0])
        if layer % 2 == 0:
            main, small = _norm_proj(xs, gain, scale, shift, ab_main_w, ab_small_w, i, tm=1024, tn=1024)
            wq, wkv = _mla_weights(mla_w_uq[i], mla_w_ukv[i])
            q, k, v = _mla_up(main, small, pos_col, invf, row(mla_q_norm[i]), row(mla_kv_norm[i]), wq, wkv, tm=256)
            first = 2 * layer + 1
            o_a, w1, w3, w2, later = _causal_attention(q, k, v, (ffn_w1, ffn_w3, ffn_w2), layer, 2,
                                                       (silu_c, ada_w2, ada_b2, first, 4), t=1024)
            mods.update({first + n: later[n] for n in range(4)})
            lane_pad = lambda vec: jnp.pad(vec, (GDN_HEADS, LANES - 2 * GDN_HEADS)).reshape(1, LANES)
            qd, kd, u, w, a, gc = _gdn_prep(main, small, gdn_conv_w[i], lane_pad(gdn_a_log[i]),
                                            lane_pad(gdn_dt_bias[i]), tm=256)
            o_b = _gdn_scan(qd, kd, u, w, a, gc, main, row(gdn_norm[i]), tm=512)
            xs = _out_proj(o_a, o_b, 0, 0, ab_out_w, i, xs, gate, tm=512)
        else:
            main, small = _norm_proj(xs, gain, scale, shift, gla_main_w, gla_small_w, i, tm=1024, tn=1024)
            wg = jnp.pad(gla_w_gk2[i], ((0, LANES - GLA_GATE_RANK), (0, 0)))
            o_c = _gla(main, small, wg, row(gla_b_gk2[i]), row(gla_norm[i]), key_dim=key_dim, value_dim=value_dim,
                       tm=256)
            xs = _out_proj(o_c, o_c, 0, 1, gla_out_w, i, xs, gate, tm=512)
        m = mods[2 * layer + 1]
        shift, scale, gate = m[:, :d], m[:, d:2 * d], m[:, 2 * d:]
        xs = _ffn(xs, row(norm_g[layer, 1]), scale, shift, gate, row(final_norm), w1, w3, w2, layer % 2, tm=512,
                  tf=512,
                  final_norm=(layer == depth - 1))
    return xs.reshape(batch, s, d)
```

```python
import functools
import math

import jax
import jax.numpy as jnp
from jax import lax
from jax.experimental import pallas as pl
from jax.experimental.pallas import tpu as pltpu

F32 = jnp.float32
BF16 = jnp.bfloat16

EPS = 1e-6
ROPE_THETA = 10000.0
MLA_HEADS = 8
MLA_NOPE = 128
MLA_ROPE = 64
MLA_V = 128
MLA_QK = MLA_NOPE + MLA_ROPE
MLA_QK_PAD = 256
MLA_V_PAD = MLA_V + 16
MLA_Q_RANK = 512
MLA_KV_RANK = 512
GDN_HEADS = 8
GDN_HEAD_DIM = 128
GDN_DIM = GDN_HEADS * GDN_HEAD_DIM
GDN_CONV = 4
GLA_HEADS = 4
GLA_GATE_RANK = 16
GLA_GATE_NORMALIZER = 16.0
GLA_SUB = 16
GLA_FACTOR_RANGE = 60.0
CHUNK = 64
LANES = 128
VMEM_LIMIT = 48 * 1024 * 1024


def _cparams(*sem):
    return pltpu.CompilerParams(dimension_semantics=sem, vmem_limit_bytes=VMEM_LIMIT)


def _dot(a, b):
    return jnp.dot(a, b, preferred_element_type=F32)


def _dot_nt(a, b):
    return lax.dot_general(a, b, (((1,), (1,)), ((), ())), preferred_element_type=F32)


def _dot_tn(a, b):
    return lax.dot_general(a, b, (((0,), (0,)), ((), ())), preferred_element_type=F32)


def _split3(x):
    x1 = x.astype(BF16)
    r1 = x - x1.astype(F32)
    x2 = r1.astype(BF16)
    x3 = (r1 - x2.astype(F32)).astype(BF16)
    return x1, x2, x3


def _cumsum_rows(tri, x):
    x1, x2, x3 = _split3(x)
    return _dot(tri, x1) + _dot(tri, x2) + _dot(tri, x3)


def _silu(x):
    return x * jax.nn.sigmoid(x)


def _softplus(x):
    return jnp.maximum(x, 0.0) + jnp.log1p(jnp.exp(-jnp.abs(x)))


def _rms(x):
    return x * lax.rsqrt(jnp.mean(x * x, axis=-1, keepdims=True) + EPS)


def _ada_kernel(c_ref, w_ref, b_ref, o_ref, s_ref):
    s = _silu(c_ref[...])
    o_ref[0] = jnp.sum(w_ref[0] * s, axis=0, keepdims=True) + b_ref[0]
    s_ref[...] = jnp.broadcast_to(s, s_ref.shape)


def _ada_modulation(c_col, w, b, n_mod, *, tn=768):
    _, d, n = w.shape
    return pl.pallas_call(
        _ada_kernel,
        grid=(n_mod, n // tn),
        in_specs=[
            pl.BlockSpec((d, 1), lambda l, j: (0, 0)),
            pl.BlockSpec((1, d, tn), lambda l, j: (l, 0, j)),
            pl.BlockSpec((1, 1, tn), lambda l, j: (l, 0, j)),
        ],
        out_specs=[pl.BlockSpec((1, 1, tn), lambda l, j: (l, 0, j)), pl.BlockSpec((d, LANES), lambda l, j: (0, 0))],
        out_shape=[jax.ShapeDtypeStruct((n_mod, 1, n), F32), jax.ShapeDtypeStruct((d, LANES), F32)],
        compiler_params=_cparams("arbitrary", "arbitrary"),
        name="ada_modulation",
    )(c_col, w, b)


def _modulated_norm(x, g, sc, sh):
    return _rms(x) * (g * (1.0 + sc)) + sh


def _norm_proj_kernel(x_ref, g_ref, sc_ref, sh_ref, w_ref, ws_ref, o_ref, os_ref, h_ref):
    @pl.when(pl.program_id(1) == 0)
    def _():
        h = _modulated_norm(x_ref[...], g_ref[...], sc_ref[...], sh_ref[...]).astype(BF16)
        h_ref[...] = h
        os_ref[...] = _dot_nt(h, ws_ref[...])

    o_ref[...] = _dot_nt(h_ref[...], w_ref[...]).astype(o_ref.dtype)


def _norm_proj(x, g, sc, sh, w, ws, layer, *, tm, tn):
    s, d = x.shape
    n, ns = w.shape[1], ws.shape[1]
    assert s % tm == 0 and n % tn == 0
    vec = pl.BlockSpec((1, d), lambda i, j: (0, 0))
    return pl.pallas_call(
        _norm_proj_kernel,
        grid=(s // tm, n // tn),
        in_specs=[
            pl.BlockSpec((tm, d), lambda i, j: (i, 0)),
            vec, vec, vec,
            pl.BlockSpec((None, tn, d), lambda i, j: (layer, j, 0)),
            pl.BlockSpec((None, ns, d), lambda i, j: (layer, 0, 0)),
        ],
        out_specs=[
            pl.BlockSpec((tm, tn), lambda i, j: (i, j)),
            pl.BlockSpec((tm, ns), lambda i, j: (i, 0)),
        ],
        out_shape=[jax.ShapeDtypeStruct((s, n), BF16), jax.ShapeDtypeStruct((s, ns), F32)],
        scratch_shapes=[pltpu.VMEM((tm, d), BF16)],
        compiler_params=_cparams("parallel", "arbitrary"),
        name="norm_proj",
    )(x, g, sc, sh, w, ws)


def _mla_up_kernel(cq_ref, ckv_ref, kr_ref, krr_ref, pos_ref, invf_ref, qn_ref, kvn_ref, wq_ref, wkv_ref,
                   q_out, k_out, v_out):
    hd = MLA_HEADS * LANES
    cqn = (_rms(cq_ref[...].astype(F32)) * qn_ref[...]).astype(BF16)
    ckvn = (_rms(ckv_ref[...].astype(F32)) * kvn_ref[...]).astype(BF16)
    qall = _dot(cqn, wq_ref[...])
    kv = _dot(ckvn, wkv_ref[...])
    ang = pos_ref[...].astype(F32) * invf_ref[...]
    cos, sin = jnp.cos(ang), jnp.sin(ang)
    k_rope = kr_ref[...] * cos + krr_ref[...] * sin
    scale = math.log2(math.e) / math.sqrt(MLA_QK)
    ones = jnp.ones((MLA_V_PAD - MLA_V, cq_ref.shape[0]), F32)
    for h in range(MLA_HEADS):
        lo, hi = h * LANES, (h + 1) * LANES
        q_rope = qall[:, hd + lo:hd + hi] * cos + qall[:, 2 * hd + lo:2 * hd + hi] * sin
        q_out[h] = (jnp.concatenate([qall[:, lo:hi], q_rope], axis=-1) * scale).astype(BF16)
        k_out[h] = jnp.concatenate([kv[:, lo:hi], k_rope], axis=-1).astype(BF16)
        v_out[h] = jnp.concatenate([kv[:, hd + lo:hd + hi].T, ones], axis=0).astype(BF16)


def _mla_up(main, small, pos_col, invf, qn, kvn, wq, wkv, *, tm):
    s = main.shape[0]
    full = lambda a: pl.BlockSpec(a.shape, lambda i: (0,) * a.ndim)
    qk_shape = jax.ShapeDtypeStruct((MLA_HEADS, s, MLA_QK_PAD), BF16)
    return pl.pallas_call(
        _mla_up_kernel,
        grid=(s // tm,),
        in_specs=[
            pl.BlockSpec((tm, MLA_Q_RANK), lambda i: (i, 0)),
            pl.BlockSpec((tm, MLA_KV_RANK), lambda i: (i, 1)),
            pl.BlockSpec((tm, LANES), lambda i: (i, 0)),
            pl.BlockSpec((tm, LANES), lambda i: (i, 1)),
            pl.BlockSpec((tm, 1), lambda i: (i, 0)),
            full(invf), full(qn), full(kvn), full(wq), full(wkv),
        ],
        out_specs=[
            pl.BlockSpec((MLA_HEADS, tm, MLA_QK_PAD), lambda i: (0, i, 0)),
            pl.BlockSpec((MLA_HEADS, tm, MLA_QK_PAD), lambda i: (0, i, 0)),
            pl.BlockSpec((MLA_HEADS, MLA_V_PAD, tm), lambda i: (0, 0, i)),
        ],
        out_shape=[qk_shape, qk_shape, jax.ShapeDtypeStruct((MLA_HEADS, MLA_V_PAD, s), BF16)],
        compiler_params=_cparams("parallel"),
        name="mla_up",
    )(main, main, small, small, pos_col, invf, qn, kvn, wq, wkv)


def _attn_kernel(q_ref, k_ref, v_ref, *rest, t, n_side):
    side_in, (sc_ref, aw_ref, ab_ref) = rest[:n_side], rest[n_side:n_side + 3]
    (o_ref, *side_out), mods_ref = rest[n_side + 3:2 * n_side + 4], rest[2 * n_side + 4]
    s0_ref, s1_ref, m_ref, acc_ref = rest[2 * n_side + 5:]
    for src, dst in zip(side_in, side_out):
        dst[...] = src[...].astype(dst.dtype)
    silu_c = jnp.concatenate([sc_ref[...]] * (aw_ref.shape[1] // LANES), axis=1)
    mods_ref[...] = jnp.sum(aw_ref[...] * silu_c, axis=0, keepdims=True) + ab_ref[...]
    i = pl.program_id(1)
    q = q_ref[0]
    dv = MLA_V

    def tile(j):
        return pl.ds(pl.multiple_of(j * t, t), t)

    def scores(s_ref, j):
        s_ref[...] = _dot_nt(k_ref[0, tile(j), :], q)

    def update(s_ref, j, diagonal):
        s = s_ref[...]
        if diagonal:
            key = lax.broadcasted_iota(jnp.int32, (t, t), 0)
            qry = lax.broadcasted_iota(jnp.int32, (t, t), 1)
            s = jnp.where(key <= qry, s, -jnp.inf)
        m = m_ref[...]
        m_new = jnp.maximum(m, jnp.max(s, axis=0, keepdims=True))
        p = jnp.exp2(s - m_new)
        alpha = jnp.exp2(m - m_new)
        acc_ref[...] = alpha * acc_ref[...] + _dot(v_ref[0, :, tile(j)], p.astype(BF16))
        m_ref[...] = m_new

    m_ref[...] = jnp.full(m_ref.shape, -jnp.inf, F32)
    acc_ref[...] = jnp.zeros(acc_ref.shape, F32)
    scores(s0_ref, 0)

    def pair(jj, carry):
        j = 2 * jj
        scores(s1_ref, j + 1)
        update(s0_ref, j, False)
        scores(s0_ref, j + 2)
        update(s1_ref, j + 1, False)
        return carry

    lax.fori_loop(0, i // 2, pair, 0)

    @pl.when(i % 2 == 0)
    def _():
        update(s0_ref, i, True)

    @pl.when(i % 2 == 1)
    def _():
        scores(s1_ref, i)
        update(s0_ref, i - 1, False)
        update(s1_ref, i, True)

    acc = acc_ref[...]
    o_ref[...] = (acc[:dv] / acc[dv:dv + 1]).T.astype(o_ref.dtype)


def _causal_attention(q, k, v, side, layer0, n_layers, ada, *, t):
    h, s, dqk = q.shape
    dvp = v.shape[1]
    nq = s // t
    steps = h * nq
    assert s % t == 0 and steps % n_layers == 0
    step = lambda hh, i: hh * nq + i
    per_layer = steps // n_layers
    side_in, side_out, side_shapes = [], [], []
    for w in side:
        rows = w.shape[1] // per_layer
        assert rows * per_layer == w.shape[1] and rows % 16 == 0
        blk = (None, rows) + w.shape[2:]
        side_in.append(pl.BlockSpec(blk, lambda hh, i: (layer0 + step(hh, i) // per_layer, step(hh, i) % per_layer, 0)))
        side_out.append(pl.BlockSpec(blk, lambda hh, i: (step(hh, i) // per_layer, step(hh, i) % per_layer, 0)))
        side_shapes.append(jax.ShapeDtypeStruct((n_layers,) + w.shape[1:], BF16))
    silu_c, ada_w, ada_b, first, count = ada
    n_mod, d, n3 = ada_w.shape
    per_mod = steps // count
    slab = n3 // per_mod
    assert per_mod * count == steps and slab * per_mod == n3 and slab % LANES == 0
    mod_idx = lambda hh, i: jnp.minimum(first + step(hh, i) // per_mod, n_mod - 1)
    ada_in = [
        pl.BlockSpec((d, LANES), lambda hh, i: (0, 0)),
        pl.BlockSpec((None, d, slab), lambda hh, i: (mod_idx(hh, i), 0, step(hh, i) % per_mod)),
        pl.BlockSpec((None, 1, slab), lambda hh, i: (mod_idx(hh, i), 0, step(hh, i) % per_mod)),
    ]
    mods_out = pl.BlockSpec((None, 1, slab), lambda hh, i: (step(hh, i) // per_mod, 0, step(hh, i) % per_mod))
    return pl.pallas_call(
        functools.partial(_attn_kernel, t=t, n_side=len(side)),
        grid=(h, nq),
        in_specs=[
            pl.BlockSpec((1, t, dqk), lambda hh, i: (hh, i, 0)),
            pl.BlockSpec((1, s, dqk), lambda hh, i: (hh, 0, 0)),
            pl.BlockSpec((1, dvp, s), lambda hh, i: (hh, 0, 0)),
        ] + side_in + ada_in,
        out_specs=[pl.BlockSpec((t, MLA_V), lambda hh, i: (i, hh))] + side_out + [mods_out],
        out_shape=[jax.ShapeDtypeStruct((s, h * MLA_V), BF16)] + side_shapes
        + [jax.ShapeDtypeStruct((count, 1, n3), F32)],
        scratch_shapes=[pltpu.VMEM((t, t), F32), pltpu.VMEM((t, t), F32), pltpu.VMEM((1, t), F32),
                        pltpu.VMEM((dvp, t), F32)],
        compiler_params=_cparams("arbitrary", "arbitrary"),
        name="mla_attention",
    )(q, k, v, *side, silu_c, ada_w, ada_b)


def _bdot(a, b):
    return lax.dot_general(a, b, (((2,), (1,)), ((0,), (0,))), preferred_element_type=F32)


def _bdot_nt(a, b):
    return lax.dot_general(a, b, (((2,), (2,)), ((0,), (0,))), preferred_element_type=F32)


def _bdot_tn(a, b):
    return lax.dot_general(a, b, (((1,), (1,)), ((0,), (0,))), preferred_element_type=F32)


def _unit_lower_inverse(low, eye, blk=16):
    n = low.shape[-1]
    bdot = lambda a, b: _bdot(a.astype(BF16), b.astype(BF16))
    rb = lax.broadcasted_iota(jnp.int32, (n, n), 0) // blk
    cb = lax.broadcasted_iota(jnp.int32, (n, n), 1) // blk
    diag = jnp.where(rb == cb, low, 0.0)
    p = diag
    d_inv = eye - diag
    for _ in range(int(math.log2(blk)) - 1):
        p = bdot(p, p)
        d_inv = d_inv + bdot(d_inv, p)
    m = bdot(d_inv, low - diag)
    assert n // blk == 4
    i_m = eye - m
    return bdot(i_m + bdot(i_m, bdot(m, m)), d_inv)


def _gdn_prep_kernel(q_ref, k_ref, v_ref, qh_ref, kh_ref, vh_ref, sm_ref, cw_ref, alog_ref, dtb_ref,
                     qd_out, kd_out, u_out, w_out, a_out, gc_out, hist_ref):
    c, d, nh = CHUNK, GDN_HEAD_DIM, GDN_HEADS
    tm = q_ref.shape[0]
    nc = tm // c
    halo = qh_ref.shape[0]

    first = pl.program_id(0) == 0
    for n, (cur, prev) in enumerate(((q_ref, qh_ref), (k_ref, kh_ref), (v_ref, vh_ref))):
        lanes = slice(n * GDN_DIM, (n + 1) * GDN_DIM)
        hist_ref[pl.ds(0, halo), lanes] = jnp.where(first, 0.0, prev[...].astype(F32))
        hist_ref[pl.ds(halo, tm), lanes] = cur[...].astype(F32)
    base = halo - (GDN_CONV - 1)
    y = hist_ref[pl.ds(base, tm), :] * cw_ref[0:1, :]
    for j in range(1, GDN_CONV):
        y = y + hist_ref[pl.ds(base + j, tm), :] * cw_ref[j:j + 1, :]
    y = _silu(y)

    sm = sm_ref[...]
    beta_all = jax.nn.sigmoid(sm)
    g_all = -jnp.exp(alog_ref[...]) * _softplus(sm + dtb_ref[...])
    row = lax.broadcasted_iota(jnp.int32, (tm, tm), 0)
    col = lax.broadcasted_iota(jnp.int32, (tm, tm), 1)
    tri = jnp.where((col <= row) & (row // c == col // c), 1.0, 0.0).astype(BF16)
    gc_all = _cumsum_rows(tri, g_all)
    gc_out[...] = gc_all
    zpad = jnp.zeros((LANES - c, LANES), F32)
    gc_t = [jnp.concatenate([gc_all[ci * c:(ci + 1) * c], zpad], axis=0).T for ci in range(nc)]

    def heads(arr, off):
        return jnp.stack([arr[ci * c:(ci + 1) * c, off + h * d:off + (h + 1) * d]
                          for ci in range(nc) for h in range(nh)])

    def cols(arr, off):
        return jnp.stack([arr[ci * c:(ci + 1) * c, off + h:off + h + 1] for ci in range(nc) for h in range(nh)])

    q = heads(y, 0)
    k = heads(y, GDN_DIM)
    v = heads(y, 2 * GDN_DIM)
    q = q * lax.rsqrt(jnp.sum(q * q, axis=-1, keepdims=True) + EPS) * (1.0 / math.sqrt(d))
    k = k * lax.rsqrt(jnp.sum(k * k, axis=-1, keepdims=True) + EPS)
    beta = cols(beta_all, 0)
    gc = cols(gc_all, nh)
    gcr = jnp.stack([gc_t[ci][nh + h:nh + h + 1, 0:c] for ci in range(nc) for h in range(nh)])
    r2 = lax.broadcasted_iota(jnp.int32, (c, c), 0)
    c2 = lax.broadcasted_iota(jnp.int32, (c, c), 1)
    decay = jnp.exp(jnp.where(c2 <= r2, gc - gcr, -jnp.inf))
    eye = jnp.where(c2 == r2, 1.0, 0.0).astype(F32)
    kbeta = k * beta
    kb16 = k.astype(BF16)
    low = jnp.where(c2 < r2, _bdot_nt(kbeta.astype(BF16), kb16) * decay, 0.0)
    t_inv = _unit_lower_inverse(low, eye)
    egc = jnp.exp(gc)
    rhs = jnp.concatenate([v * beta, kbeta * egc], axis=-1)
    sol = _bdot(t_inv.astype(BF16), rhs.astype(BF16))
    intra = _bdot_nt(q.astype(BF16), kb16) * decay
    g_last = gcr[:, :, c - 1:c]
    k_dec = k * jnp.exp(g_last - gc)
    q_dec = q * egc
    for ci in range(nc):
        rows = slice(ci * c, (ci + 1) * c)
        for h in range(nh):
            b = ci * nh + h
            lanes = slice(h * d, (h + 1) * d)
            qd_out[rows, lanes] = q_dec[b].astype(BF16)
            kd_out[rows, lanes] = k_dec[b].astype(BF16)
            u_out[rows, lanes] = sol[b, :, :d]
            w_out[rows, lanes] = sol[b, :, d:].astype(BF16)
            a_out[h, rows, :] = intra[b].astype(BF16)


def _gdn_prep(main, small, conv_w, alog, dtb, *, tm, halo=8):
    s = main.shape[0]
    full = lambda a: pl.BlockSpec(a.shape, lambda i: (0,) * a.ndim)
    blk = lambda jcol: pl.BlockSpec((tm, GDN_DIM), lambda i: (i, jcol))
    prev = lambda jcol: pl.BlockSpec((halo, GDN_DIM), lambda i: (jnp.maximum(i * (tm // halo) - 1, 0), jcol))
    act = jax.ShapeDtypeStruct((s, GDN_DIM), BF16)
    return pl.pallas_call(
        _gdn_prep_kernel,
        grid=(s // tm,),
        in_specs=[blk(1), blk(2), blk(3), prev(1), prev(2), prev(3), pl.BlockSpec((tm, LANES), lambda i: (i, 2)),
                  full(conv_w), full(alog), full(dtb)],
        out_specs=[blk(0), blk(0), blk(0), blk(0),
                   pl.BlockSpec((GDN_HEADS, tm, CHUNK), lambda i: (0, i, 0)),
                   pl.BlockSpec((tm, LANES), lambda i: (i, 0))],
        out_shape=[act, act, jax.ShapeDtypeStruct((s, GDN_DIM), F32), act,
                   jax.ShapeDtypeStruct((GDN_HEADS, s, CHUNK), BF16), jax.ShapeDtypeStruct((s, LANES), F32)],
        scratch_shapes=[pltpu.VMEM((tm + halo, 3 * GDN_DIM), F32)],
        compiler_params=_cparams("parallel"),
        name="gdn_prep",
    )(main, main, main, main, main, main, small, conv_w, alog, dtb)


def _gdn_scan_kernel(qd_ref, kd_ref, u_ref, w_ref, a_ref, gc_ref, z_ref, gn_ref, o_ref, state_ref):
    c, d, nh = CHUNK, GDN_HEAD_DIM, GDN_HEADS

    @pl.when(pl.program_id(0) == 0)
    def _():
        state_ref[...] = jnp.zeros(state_ref.shape, F32)

    def heads(ref, rows):
        return jnp.stack([ref[rows, h * d:(h + 1) * d] for h in range(nh)])

    for ci in range(qd_ref.shape[0] // c):
        rows = slice(ci * c, (ci + 1) * c)
        last = ci * c + c - 1
        g_last = jnp.stack([gc_ref[last:last + 1, nh + h:nh + h + 1] for h in range(nh)])
        st = state_ref[...]
        stb = st.astype(BF16)
        v_new = heads(u_ref, rows) - _bdot(heads(w_ref, rows), stb)
        v_newb = v_new.astype(BF16)
        o = _bdot(heads(qd_ref, rows), stb) + _bdot(a_ref[:, rows, :], v_newb)
        state_ref[...] = st * jnp.exp(g_last) + _bdot_tn(heads(kd_ref, rows), v_newb)
        o = _rms(o) * gn_ref[...]
        for h in range(nh):
            lanes = slice(h * d, (h + 1) * d)
            o_ref[rows, lanes] = (o[h] * _silu(z_ref[rows, lanes].astype(F32))).astype(o_ref.dtype)


def _gdn_scan(qd, kd, u, w, a, gc, main, gnorm, *, tm):
    s = qd.shape[0]
    blk = lambda jcol: pl.BlockSpec((tm, GDN_DIM), lambda i: (i, jcol))
    return pl.pallas_call(
        _gdn_scan_kernel,
        grid=(s // tm,),
        in_specs=[blk(0), blk(0), blk(0), blk(0),
                  pl.BlockSpec((GDN_HEADS, tm, CHUNK), lambda i: (0, i, 0)),
                  pl.BlockSpec((tm, LANES), lambda i: (i, 0)),
                  blk(4), pl.BlockSpec(gnorm.shape, lambda i: (0, 0))],
        out_specs=blk(0),
        out_shape=jax.ShapeDtypeStruct((s, GDN_DIM), BF16),
        scratch_shapes=[pltpu.VMEM((GDN_HEADS, GDN_HEAD_DIM, GDN_HEAD_DIM), F32)],
        compiler_params=_cparams("arbitrary"),
        name="gdn_scan",
    )(qd, kd, u, w, a, gc, main, gnorm)


def _gla_kernel(q_ref, k_ref, v_ref, r_ref, sm_ref, wg1_ref, wg2_ref, bg_ref, gn_ref, o_ref, state_ref):
    c = CHUNK
    dk = q_ref.shape[1] // GLA_HEADS
    dv = v_ref.shape[1] // GLA_HEADS

    @pl.when(pl.program_id(0) == 0)
    def _():
        state_ref[...] = jnp.zeros(state_ref.shape, F32)

    a1, a2, _ = _split3(sm_ref[...])
    w1, w2 = wg1_ref[...], wg2_ref[...]
    pre = _dot(a1, w1) + _dot(a1, w2) + _dot(a2, w1) + bg_ref[...]
    gk = (jnp.minimum(pre, 0.0) - jnp.log1p(jnp.exp(-jnp.abs(pre)))) * (1.0 / GLA_GATE_NORMALIZER)
    tm = q_ref.shape[0]
    trow = lax.broadcasted_iota(jnp.int32, (tm, tm), 0)
    tcol = lax.broadcasted_iota(jnp.int32, (tm, tm), 1)
    tri = jnp.where((tcol <= trow) & (trow // c == tcol // c), 1.0, 0.0).astype(BF16)
    b_tile = _cumsum_rows(tri, gk)
    row = lax.broadcasted_iota(jnp.int32, (c, c), 0)
    col = lax.broadcasted_iota(jnp.int32, (c, c), 1)
    sub_row = lax.broadcasted_iota(jnp.int32, (GLA_SUB, 1), 0)
    sub_col = lax.broadcasted_iota(jnp.int32, (GLA_SUB, GLA_SUB), 1)

    def chunk_factored(rows, b_all):
        heads = lambda ref, w: jnp.stack([ref[rows, h * w:(h + 1) * w] for h in range(GLA_HEADS)])
        bh = jnp.stack([b_all[:, h * dk:(h + 1) * dk] for h in range(GLA_HEADS)])
        q = heads(q_ref, dk).astype(F32) * (1.0 / math.sqrt(dk))
        k = heads(k_ref, dk).astype(F32)
        v = heads(v_ref, dv)
        b0, b_last = bh[:, 0:1], bh[:, c - 1:c]
        q_t = q * jnp.exp(bh - b0)
        k_t = k * jnp.exp(b0 - bh)
        a = jnp.where(col <= row, _bdot_nt(q_t.astype(BF16), k_t.astype(BF16)), 0.0)
        q_dec = q_t * jnp.exp(b0)
        k_dec = k_t * jnp.exp(b_last - b0)
        st = state_ref[...]
        o = _bdot_nt(q_dec.astype(BF16), st.astype(BF16)) + _bdot(a.astype(BF16), v)
        state_ref[...] = st * jnp.exp(b_last) + _bdot_tn(v, k_dec.astype(BF16))
        o = _rms(o) * gn_ref[...]
        for h in range(GLA_HEADS):
            lanes = slice(h * dv, (h + 1) * dv)
            o_ref[rows, lanes] = (o[h] * _silu(r_ref[rows, lanes].astype(F32))).astype(o_ref.dtype)

    def intra_exact(qh, kh, bh, vh):
        o_rows = []
        for blk in range(c // GLA_SUB):
            r0 = blk * GLA_SUB
            qs, bs = qh[r0:r0 + GLA_SUB], bh[r0:r0 + GLA_SUB]
            ks = kh[r0:r0 + GLA_SUB]
            a_diag = jnp.zeros((GLA_SUB, GLA_SUB), F32)
            for j in range(GLA_SUB):
                e = jnp.exp(jnp.where(sub_row >= j, bs - bs[j:j + 1], -jnp.inf))
                dj = jnp.sum(qs * ks[j:j + 1] * e, axis=-1, keepdims=True)
                a_diag = jnp.where(sub_col == j, dj, a_diag)
            o_blk = _dot(a_diag.astype(BF16), vh[r0:r0 + GLA_SUB])
            if blk > 0:
                b_ref0 = bs[0:1]
                q_t = qs * jnp.exp(bs - b_ref0)
                k_t = kh[0:r0] * jnp.exp(b_ref0 - bh[0:r0])
                a_off = _dot_nt(q_t.astype(BF16), k_t.astype(BF16))
                o_blk = o_blk + _dot(a_off.astype(BF16), vh[0:r0])
            o_rows.append(o_blk)
        return jnp.concatenate(o_rows, axis=0), qh * jnp.exp(bh), kh * jnp.exp(bh[c - 1:c] - bh)

    def chunk_exact(rows, b_all):
        for h in range(GLA_HEADS):
            bh = b_all[:, h * dk:(h + 1) * dk]
            qh = q_ref[rows, h * dk:(h + 1) * dk].astype(F32) * (1.0 / math.sqrt(dk))
            kh = k_ref[rows, h * dk:(h + 1) * dk].astype(F32)
            vh = v_ref[rows, h * dv:(h + 1) * dv]
            st = state_ref[h]
            o_intra, q_dec, k_dec = intra_exact(qh, kh, bh, vh)
            o = _dot_nt(q_dec.astype(BF16), st.astype(BF16)) + o_intra
            state_ref[h] = st * jnp.exp(bh[c - 1:c]) + _dot_tn(vh, k_dec.astype(BF16))
            o = _rms(o) * gn_ref[...] * _silu(r_ref[rows, h * dv:(h + 1) * dv].astype(F32))
            o_ref[rows, h * dv:(h + 1) * dv] = o.astype(o_ref.dtype)

    for ci in range(tm // c):
        rows = slice(ci * c, (ci + 1) * c)
        b_all = b_tile[rows]
        decay_range = jnp.max(b_all[0:1, :] - b_all[c - 1:c, :])
        factorable = decay_range <= GLA_FACTOR_RANGE
        pl.when(factorable)(functools.partial(chunk_factored, rows, b_all))
        pl.when(jnp.logical_not(factorable))(functools.partial(chunk_exact, rows, b_all))


def _gla(main, small, wg, bg, gnorm, *, key_dim, value_dim, tm):
    s = main.shape[0]
    full = lambda a: pl.BlockSpec(a.shape, lambda i: (0,) * a.ndim)
    wg1 = wg.astype(BF16)
    wg2 = (wg - wg1.astype(F32)).astype(BF16)
    return pl.pallas_call(
        _gla_kernel,
        grid=(s // tm,),
        in_specs=[
            pl.BlockSpec((tm, key_dim), lambda i: (i, 0)),
            pl.BlockSpec((tm, key_dim), lambda i: (i, 1)),
            pl.BlockSpec((tm, value_dim), lambda i: (i, 1)),
            pl.BlockSpec((tm, value_dim), lambda i: (i, 2)),
            pl.BlockSpec((tm, LANES), lambda i: (i, 0)),
            full(wg1), full(wg2), full(bg), full(gnorm),
        ],
        out_specs=pl.BlockSpec((tm, value_dim), lambda i: (i, 0)),
        out_shape=jax.ShapeDtypeStruct((s, value_dim), BF16),
        scratch_shapes=[pltpu.VMEM((GLA_HEADS, value_dim // GLA_HEADS, key_dim // GLA_HEADS), F32)],
        compiler_params=_cparams("arbitrary"),
        name="gated_linear_attention",
    )(main, main, main, main, small, wg1, wg2, bg, gnorm)


def _out_proj_kernel(a1_ref, a2_ref, w_ref, x_ref, gate_ref, o_ref):
    half = a1_ref.shape[1]
    mix = _dot(a1_ref[...], w_ref[0:half, :]) + _dot(a2_ref[...], w_ref[half:, :])
    o_ref[...] = x_ref[...] + gate_ref[...] * mix


def _out_proj(a1, a2, col1, col2, w, layer, x, gate, *, tm):
    s, d = x.shape
    half = w.shape[1] // 2
    return pl.pallas_call(
        _out_proj_kernel,
        grid=(s // tm,),
        in_specs=[
            pl.BlockSpec((tm, half), lambda i: (i, col1)),
            pl.BlockSpec((tm, half), lambda i: (i, col2)),
            pl.BlockSpec((None,) + w.shape[1:], lambda i: (layer, 0, 0)),
            pl.BlockSpec((tm, d), lambda i: (i, 0)),
            pl.BlockSpec((1, d), lambda i: (0, 0)),
        ],
        out_specs=pl.BlockSpec((tm, d), lambda i: (i, 0)),
        out_shape=jax.ShapeDtypeStruct((s, d), F32),
        compiler_params=_cparams("parallel"),
        name="out_proj_residual",
    )(a1, a2, w, x, gate)


def _ffn_kernel(x_ref, g_ref, sc_ref, sh_ref, gate_ref, fin_ref, w1_ref, w3_ref, w2_ref, o_ref, h_ref, acc_ref,
                *, final_norm):
    j = pl.program_id(1)

    @pl.when(j == 0)
    def _():
        h_ref[...] = _modulated_norm(x_ref[...], g_ref[...], sc_ref[...], sh_ref[...]).astype(BF16)
        acc_ref[...] = jnp.zeros(acc_ref.shape, F32)

    h = h_ref[...]
    a = _dot(h, w1_ref[...])
    t = (_silu(a) * _dot(h, w3_ref[...])).astype(BF16)
    acc_ref[...] += _dot(t, w2_ref[...])

    @pl.when(j == pl.num_programs(1) - 1)
    def _():
        y = x_ref[...] + gate_ref[...] * acc_ref[...]
        if final_norm:
            y = _rms(y) * fin_ref[...]
        o_ref[...] = y


def _ffn(x, g, sc, sh, gate, fin, w1, w3, w2, layer, *, tm, tf, final_norm):
    s, d = x.shape
    dff = w1.shape[2]
    vec = pl.BlockSpec((1, d), lambda i, j: (0, 0))
    return pl.pallas_call(
        functools.partial(_ffn_kernel, final_norm=final_norm),
        grid=(s // tm, dff // tf),
        in_specs=[
            pl.BlockSpec((tm, d), lambda i, j: (i, 0)),
            vec, vec, vec, vec, vec,
            pl.BlockSpec((None, d, tf), lambda i, j: (layer, 0, j)),
            pl.BlockSpec((None, d, tf), lambda i, j: (layer, 0, j)),
            pl.BlockSpec((None, tf, d), lambda i, j: (layer, j, 0)),
        ],
        out_specs=pl.BlockSpec((tm, d), lambda i, j: (i, 0)),
        out_shape=jax.ShapeDtypeStruct((s, d), F32),
        scratch_shapes=[pltpu.VMEM((tm, d), BF16), pltpu.VMEM((tm, d), F32)],
        compiler_params=_cparams("parallel", "arbitrary"),
        name="swiglu_ffn",
    )(x, g, sc, sh, gate, fin, w1, w3, w2)


def _rot_cols(w):
    half = w.shape[-1] // 2
    return jnp.concatenate([-w[..., half:], w[..., :half]], axis=-1)


def _pad_cols(w, n):
    return jnp.pad(w, [(0, 0)] * (w.ndim - 1) + [(0, n - w.shape[-1])])


def _pad_lanes(p):
    return jnp.concatenate([p, jnp.zeros((p.shape[0], LANES - p.shape[1]), p.dtype)], axis=-1)


def _pad_rows(p):
    return jnp.concatenate([p, jnp.zeros((LANES - p.shape[0], p.shape[1]), p.dtype)], axis=0)


def _even_in_layout_kernel(w_ref, main_ref, small_ref):
    o_kr = MLA_Q_RANK + MLA_KV_RANK
    o_qkv = o_kr + MLA_ROPE
    o_ba = o_qkv + 4 * GDN_DIM
    half = MLA_ROPE // 2
    w = w_ref[...]
    main_ref[:o_kr, :] = w[:o_kr].astype(BF16)
    main_ref[o_kr:, :] = w[o_qkv:o_ba].astype(BF16)
    kr = w[o_kr:o_qkv]
    rot = jnp.concatenate([-kr[half:], kr[:half]], axis=0)
    small_ref[...] = jnp.concatenate([_pad_rows(kr), _pad_rows(rot), _pad_rows(w[o_ba:])], axis=0).astype(BF16)


def _gla_in_layout_kernel(w_ref, main_ref, small_ref):
    n = main_ref.shape[0]
    w = w_ref[...]
    main_ref[...] = w[:n].astype(BF16)
    small_ref[...] = _pad_rows(w[n:]).astype(BF16)


def _in_weight_layout(body, w_in, n_main, n_small, *, tc=256):
    w_t = jnp.swapaxes(w_in, 1, 2)
    nl, n, d = w_t.shape
    return pl.pallas_call(
        body,
        grid=(nl, d // tc),
        in_specs=[pl.BlockSpec((None, n, tc), lambda l, i: (l, 0, i))],
        out_specs=[pl.BlockSpec((None, n_main, tc), lambda l, i: (l, 0, i)),
                   pl.BlockSpec((None, n_small, tc), lambda l, i: (l, 0, i))],
        out_shape=[jax.ShapeDtypeStruct((nl, n_main, d), BF16), jax.ShapeDtypeStruct((nl, n_small, d), BF16)],
        compiler_params=_cparams("parallel", "parallel"),
        name="in_weight_layout",
    )(w_t)


def _mla_weights(w_uq, w_ukv):
    r = w_uq.shape[0]
    wq = w_uq.reshape(r, MLA_HEADS, MLA_QK)
    nope, rope = wq[..., :MLA_NOPE], wq[..., MLA_NOPE:]
    flat = lambda t: t.reshape(r, MLA_HEADS * LANES)
    wq_all = jnp.concatenate([flat(nope), flat(_pad_cols(rope, LANES)), flat(_pad_cols(_rot_cols(rope), LANES))], axis=1)
    wkv = w_ukv.reshape(w_ukv.shape[0], MLA_HEADS, MLA_NOPE + MLA_V)
    wkv_all = jnp.concatenate([flat(wkv[..., :MLA_NOPE]), flat(wkv[..., MLA_NOPE:])], axis=1)
    return wq_all.astype(BF16), wkv_all.astype(BF16)


def kernel(x, c, positions, norm_g, ada_w, ada_b, ab_w_in, mla_q_norm, mla_w_uq, mla_kv_norm, mla_w_ukv, gdn_conv_w,
           gdn_a_log, gdn_dt_bias, gdn_norm, ab_w_out, gla_w_in, gla_w_gk2, gla_b_gk2, gla_norm, gla_w_out, ffn_w1,
           ffn_w3, ffn_w2, final_norm):
    batch, s, d = x.shape
    assert batch == 1 and c.shape == (1, d)
    depth = norm_g.shape[0]
    key_dim = gla_w_gk2.shape[2]
    value_dim = gla_w_out.shape[1]

    ada_w2, ada_b2 = ada_w.reshape(depth * 2, d, 3 * d), ada_b.reshape(depth * 2, 1, 3 * d)
    mods0, silu_c = _ada_modulation(c.reshape(d, 1), ada_w2, ada_b2, 1)
    mods = {0: mods0[0]}
    xs = x.reshape(s, d)
    pos_col = positions.reshape(s, 1)
    half = MLA_ROPE // 2
    inv_freq = ROPE_THETA ** (-jnp.arange(half, dtype=F32) / half)
    invf = jnp.tile(inv_freq, LANES // half).reshape(1, LANES)
    row = lambda v: v.reshape(1, -1)

    n_gla = 2 * key_dim + 2 * value_dim
    n_even = MLA_Q_RANK + MLA_KV_RANK + 4 * GDN_DIM
    ab_main_w, ab_small_w = _in_weight_layout(_even_in_layout_kernel, ab_w_in, n_even, 3 * LANES)
    gla_main_w, gla_small_w = _in_weight_layout(_gla_in_layout_kernel, gla_w_in, n_gla, LANES)
    ab_out_w, gla_out_w = ab_w_out.astype(BF16), gla_w_out.astype(BF16)
    assert depth % 2 == 0

    for layer in range(depth):
        i = layer // 2
        m = mods[2 * layer]
        shift, scale, gate = m[:, :d], m[:, d:2 * d], m[:, 2 * d:]
        gain = row(norm_g[layer, 0])
        if layer % 2 == 0:
            main, small = _norm_proj(xs, gain, scale, shift, ab_main_w, ab_small_w, i, tm=1024, tn=1024)
            wq, wkv = _mla_weights(mla_w_uq[i], mla_w_ukv[i])
            q, k, v = _mla_up(main, small, pos_col, invf, row(mla_q_norm[i]), row(mla_kv_norm[i]), wq, wkv, tm=256)
            first = 2 * layer + 1
            o_a, w1, w3, w2, later = _causal_attention(q, k, v, (ffn_w1, ffn_w3, ffn_w2), layer, 2,
                                                       (silu_c, ada_w2, ada_b2, first, 4), t=1024)
            mods.update({first + n: later[n] for n in range(4)})
            lane_pad = lambda vec: jnp.pad(vec, (GDN_HEADS, LANES - 2 * GDN_HEADS)).reshape(1, LANES)
            qd, kd, u, w, a, gc = _gdn_prep(main, small, gdn_conv_w[i], lane_pad(gdn_a_log[i]),
                                            lane_pad(gdn_dt_bias[i]), tm=256)
            o_b = _gdn_scan(qd, kd, u, w, a, gc, main, row(gdn_norm[i]), tm=512)
            xs = _out_proj(o_a, o_b, 0, 0, ab_out_w, i, xs, gate, tm=512)
        else:
            main, small = _norm_proj(xs, gain, scale, shift, gla_main_w, gla_small_w, i, tm=1024, tn=1024)
            wg = jnp.pad(gla_w_gk2[i], ((0, LANES - GLA_GATE_RANK), (0, 0)))
            o_c = _gla(main, small, wg, row(gla_b_gk2[i]), row(gla_norm[i]), key_dim=key_dim, value_dim=value_dim,
                       tm=256)
            xs = _out_proj(o_c, o_c, 0, 1, gla_out_w, i, xs, gate, tm=512)
        m = mods[2 * layer + 1]
        shift, scale, gate = m[:, :d], m[:, d:2 * d], m[:, 2 * d:]
        xs = _ffn(xs, row(norm_g[layer, 1]), scale, shift, gate, row(final_norm), w1, w3, w2, layer % 2, tm=512,
                  tf=512,
                  final_norm=(layer == depth - 1))
    return xs.reshape(batch, s, d)
```

```python
import functools
import math

import jax
import jax.numpy as jnp
from jax import lax
from jax.experimental import pallas as pl
from jax.experimental.pallas import tpu as pltpu

F32 = jnp.float32
BF16 = jnp.bfloat16

EPS = 1e-6
ROPE_THETA = 10000.0
MLA_HEADS = 8
MLA_NOPE = 128
MLA_ROPE = 64
MLA_V = 128
MLA_QK = MLA_NOPE + MLA_ROPE
MLA_QK_PAD = 256
MLA_V_PAD = MLA_V + 16
MLA_Q_RANK = 512
MLA_KV_RANK = 512
GDN_HEADS = 8
GDN_HEAD_DIM = 128
GDN_DIM = GDN_HEADS * GDN_HEAD_DIM
GDN_CONV = 4
GLA_HEADS = 4
GLA_GATE_RANK = 16
GLA_GATE_NORMALIZER = 16.0
GLA_SUB = 16
GLA_FACTOR_RANGE = 60.0
CHUNK = 64
LANES = 128
VMEM_LIMIT = 48 * 1024 * 1024

TM_PROJ, TN_PROJ = 1024, 1024
TM_MLA_UP = 256
T_ATTN = 1024
TM_GDN_PREP = 256
TM_GDN_SCAN = 512
TM_GLA = 256
TM_OUT = 512
TM_FFN, TF_FFN = 512, 512


def _cparams(*sem):
    return pltpu.CompilerParams(dimension_semantics=sem, vmem_limit_bytes=VMEM_LIMIT)


def _dot(a, b):
    return jnp.dot(a, b, preferred_element_type=F32)


def _dot_nt(a, b):
    return lax.dot_general(a, b, (((1,), (1,)), ((), ())), preferred_element_type=F32)


def _dot_tn(a, b):
    return lax.dot_general(a, b, (((0,), (0,)), ((), ())), preferred_element_type=F32)


def _split3(x):
    x1 = x.astype(BF16)
    r1 = x - x1.astype(F32)
    x2 = r1.astype(BF16)
    x3 = (r1 - x2.astype(F32)).astype(BF16)
    return x1, x2, x3


def _cumsum_rows(tri, x):
    x1, x2, x3 = _split3(x)
    return _dot(tri, x1) + _dot(tri, x2) + _dot(tri, x3)


def _silu(x):
    return x * jax.nn.sigmoid(x)


def _softplus(x):
    return jnp.maximum(x, 0.0) + jnp.log1p(jnp.exp(-jnp.abs(x)))


def _rms(x):
    return x * lax.rsqrt(jnp.mean(x * x, axis=-1, keepdims=True) + EPS)


def _ada_kernel(c_ref, w_ref, b_ref, o_ref, s_ref):
    s = _silu(c_ref[...])
    o_ref[0] = jnp.sum(w_ref[0] * s, axis=0, keepdims=True) + b_ref[0]
    s_ref[...] = jnp.broadcast_to(s, s_ref.shape)


def _ada_modulation(c_col, w, b, n_mod, *, tn=768):
    _, d, n = w.shape
    return pl.pallas_call(
        _ada_kernel,
        grid=(n_mod, n // tn),
        in_specs=[
            pl.BlockSpec((d, 1), lambda l, j: (0, 0)),
            pl.BlockSpec((1, d, tn), lambda l, j: (l, 0, j)),
            pl.BlockSpec((1, 1, tn), lambda l, j: (l, 0, j)),
        ],
        out_specs=[pl.BlockSpec((1, 1, tn), lambda l, j: (l, 0, j)), pl.BlockSpec((d, LANES), lambda l, j: (0, 0))],
        out_shape=[jax.ShapeDtypeStruct((n_mod, 1, n), F32), jax.ShapeDtypeStruct((d, LANES), F32)],
        compiler_params=_cparams("arbitrary", "arbitrary"),
        name="ada_modulation",
    )(c_col, w, b)


def _modulated_norm(x, g, sc, sh):
    return _rms(x) * (g * (1.0 + sc)) + sh


def _norm_proj_kernel(x_ref, g_ref, sc_ref, sh_ref, w_ref, ws_ref, o_ref, os_ref, h_ref):
    @pl.when(pl.program_id(1) == 0)
    def _():
        h = _modulated_norm(x_ref[...], g_ref[...], sc_ref[...], sh_ref[...]).astype(BF16)
        h_ref[...] = h
        os_ref[...] = _dot_nt(h, ws_ref[...])

    o_ref[...] = _dot_nt(h_ref[...], w_ref[...]).astype(o_ref.dtype)


def _norm_proj(x, g, sc, sh, w, ws, layer, *, tm, tn):
    s, d = x.shape
    n, ns = w.shape[1], ws.shape[1]
    assert s % tm == 0 and n % tn == 0
    vec = pl.BlockSpec((1, d), lambda i, j: (0, 0))
    return pl.pallas_call(
        _norm_proj_kernel,
        grid=(s // tm, n // tn),
        in_specs=[
            pl.BlockSpec((tm, d), lambda i, j: (i, 0)),
            vec, vec, vec,
            pl.BlockSpec((None, tn, d), lambda i, j: (layer, j, 0)),
            pl.BlockSpec((None, ns, d), lambda i, j: (layer, 0, 0)),
        ],
        out_specs=[
            pl.BlockSpec((tm, tn), lambda i, j: (i, j)),
            pl.BlockSpec((tm, ns), lambda i, j: (i, 0)),
        ],
        out_shape=[jax.ShapeDtypeStruct((s, n), BF16), jax.ShapeDtypeStruct((s, ns), F32)],
        scratch_shapes=[pltpu.VMEM((tm, d), BF16)],
        compiler_params=_cparams("parallel", "arbitrary"),
        name="norm_proj",
    )(x, g, sc, sh, w, ws)


def _mla_up_kernel(cq_ref, ckv_ref, kr_ref, krr_ref, pos_ref, invf_ref, qn_ref, kvn_ref, wq_ref, wkv_ref,
                   q_out, k_out, v_out):
    hd = MLA_HEADS * LANES
    cqn = (_rms(cq_ref[...].astype(F32)) * qn_ref[...]).astype(BF16)
    ckvn = (_rms(ckv_ref[...].astype(F32)) * kvn_ref[...]).astype(BF16)
    qall = _dot(cqn, wq_ref[...])
    kv = _dot(ckvn, wkv_ref[...])
    ang = pos_ref[...].astype(F32) * invf_ref[...]
    cos, sin = jnp.cos(ang), jnp.sin(ang)
    k_rope = kr_ref[...] * cos + krr_ref[...] * sin
    scale = math.log2(math.e) / math.sqrt(MLA_QK)
    ones = jnp.ones((MLA_V_PAD - MLA_V, cq_ref.shape[0]), F32)
    for h in range(MLA_HEADS):
        lo, hi = h * LANES, (h + 1) * LANES
        q_rope = qall[:, hd + lo:hd + hi] * cos + qall[:, 2 * hd + lo:2 * hd + hi] * sin
        q_out[h] = (jnp.concatenate([qall[:, lo:hi], q_rope], axis=-1) * scale).astype(BF16)
        k_out[h] = jnp.concatenate([kv[:, lo:hi], k_rope], axis=-1).astype(BF16)
        v_out[h] = jnp.concatenate([kv[:, hd + lo:hd + hi].T, ones], axis=0).astype(BF16)


def _mla_up(main, small, pos_col, invf, qn, kvn, wq, wkv, *, tm):
    s = main.shape[0]
    full = lambda a: pl.BlockSpec(a.shape, lambda i: (0,) * a.ndim)
    qk_shape = jax.ShapeDtypeStruct((MLA_HEADS, s, MLA_QK_PAD), BF16)
    return pl.pallas_call(
        _mla_up_kernel,
        grid=(s // tm,),
        in_specs=[
            pl.BlockSpec((tm, MLA_Q_RANK), lambda i: (i, 0)),
            pl.BlockSpec((tm, MLA_KV_RANK), lambda i: (i, 1)),
            pl.BlockSpec((tm, LANES), lambda i: (i, 0)),
            pl.BlockSpec((tm, LANES), lambda i: (i, 1)),
            pl.BlockSpec((tm, 1), lambda i: (i, 0)),
            full(invf), full(qn), full(kvn), full(wq), full(wkv),
        ],
        out_specs=[
            pl.BlockSpec((MLA_HEADS, tm, MLA_QK_PAD), lambda i: (0, i, 0)),
            pl.BlockSpec((MLA_HEADS, tm, MLA_QK_PAD), lambda i: (0, i, 0)),
            pl.BlockSpec((MLA_HEADS, MLA_V_PAD, tm), lambda i: (0, 0, i)),
        ],
        out_shape=[qk_shape, qk_shape, jax.ShapeDtypeStruct((MLA_HEADS, MLA_V_PAD, s), BF16)],
        compiler_params=_cparams("parallel"),
        name="mla_up",
    )(main, main, small, small, pos_col, invf, qn, kvn, wq, wkv)


def _attn_kernel(q_ref, k_ref, v_ref, *rest, t, n_side):
    side_in, (sc_ref, aw_ref, ab_ref) = rest[:n_side], rest[n_side:n_side + 3]
    (o_ref, *side_out), mods_ref = rest[n_side + 3:2 * n_side + 4], rest[2 * n_side + 4]
    s0_ref, s1_ref, m_ref, acc_ref = rest[2 * n_side + 5:]
    for src, dst in zip(side_in, side_out):
        dst[...] = src[...].astype(dst.dtype)
    silu_c = jnp.concatenate([sc_ref[...]] * (aw_ref.shape[1] // LANES), axis=1)
    mods_ref[...] = jnp.sum(aw_ref[...] * silu_c, axis=0, keepdims=True) + ab_ref[...]
    i = pl.program_id(1)
    q = q_ref[0]
    dv = MLA_V

    def tile(j):
        return pl.ds(pl.multiple_of(j * t, t), t)

    def scores(s_ref, j):
        s_ref[...] = _dot_nt(k_ref[0, tile(j), :], q)

    def update(s_ref, j, diagonal):
        s = s_ref[...]
        if diagonal:
            key = lax.broadcasted_iota(jnp.int32, (t, t), 0)
            qry = lax.broadcasted_iota(jnp.int32, (t, t), 1)
            s = jnp.where(key <= qry, s, -jnp.inf)
        m = m_ref[...]
        m_new = jnp.maximum(m, jnp.max(s, axis=0, keepdims=True))
        p = jnp.exp2(s - m_new)
        alpha = jnp.exp2(m - m_new)
        acc_ref[...] = alpha * acc_ref[...] + _dot(v_ref[0, :, tile(j)], p.astype(BF16))
        m_ref[...] = m_new

    m_ref[...] = jnp.full(m_ref.shape, -jnp.inf, F32)
    acc_ref[...] = jnp.zeros(acc_ref.shape, F32)
    scores(s0_ref, 0)

    def pair(jj, carry):
        j = 2 * jj
        scores(s1_ref, j + 1)
        update(s0_ref, j, False)
        scores(s0_ref, j + 2)
        update(s1_ref, j + 1, False)
        return carry

    lax.fori_loop(0, i // 2, pair, 0)

    @pl.when(i % 2 == 0)
    def _():
        update(s0_ref, i, True)

    @pl.when(i % 2 == 1)
    def _():
        scores(s1_ref, i)
        update(s0_ref, i - 1, False)
        update(s1_ref, i, True)

    acc = acc_ref[...]
    o_ref[...] = (acc[:dv] / acc[dv:dv + 1]).T.astype(o_ref.dtype)


def _causal_attention(q, k, v, side, layer0, n_layers, ada, *, t):
    h, s, dqk = q.shape
    dvp = v.shape[1]
    nq = s // t
    steps = h * nq
    assert s % t == 0 and steps % n_layers == 0
    step = lambda hh, i: hh * nq + i
    per_layer = steps // n_layers
    side_in, side_out, side_shapes = [], [], []
    for w in side:
        rows = w.shape[1] // per_layer
        assert rows * per_layer == w.shape[1] and rows % 16 == 0
        blk = (None, rows) + w.shape[2:]
        side_in.append(pl.BlockSpec(blk, lambda hh, i: (layer0 + step(hh, i) // per_layer, step(hh, i) % per_layer, 0)))
        side_out.append(pl.BlockSpec(blk, lambda hh, i: (step(hh, i) // per_layer, step(hh, i) % per_layer, 0)))
        side_shapes.append(jax.ShapeDtypeStruct((n_layers,) + w.shape[1:], BF16))
    silu_c, ada_w, ada_b, first, count = ada
    n_mod, d, n3 = ada_w.shape
    per_mod = steps // count
    slab = n3 // per_mod
    assert per_mod * count == steps and slab * per_mod == n3 and slab % LANES == 0
    mod_idx = lambda hh, i: jnp.minimum(first + step(hh, i) // per_mod, n_mod - 1)
    ada_in = [
        pl.BlockSpec((d, LANES), lambda hh, i: (0, 0)),
        pl.BlockSpec((None, d, slab), lambda hh, i: (mod_idx(hh, i), 0, step(hh, i) % per_mod)),
        pl.BlockSpec((None, 1, slab), lambda hh, i: (mod_idx(hh, i), 0, step(hh, i) % per_mod)),
    ]
    mods_out = pl.BlockSpec((None, 1, slab), lambda hh, i: (step(hh, i) // per_mod, 0, step(hh, i) % per_mod))
    return pl.pallas_call(
        functools.partial(_attn_kernel, t=t, n_side=len(side)),
        grid=(h, nq),
        in_specs=[
            pl.BlockSpec((1, t, dqk), lambda hh, i: (hh, i, 0)),
            pl.BlockSpec((1, s, dqk), lambda hh, i: (hh, 0, 0)),
            pl.BlockSpec((1, dvp, s), lambda hh, i: (hh, 0, 0)),
        ] + side_in + ada_in,
        out_specs=[pl.BlockSpec((t, MLA_V), lambda hh, i: (i, hh))] + side_out + [mods_out],
        out_shape=[jax.ShapeDtypeStruct((s, h * MLA_V), BF16)] + side_shapes
        + [jax.ShapeDtypeStruct((count, 1, n3), F32)],
        scratch_shapes=[pltpu.VMEM((t, t), F32), pltpu.VMEM((t, t), F32), pltpu.VMEM((1, t), F32),
                        pltpu.VMEM((dvp, t), F32)],
        compiler_params=_cparams("arbitrary", "arbitrary"),
        name="mla_attention",
    )(q, k, v, *side, silu_c, ada_w, ada_b)


def _bdot(a, b):
    return lax.dot_general(a, b, (((2,), (1,)), ((0,), (0,))), preferred_element_type=F32)


def _bdot_nt(a, b):
    return lax.dot_general(a, b, (((2,), (2,)), ((0,), (0,))), preferred_element_type=F32)


def _bdot_tn(a, b):
    return lax.dot_general(a, b, (((1,), (1,)), ((0,), (0,))), preferred_element_type=F32)


def _unit_lower_inverse(low, eye, blk=16):
    n = low.shape[-1]
    bdot = lambda a, b: _bdot(a.astype(BF16), b.astype(BF16))
    rb = lax.broadcasted_iota(jnp.int32, (n, n), 0) // blk
    cb = lax.broadcasted_iota(jnp.int32, (n, n), 1) // blk
    diag = jnp.where(rb == cb, low, 0.0)
    p = diag
    d_inv = eye - diag
    for _ in range(int(math.log2(blk)) - 1):
        p = bdot(p, p)
        d_inv = d_inv + bdot(d_inv, p)
    m = bdot(d_inv, low - diag)
    assert n // blk == 4
    i_m = eye - m
    return bdot(i_m + bdot(i_m, bdot(m, m)), d_inv)


def _gdn_prep_kernel(q_ref, k_ref, v_ref, qh_ref, kh_ref, vh_ref, sm_ref, cw_ref, alog_ref, dtb_ref,
                     qd_out, kd_out, u_out, w_out, a_out, gc_out, hist_ref):
    c, d, nh = CHUNK, GDN_HEAD_DIM, GDN_HEADS
    tm = q_ref.shape[0]
    nc = tm // c
    halo = qh_ref.shape[0]

    first = pl.program_id(0) == 0
    for n, (cur, prev) in enumerate(((q_ref, qh_ref), (k_ref, kh_ref), (v_ref, vh_ref))):
        lanes = slice(n * GDN_DIM, (n + 1) * GDN_DIM)
        hist_ref[pl.ds(0, halo), lanes] = jnp.where(first, 0.0, prev[...].astype(F32))
        hist_ref[pl.ds(halo, tm), lanes] = cur[...].astype(F32)
    base = halo - (GDN_CONV - 1)
    y = hist_ref[pl.ds(base, tm), :] * cw_ref[0:1, :]
    for j in range(1, GDN_CONV):
        y = y + hist_ref[pl.ds(base + j, tm), :] * cw_ref[j:j + 1, :]
    y = _silu(y)

    sm = sm_ref[...]
    beta_all = jax.nn.sigmoid(sm)
    g_all = -jnp.exp(alog_ref[...]) * _softplus(sm + dtb_ref[...])
    row = lax.broadcasted_iota(jnp.int32, (tm, tm), 0)
    col = lax.broadcasted_iota(jnp.int32, (tm, tm), 1)
    tri = jnp.where((col <= row) & (row // c == col // c), 1.0, 0.0).astype(BF16)
    gc_all = _cumsum_rows(tri, g_all)
    gc_out[...] = gc_all
    zpad = jnp.zeros((LANES - c, LANES), F32)
    gc_t = [jnp.concatenate([gc_all[ci * c:(ci + 1) * c], zpad], axis=0).T for ci in range(nc)]

    def heads(arr, off):
        return jnp.stack([arr[ci * c:(ci + 1) * c, off + h * d:off + (h + 1) * d]
                          for ci in range(nc) for h in range(nh)])

    def cols(arr, off):
        return jnp.stack([arr[ci * c:(ci + 1) * c, off + h:off + h + 1] for ci in range(nc) for h in range(nh)])

    q = heads(y, 0)
    k = heads(y, GDN_DIM)
    v = heads(y, 2 * GDN_DIM)
    q = q * lax.rsqrt(jnp.sum(q * q, axis=-1, keepdims=True) + EPS) * (1.0 / math.sqrt(d))
    k = k * lax.rsqrt(jnp.sum(k * k, axis=-1, keepdims=True) + EPS)
    beta = cols(beta_all, 0)
    gc = cols(gc_all, nh)
    gcr = jnp.stack([gc_t[ci][nh + h:nh + h + 1, 0:c] for ci in range(nc) for h in range(nh)])
    r2 = lax.broadcasted_iota(jnp.int32, (c, c), 0)
    c2 = lax.broadcasted_iota(jnp.int32, (c, c), 1)
    decay = jnp.exp(jnp.where(c2 <= r2, gc - gcr, -jnp.inf))
    eye = jnp.where(c2 == r2, 1.0, 0.0).astype(F32)
    kbeta = k * beta
    kb16 = k.astype(BF16)
    low = jnp.where(c2 < r2, _bdot_nt(kbeta.astype(BF16), kb16) * decay, 0.0)
    t_inv = _unit_lower_inverse(low, eye)
    egc = jnp.exp(gc)
    rhs = jnp.concatenate([v * beta, kbeta * egc], axis=-1)
    sol = _bdot(t_inv.astype(BF16), rhs.astype(BF16))
    intra = _bdot_nt(q.astype(BF16), kb16) * decay
    g_last = gcr[:, :, c - 1:c]
    k_dec = k * jnp.exp(g_last - gc)
    q_dec = q * egc
    for ci in range(nc):
        rows = slice(ci * c, (ci + 1) * c)
        for h in range(nh):
            b = ci * nh + h
            lanes = slice(h * d, (h + 1) * d)
            qd_out[rows, lanes] = q_dec[b].astype(BF16)
            kd_out[rows, lanes] = k_dec[b].astype(BF16)
            u_out[rows, lanes] = sol[b, :, :d]
            w_out[rows, lanes] = sol[b, :, d:].astype(BF16)
            a_out[h, rows, :] = intra[b].astype(BF16)


def _gdn_prep(main, small, conv_w, alog, dtb, *, tm, halo=8):
    s = main.shape[0]
    full = lambda a: pl.BlockSpec(a.shape, lambda i: (0,) * a.ndim)
    blk = lambda jcol: pl.BlockSpec((tm, GDN_DIM), lambda i: (i, jcol))
    prev = lambda jcol: pl.BlockSpec((halo, GDN_DIM), lambda i: (jnp.maximum(i * (tm // halo) - 1, 0), jcol))
    act = jax.ShapeDtypeStruct((s, GDN_DIM), BF16)
    return pl.pallas_call(
        _gdn_prep_kernel,
        grid=(s // tm,),
        in_specs=[blk(1), blk(2), blk(3), prev(1), prev(2), prev(3), pl.BlockSpec((tm, LANES), lambda i: (i, 2)),
                  full(conv_w), full(alog), full(dtb)],
        out_specs=[blk(0), blk(0), blk(0), blk(0),
                   pl.BlockSpec((GDN_HEADS, tm, CHUNK), lambda i: (0, i, 0)),
                   pl.BlockSpec((tm, LANES), lambda i: (i, 0))],
        out_shape=[act, act, jax.ShapeDtypeStruct((s, GDN_DIM), F32), act,
                   jax.ShapeDtypeStruct((GDN_HEADS, s, CHUNK), BF16), jax.ShapeDtypeStruct((s, LANES), F32)],
        scratch_shapes=[pltpu.VMEM((tm + halo, 3 * GDN_DIM), F32)],
        compiler_params=_cparams("parallel"),
        name="gdn_prep",
    )(main, main, main, main, main, main, small, conv_w, alog, dtb)


def _gdn_scan_kernel(qd_ref, kd_ref, u_ref, w_ref, a_ref, gc_ref, z_ref, gn_ref, o_ref, state_ref):
    c, d, nh = CHUNK, GDN_HEAD_DIM, GDN_HEADS

    @pl.when(pl.program_id(0) == 0)
    def _():
        state_ref[...] = jnp.zeros(state_ref.shape, F32)

    def heads(ref, rows):
        return jnp.stack([ref[rows, h * d:(h + 1) * d] for h in range(nh)])

    for ci in range(qd_ref.shape[0] // c):
        rows = slice(ci * c, (ci + 1) * c)
        last = ci * c + c - 1
        g_last = jnp.stack([gc_ref[last:last + 1, nh + h:nh + h + 1] for h in range(nh)])
        st = state_ref[...]
        stb = st.astype(BF16)
        v_new = heads(u_ref, rows) - _bdot(heads(w_ref, rows), stb)
        v_newb = v_new.astype(BF16)
        o = _bdot(heads(qd_ref, rows), stb) + _bdot(a_ref[:, rows, :], v_newb)
        state_ref[...] = st * jnp.exp(g_last) + _bdot_tn(heads(kd_ref, rows), v_newb)
        o = _rms(o) * gn_ref[...]
        for h in range(nh):
            lanes = slice(h * d, (h + 1) * d)
            o_ref[rows, lanes] = (o[h] * _silu(z_ref[rows, lanes].astype(F32))).astype(o_ref.dtype)


def _gdn_scan(qd, kd, u, w, a, gc, main, gnorm, *, tm):
    s = qd.shape[0]
    blk = lambda jcol: pl.BlockSpec((tm, GDN_DIM), lambda i: (i, jcol))
    return pl.pallas_call(
        _gdn_scan_kernel,
        grid=(s // tm,),
        in_specs=[blk(0), blk(0), blk(0), blk(0),
                  pl.BlockSpec((GDN_HEADS, tm, CHUNK), lambda i: (0, i, 0)),
                  pl.BlockSpec((tm, LANES), lambda i: (i, 0)),
                  blk(4), pl.BlockSpec(gnorm.shape, lambda i: (0, 0))],
        out_specs=blk(0),
        out_shape=jax.ShapeDtypeStruct((s, GDN_DIM), BF16),
        scratch_shapes=[pltpu.VMEM((GDN_HEADS, GDN_HEAD_DIM, GDN_HEAD_DIM), F32)],
        compiler_params=_cparams("arbitrary"),
        name="gdn_scan",
    )(qd, kd, u, w, a, gc, main, gnorm)


def _gla_kernel(q_ref, k_ref, v_ref, r_ref, sm_ref, wg1_ref, wg2_ref, bg_ref, gn_ref, o_ref, state_ref):
    c = CHUNK
    dk = q_ref.shape[1] // GLA_HEADS
    dv = v_ref.shape[1] // GLA_HEADS

    @pl.when(pl.program_id(0) == 0)
    def _():
        state_ref[...] = jnp.zeros(state_ref.shape, F32)

    a1, a2, _ = _split3(sm_ref[...])
    w1, w2 = wg1_ref[...], wg2_ref[...]
    pre = _dot(a1, w1) + _dot(a1, w2) + _dot(a2, w1) + bg_ref[...]
    gk = (jnp.minimum(pre, 0.0) - jnp.log1p(jnp.exp(-jnp.abs(pre)))) * (1.0 / GLA_GATE_NORMALIZER)
    tm = q_ref.shape[0]
    trow = lax.broadcasted_iota(jnp.int32, (tm, tm), 0)
    tcol = lax.broadcasted_iota(jnp.int32, (tm, tm), 1)
    tri = jnp.where((tcol <= trow) & (trow // c == tcol // c), 1.0, 0.0).astype(BF16)
    b_tile = _cumsum_rows(tri, gk)
    row = lax.broadcasted_iota(jnp.int32, (c, c), 0)
    col = lax.broadcasted_iota(jnp.int32, (c, c), 1)
    sub_row = lax.broadcasted_iota(jnp.int32, (GLA_SUB, 1), 0)
    sub_col = lax.broadcasted_iota(jnp.int32, (GLA_SUB, GLA_SUB), 1)

    def chunk_factored(rows, b_all):
        heads = lambda ref, w: jnp.stack([ref[rows, h * w:(h + 1) * w] for h in range(GLA_HEADS)])
        bh = jnp.stack([b_all[:, h * dk:(h + 1) * dk] for h in range(GLA_HEADS)])
        q = heads(q_ref, dk).astype(F32) * (1.0 / math.sqrt(dk))
        k = heads(k_ref, dk).astype(F32)
        v = heads(v_ref, dv)
        b0, b_last = bh[:, 0:1], bh[:, c - 1:c]
        q_t = q * jnp.exp(bh - b0)
        k_t = k * jnp.exp(b0 - bh)
        a = jnp.where(col <= row, _bdot_nt(q_t.astype(BF16), k_t.astype(BF16)), 0.0)
        q_dec = q_t * jnp.exp(b0)
        k_dec = k_t * jnp.exp(b_last - b0)
        st = state_ref[...]
        o = _bdot_nt(q_dec.astype(BF16), st.astype(BF16)) + _bdot(a.astype(BF16), v)
        state_ref[...] = st * jnp.exp(b_last) + _bdot_tn(v, k_dec.astype(BF16))
        o = _rms(o) * gn_ref[...]
        for h in range(GLA_HEADS):
            lanes = slice(h * dv, (h + 1) * dv)
            o_ref[rows, lanes] = (o[h] * _silu(r_ref[rows, lanes].astype(F32))).astype(o_ref.dtype)

    def intra_exact(qh, kh, bh, vh):
        o_rows = []
        for blk in range(c // GLA_SUB):
            r0 = blk * GLA_SUB
            qs, bs = qh[r0:r0 + GLA_SUB], bh[r0:r0 + GLA_SUB]
            ks = kh[r0:r0 + GLA_SUB]
            a_diag = jnp.zeros((GLA_SUB, GLA_SUB), F32)
            for j in range(GLA_SUB):
                e = jnp.exp(jnp.where(sub_row >= j, bs - bs[j:j + 1], -jnp.inf))
                dj = jnp.sum(qs * ks[j:j + 1] * e, axis=-1, keepdims=True)
                a_diag = jnp.where(sub_col == j, dj, a_diag)
            o_blk = _dot(a_diag.astype(BF16), vh[r0:r0 + GLA_SUB])
            if blk > 0:
                b_ref0 = bs[0:1]
                q_t = qs * jnp.exp(bs - b_ref0)
                k_t = kh[0:r0] * jnp.exp(b_ref0 - bh[0:r0])
                a_off = _dot_nt(q_t.astype(BF16), k_t.astype(BF16))
                o_blk = o_blk + _dot(a_off.astype(BF16), vh[0:r0])
            o_rows.append(o_blk)
        return jnp.concatenate(o_rows, axis=0), qh * jnp.exp(bh), kh * jnp.exp(bh[c - 1:c] - bh)

    def chunk_exact(rows, b_all):
        for h in range(GLA_HEADS):
            bh = b_all[:, h * dk:(h + 1) * dk]
            qh = q_ref[rows, h * dk:(h + 1) * dk].astype(F32) * (1.0 / math.sqrt(dk))
            kh = k_ref[rows, h * dk:(h + 1) * dk].astype(F32)
            vh = v_ref[rows, h * dv:(h + 1) * dv]
            st = state_ref[h]
            o_intra, q_dec, k_dec = intra_exact(qh, kh, bh, vh)
            o = _dot_nt(q_dec.astype(BF16), st.astype(BF16)) + o_intra
            state_ref[h] = st * jnp.exp(bh[c - 1:c]) + _dot_tn(vh, k_dec.astype(BF16))
            o = _rms(o) * gn_ref[...] * _silu(r_ref[rows, h * dv:(h + 1) * dv].astype(F32))
            o_ref[rows, h * dv:(h + 1) * dv] = o.astype(o_ref.dtype)

    for ci in range(tm // c):
        rows = slice(ci * c, (ci + 1) * c)
        b_all = b_tile[rows]
        decay_range = jnp.max(b_all[0:1, :] - b_all[c - 1:c, :])
        factorable = decay_range <= GLA_FACTOR_RANGE
        pl.when(factorable)(functools.partial(chunk_factored, rows, b_all))
        pl.when(jnp.logical_not(factorable))(functools.partial(chunk_exact, rows, b_all))


def _gla(main, small, wg, bg, gnorm, *, key_dim, value_dim, tm):
    s = main.shape[0]
    full = lambda a: pl.BlockSpec(a.shape, lambda i: (0,) * a.ndim)
    wg1 = wg.astype(BF16)
    wg2 = (wg - wg1.astype(F32)).astype(BF16)
    return pl.pallas_call(
        _gla_kernel,
        grid=(s // tm,),
        in_specs=[
            pl.BlockSpec((tm, key_dim), lambda i: (i, 0)),
            pl.BlockSpec((tm, key_dim), lambda i: (i, 1)),
            pl.BlockSpec((tm, value_dim), lambda i: (i, 1)),
            pl.BlockSpec((tm, value_dim), lambda i: (i, 2)),
            pl.BlockSpec((tm, LANES), lambda i: (i, 0)),
            full(wg1), full(wg2), full(bg), full(gnorm),
        ],
        out_specs=pl.BlockSpec((tm, value_dim), lambda i: (i, 0)),
        out_shape=jax.ShapeDtypeStruct((s, value_dim), BF16),
        scratch_shapes=[pltpu.VMEM((GLA_HEADS, value_dim // GLA_HEADS, key_dim // GLA_HEADS), F32)],
        compiler_params=_cparams("arbitrary"),
        name="gated_linear_attention",
    )(main, main, main, main, small, wg1, wg2, bg, gnorm)


def _out_proj_kernel(a1_ref, a2_ref, w_ref, x_ref, gate_ref, o_ref):
    half = a1_ref.shape[1]
    mix = _dot(a1_ref[...], w_ref[0:half, :]) + _dot(a2_ref[...], w_ref[half:, :])
    o_ref[...] = x_ref[...] + gate_ref[...] * mix


def _out_proj(a1, a2, col1, col2, w, layer, x, gate, *, tm):
    s, d = x.shape
    half = w.shape[1] // 2
    return pl.pallas_call(
        _out_proj_kernel,
        grid=(s // tm,),
        in_specs=[
            pl.BlockSpec((tm, half), lambda i: (i, col1)),
            pl.BlockSpec((tm, half), lambda i: (i, col2)),
            pl.BlockSpec((None,) + w.shape[1:], lambda i: (layer, 0, 0)),
            pl.BlockSpec((tm, d), lambda i: (i, 0)),
            pl.BlockSpec((1, d), lambda i: (0, 0)),
        ],
        out_specs=pl.BlockSpec((tm, d), lambda i: (i, 0)),
        out_shape=jax.ShapeDtypeStruct((s, d), F32),
        compiler_params=_cparams("parallel"),
        name="out_proj_residual",
    )(a1, a2, w, x, gate)


def _ffn_kernel(x_ref, g_ref, sc_ref, sh_ref, gate_ref, fin_ref, w1_ref, w3_ref, w2_ref, o_ref, h_ref, acc_ref,
                *, final_norm):
    j = pl.program_id(1)

    @pl.when(j == 0)
    def _():
        h_ref[...] = _modulated_norm(x_ref[...], g_ref[...], sc_ref[...], sh_ref[...]).astype(BF16)
        acc_ref[...] = jnp.zeros(acc_ref.shape, F32)

    h = h_ref[...]
    a = _dot(h, w1_ref[...])
    t = (_silu(a) * _dot(h, w3_ref[...])).astype(BF16)
    acc_ref[...] += _dot(t, w2_ref[...])

    @pl.when(j == pl.num_programs(1) - 1)
    def _():
        y = x_ref[...] + gate_ref[...] * acc_ref[...]
        if final_norm:
            y = _rms(y) * fin_ref[...]
        o_ref[...] = y


def _ffn(x, g, sc, sh, gate, fin, w1, w3, w2, layer, *, tm, tf, final_norm):
    s, d = x.shape
    dff = w1.shape[2]
    vec = pl.BlockSpec((1, d), lambda i, j: (0, 0))
    return pl.pallas_call(
        functools.partial(_ffn_kernel, final_norm=final_norm),
        grid=(s // tm, dff // tf),
        in_specs=[
            pl.BlockSpec((tm, d), lambda i, j: (i, 0)),
            vec, vec, vec, vec, vec,
            pl.BlockSpec((None, d, tf), lambda i, j: (layer, 0, j)),
            pl.BlockSpec((None, d, tf), lambda i, j: (layer, 0, j)),
            pl.BlockSpec((None, tf, d), lambda i, j: (layer, j, 0)),
        ],
        out_specs=pl.BlockSpec((tm, d), lambda i, j: (i, 0)),
        out_shape=jax.ShapeDtypeStruct((s, d), F32),
        scratch_shapes=[pltpu.VMEM((tm, d), BF16), pltpu.VMEM((tm, d), F32)],
        compiler_params=_cparams("parallel", "arbitrary"),
        name="swiglu_ffn",
    )(x, g, sc, sh, gate, fin, w1, w3, w2)


def _rot_cols(w):
    half = w.shape[-1] // 2
    return jnp.concatenate([-w[..., half:], w[..., :half]], axis=-1)


def _pad_cols(w, n):
    return jnp.pad(w, [(0, 0)] * (w.ndim - 1) + [(0, n - w.shape[-1])])


def _pad_lanes(p):
    return jnp.concatenate([p, jnp.zeros((p.shape[0], LANES - p.shape[1]), p.dtype)], axis=-1)


def _pad_rows(p):
    return jnp.concatenate([p, jnp.zeros((LANES - p.shape[0], p.shape[1]), p.dtype)], axis=0)


def _even_in_layout_kernel(w_ref, main_ref, small_ref):
    o_kr = MLA_Q_RANK + MLA_KV_RANK
    o_qkv = o_kr + MLA_ROPE
    o_ba = o_qkv + 4 * GDN_DIM
    half = MLA_ROPE // 2
    w = w_ref[...]
    main_ref[:o_kr, :] = w[:o_kr].astype(BF16)
    main_ref[o_kr:, :] = w[o_qkv:o_ba].astype(BF16)
    kr = w[o_kr:o_qkv]
    rot = jnp.concatenate([-kr[half:], kr[:half]], axis=0)
    small_ref[...] = jnp.concatenate([_pad_rows(kr), _pad_rows(rot), _pad_rows(w[o_ba:])], axis=0).astype(BF16)


def _gla_in_layout_kernel(w_ref, main_ref, small_ref):
    n = main_ref.shape[0]
    w = w_ref[...]
    main_ref[...] = w[:n].astype(BF16)
    small_ref[...] = _pad_rows(w[n:]).astype(BF16)


def _in_weight_layout(body, w_in, n_main, n_small, *, tc=256):
    w_t = jnp.swapaxes(w_in, 1, 2)
    nl, n, d = w_t.shape
    return pl.pallas_call(
        body,
        grid=(nl, d // tc),
        in_specs=[pl.BlockSpec((None, n, tc), lambda l, i: (l, 0, i))],
        out_specs=[pl.BlockSpec((None, n_main, tc), lambda l, i: (l, 0, i)),
                   pl.BlockSpec((None, n_small, tc), lambda l, i: (l, 0, i))],
        out_shape=[jax.ShapeDtypeStruct((nl, n_main, d), BF16), jax.ShapeDtypeStruct((nl, n_small, d), BF16)],
        compiler_params=_cparams("parallel", "parallel"),
        name="in_weight_layout",
    )(w_t)


def _mla_weights(w_uq, w_ukv):
    r = w_uq.shape[0]
    wq = w_uq.reshape(r, MLA_HEADS, MLA_QK)
    nope, rope = wq[..., :MLA_NOPE], wq[..., MLA_NOPE:]
    flat = lambda t: t.reshape(r, MLA_HEADS * LANES)
    wq_all = jnp.concatenate([flat(nope), flat(_pad_cols(rope, LANES)), flat(_pad_cols(_rot_cols(rope), LANES))], axis=1)
    wkv = w_ukv.reshape(w_ukv.shape[0], MLA_HEADS, MLA_NOPE + MLA_V)
    wkv_all = jnp.concatenate([flat(wkv[..., :MLA_NOPE]), flat(wkv[..., MLA_NOPE:])], axis=1)
    return wq_all.astype(BF16), wkv_all.astype(BF16)


def kernel(x, c, positions, norm_g, ada_w, ada_b, ab_w_in, mla_q_norm, mla_w_uq, mla_kv_norm, mla_w_ukv, gdn_conv_w,
           gdn_a_log, gdn_dt_bias, gdn_norm, ab_w_out, gla_w_in, gla_w_gk2, gla_b_gk2, gla_norm, gla_w_out, ffn_w1,
           ffn_w3, ffn_w2, final_norm):
    batch, s, d = x.shape
    assert batch == 1 and c.shape == (1, d)
    depth = norm_g.shape[0]
    key_dim = gla_w_gk2.shape[2]
    value_dim = gla_w_out.shape[1]

    ada_w2, ada_b2 = ada_w.reshape(depth * 2, d, 3 * d), ada_b.reshape(depth * 2, 1, 3 * d)
    mods0, silu_c = _ada_modulation(c.reshape(d, 1), ada_w2, ada_b2, 1)
    mods = {0: mods0[0]}
    xs = x.reshape(s, d)
    pos_col = positions.reshape(s, 1)
    half = MLA_ROPE // 2
    inv_freq = ROPE_THETA ** (-jnp.arange(half, dtype=F32) / half)
    invf = jnp.tile(inv_freq, LANES // half).reshape(1, LANES)
    row = lambda v: v.reshape(1, -1)

    n_gla = 2 * key_dim + 2 * value_dim
    n_even = MLA_Q_RANK + MLA_KV_RANK + 4 * GDN_DIM
    ab_main_w, ab_small_w = _in_weight_layout(_even_in_layout_kernel, ab_w_in, n_even, 3 * LANES)
    gla_main_w, gla_small_w = _in_weight_layout(_gla_in_layout_kernel, gla_w_in, n_gla, LANES)
    ab_out_w, gla_out_w = ab_w_out.astype(BF16), gla_w_out.astype(BF16)
    assert depth % 2 == 0

    for layer in range(depth):
        i = layer // 2
        m = mods[2 * layer]
        shift, scale, gate = m[:, :d], m[:, d:2 * d], m[:, 2 * d:]
        gain = row(norm_g[layer, 0])
        if layer % 2 == 0:
            main, small = _norm_proj(xs, gain, scale, shift, ab_main_w, ab_small_w, i, tm=TM_PROJ, tn=TN_PROJ)
            wq, wkv = _mla_weights(mla_w_uq[i], mla_w_ukv[i])
            q, k, v = _mla_up(main, small, pos_col, invf, row(mla_q_norm[i]), row(mla_kv_norm[i]), wq, wkv,
                              tm=TM_MLA_UP)
            first = 2 * layer + 1
            o_a, w1, w3, w2, later = _causal_attention(q, k, v, (ffn_w1, ffn_w3, ffn_w2), layer, 2,
                                                       (silu_c, ada_w2, ada_b2, first, 4), t=T_ATTN)
            mods.update({first + n: later[n] for n in range(4)})
            lane_pad = lambda vec: jnp.pad(vec, (GDN_HEADS, LANES - 2 * GDN_HEADS)).reshape(1, LANES)
            qd, kd, u, w, a, gc = _gdn_prep(main, small, gdn_conv_w[i], lane_pad(gdn_a_log[i]),
                                            lane_pad(gdn_dt_bias[i]), tm=TM_GDN_PREP)
            o_b = _gdn_scan(qd, kd, u, w, a, gc, main, row(gdn_norm[i]), tm=TM_GDN_SCAN)
            xs = _out_proj(o_a, o_b, 0, 0, ab_out_w, i, xs, gate, tm=TM_OUT)
        else:
            main, small = _norm_proj(xs, gain, scale, shift, gla_main_w, gla_small_w, i, tm=TM_PROJ,
                                     tn=TN_PROJ)
            wg = jnp.pad(gla_w_gk2[i], ((0, LANES - GLA_GATE_RANK), (0, 0)))
            o_c = _gla(main, small, wg, row(gla_b_gk2[i]), row(gla_norm[i]), key_dim=key_dim, value_dim=value_dim,
                       tm=TM_GLA)
            xs = _out_proj(o_c, o_c, 0, 1, gla_out_w, i, xs, gate, tm=TM_OUT)
        m = mods[2 * layer + 1]
        shift, scale, gate = m[:, :d], m[:, d:2 * d], m[:, 2 * d:]
        xs = _ffn(xs, row(norm_g[layer, 1]), scale, shift, gate, row(final_norm), w1, w3, w2, layer % 2,
                  tm=TM_FFN, tf=TF_FFN,
                  final_norm=(layer == depth - 1))
    return xs.reshape(batch, s, d)
```

```python
import functools
import math

import jax
import jax.numpy as jnp
from jax import lax
from jax.experimental import pallas as pl
from jax.experimental.pallas import tpu as pltpu

F32 = jnp.float32
BF16 = jnp.bfloat16

EPS = 1e-6
ROPE_THETA = 10000.0
MLA_HEADS = 8
MLA_NOPE = 128
MLA_ROPE = 64
MLA_V = 128
MLA_QK = MLA_NOPE + MLA_ROPE
MLA_QK_PAD = 256
MLA_V_PAD = MLA_V + 16
MLA_Q_RANK = 512
MLA_KV_RANK = 512
GDN_HEADS = 8
GDN_HEAD_DIM = 128
GDN_DIM = GDN_HEADS * GDN_HEAD_DIM
GDN_CONV = 4
GLA_HEADS = 4
GLA_GATE_RANK = 16
GLA_GATE_NORMALIZER = 16.0
GLA_SUB = 16
GLA_FACTOR_RANGE = 60.0
CHUNK = 64
LANES = 128
VMEM_LIMIT = 48 * 1024 * 1024

TM_PROJ, TN_PROJ = 1024, 1024
TM_MLA_UP = 256
T_ATTN = 1024
TM_GDN_PREP = 256
TM_GDN_SCAN = 512
TM_GLA = 256
TM_OUT = 512
TM_FFN, TF_FFN = 512, 512


def _cparams(*sem):
    return pltpu.CompilerParams(dimension_semantics=sem, vmem_limit_bytes=VMEM_LIMIT)


def _dot(a, b):
    return jnp.dot(a, b, preferred_element_type=F32)


def _dot_nt(a, b):
    return lax.dot_general(a, b, (((1,), (1,)), ((), ())), preferred_element_type=F32)


def _dot_tn(a, b):
    return lax.dot_general(a, b, (((0,), (0,)), ((), ())), preferred_element_type=F32)


def _split3(x):
    x1 = x.astype(BF16)
    r1 = x - x1.astype(F32)
    x2 = r1.astype(BF16)
    x3 = (r1 - x2.astype(F32)).astype(BF16)
    return x1, x2, x3


def _cumsum_rows(tri, x):
    x1, x2, x3 = _split3(x)
    return _dot(tri, x1) + _dot(tri, x2) + _dot(tri, x3)


def _silu(x):
    return x * jax.nn.sigmoid(x)


def _softplus(x):
    return jnp.maximum(x, 0.0) + jnp.log1p(jnp.exp(-jnp.abs(x)))


def _rms(x):
    return x * lax.rsqrt(jnp.mean(x * x, axis=-1, keepdims=True) + EPS)


def _ada_kernel(c_ref, w_ref, b_ref, o_ref, s_ref):
    s = _silu(c_ref[...])
    o_ref[0] = jnp.sum(w_ref[0] * s, axis=0, keepdims=True) + b_ref[0]
    s_ref[...] = jnp.broadcast_to(s, s_ref.shape)


def _ada_modulation(c_col, w, b, n_mod, *, tn=768):
    _, d, n = w.shape
    return pl.pallas_call(
        _ada_kernel,
        grid=(n_mod, n // tn),
        in_specs=[
            pl.BlockSpec((d, 1), lambda l, j: (0, 0)),
            pl.BlockSpec((1, d, tn), lambda l, j: (l, 0, j)),
            pl.BlockSpec((1, 1, tn), lambda l, j: (l, 0, j)),
        ],
        out_specs=[pl.BlockSpec((1, 1, tn), lambda l, j: (l, 0, j)), pl.BlockSpec((d, LANES), lambda l, j: (0, 0))],
        out_shape=[jax.ShapeDtypeStruct((n_mod, 1, n), F32), jax.ShapeDtypeStruct((d, LANES), F32)],
        compiler_params=_cparams("arbitrary", "arbitrary"),
        name="ada_modulation",
    )(c_col, w, b)


def _modulated_norm(x, g, sc, sh):
    return _rms(x) * (g * (1.0 + sc)) + sh


def _norm_proj_kernel(x_ref, g_ref, sc_ref, sh_ref, w_ref, ws_ref, o_ref, os_ref, h_ref):
    @pl.when(pl.program_id(1) == 0)
    def _():
        h = _modulated_norm(x_ref[...], g_ref[...], sc_ref[...], sh_ref[...]).astype(BF16)
        h_ref[...] = h
        os_ref[...] = _dot_nt(h, ws_ref[...])

    o_ref[...] = _dot_nt(h_ref[...], w_ref[...]).astype(o_ref.dtype)


def _norm_proj(x, g, sc, sh, w, ws, layer, *, tm, tn):
    s, d = x.shape
    n, ns = w.shape[1], ws.shape[1]
    assert s % tm == 0 and n % tn == 0
    vec = pl.BlockSpec((1, d), lambda i, j: (0, 0))
    return pl.pallas_call(
        _norm_proj_kernel,
        grid=(s // tm, n // tn),
        in_specs=[
            pl.BlockSpec((tm, d), lambda i, j: (i, 0)),
            vec, vec, vec,
            pl.BlockSpec((None, tn, d), lambda i, j: (layer, j, 0)),
            pl.BlockSpec((None, ns, d), lambda i, j: (layer, 0, 0)),
        ],
        out_specs=[
            pl.BlockSpec((tm, tn), lambda i, j: (i, j)),
            pl.BlockSpec((tm, ns), lambda i, j: (i, 0)),
        ],
        out_shape=[jax.ShapeDtypeStruct((s, n), BF16), jax.ShapeDtypeStruct((s, ns), F32)],
        scratch_shapes=[pltpu.VMEM((tm, d), BF16)],
        compiler_params=_cparams("parallel", "arbitrary"),
        name="norm_proj",
    )(x, g, sc, sh, w, ws)


def _mla_up_kernel(cq_ref, ckv_ref, kr_ref, krr_ref, pos_ref, invf_ref, qn_ref, kvn_ref, wq_ref, wkv_ref,
                   q_out, k_out, v_out):
    hd = MLA_HEADS * LANES
    cqn = (_rms(cq_ref[...].astype(F32)) * qn_ref[...]).astype(BF16)
    ckvn = (_rms(ckv_ref[...].astype(F32)) * kvn_ref[...]).astype(BF16)
    qall = _dot(cqn, wq_ref[...])
    kv = _dot(ckvn, wkv_ref[...])
    ang = pos_ref[...].astype(F32) * invf_ref[...]
    cos, sin = jnp.cos(ang), jnp.sin(ang)
    k_rope = kr_ref[...] * cos + krr_ref[...] * sin
    scale = math.log2(math.e) / math.sqrt(MLA_QK)
    ones = jnp.ones((MLA_V_PAD - MLA_V, cq_ref.shape[0]), F32)
    for h in range(MLA_HEADS):
        lo, hi = h * LANES, (h + 1) * LANES
        q_rope = qall[:, hd + lo:hd + hi] * cos + qall[:, 2 * hd + lo:2 * hd + hi] * sin
        q_out[h] = (jnp.concatenate([qall[:, lo:hi], q_rope], axis=-1) * scale).astype(BF16)
        k_out[h] = jnp.concatenate([kv[:, lo:hi], k_rope], axis=-1).astype(BF16)
        v_out[h] = jnp.concatenate([kv[:, hd + lo:hd + hi].T, ones], axis=0).astype(BF16)


def _mla_up(main, small, pos_col, invf, qn, kvn, wq, wkv, *, tm):
    s = main.shape[0]
    full = lambda a: pl.BlockSpec(a.shape, lambda i: (0,) * a.ndim)
    qk_shape = jax.ShapeDtypeStruct((MLA_HEADS, s, MLA_QK_PAD), BF16)
    return pl.pallas_call(
        _mla_up_kernel,
        grid=(s // tm,),
        in_specs=[
            pl.BlockSpec((tm, MLA_Q_RANK), lambda i: (i, 0)),
            pl.BlockSpec((tm, MLA_KV_RANK), lambda i: (i, 1)),
            pl.BlockSpec((tm, LANES), lambda i: (i, 0)),
            pl.BlockSpec((tm, LANES), lambda i: (i, 1)),
            pl.BlockSpec((tm, 1), lambda i: (i, 0)),
            full(invf), full(qn), full(kvn), full(wq), full(wkv),
        ],
        out_specs=[
            pl.BlockSpec((MLA_HEADS, tm, MLA_QK_PAD), lambda i: (0, i, 0)),
            pl.BlockSpec((MLA_HEADS, tm, MLA_QK_PAD), lambda i: (0, i, 0)),
            pl.BlockSpec((MLA_HEADS, MLA_V_PAD, tm), lambda i: (0, 0, i)),
        ],
        out_shape=[qk_shape, qk_shape, jax.ShapeDtypeStruct((MLA_HEADS, MLA_V_PAD, s), BF16)],
        compiler_params=_cparams("parallel"),
        name="mla_up",
    )(main, main, small, small, pos_col, invf, qn, kvn, wq, wkv)


def _attn_kernel(q_ref, k_ref, v_ref, *rest, t, n_side):
    side_in, (sc_ref, aw_ref, ab_ref) = rest[:n_side], rest[n_side:n_side + 3]
    (o_ref, *side_out), mods_ref = rest[n_side + 3:2 * n_side + 4], rest[2 * n_side + 4]
    s0_ref, s1_ref, m_ref, acc_ref = rest[2 * n_side + 5:]
    for src, dst in zip(side_in, side_out):
        dst[...] = src[...].astype(dst.dtype)
    silu_c = jnp.concatenate([sc_ref[...]] * (aw_ref.shape[1] // LANES), axis=1)
    mods_ref[...] = jnp.sum(aw_ref[...] * silu_c, axis=0, keepdims=True) + ab_ref[...]
    i = pl.program_id(1)
    q = q_ref[0]
    dv = MLA_V

    def tile(j):
        return pl.ds(pl.multiple_of(j * t, t), t)

    def scores(s_ref, j):
        s_ref[...] = _dot_nt(k_ref[0, tile(j), :], q)

    def update(s_ref, j, diagonal):
        s = s_ref[...]
        if diagonal:
            key = lax.broadcasted_iota(jnp.int32, (t, t), 0)
            qry = lax.broadcasted_iota(jnp.int32, (t, t), 1)
            s = jnp.where(key <= qry, s, -jnp.inf)
        m = m_ref[...]
        m_new = jnp.maximum(m, jnp.max(s, axis=0, keepdims=True))
        p = jnp.exp2(s - m_new)
        alpha = jnp.exp2(m - m_new)
        acc_ref[...] = alpha * acc_ref[...] + _dot(v_ref[0, :, tile(j)], p.astype(BF16))
        m_ref[...] = m_new

    m_ref[...] = jnp.full(m_ref.shape, -jnp.inf, F32)
    acc_ref[...] = jnp.zeros(acc_ref.shape, F32)
    scores(s0_ref, 0)

    def pair(jj, carry):
        j = 2 * jj
        scores(s1_ref, j + 1)
        update(s0_ref, j, False)
        scores(s0_ref, j + 2)
        update(s1_ref, j + 1, False)
        return carry

    lax.fori_loop(0, i // 2, pair, 0)

    @pl.when(i % 2 == 0)
    def _():
        update(s0_ref, i, True)

    @pl.when(i % 2 == 1)
    def _():
        scores(s1_ref, i)
        update(s0_ref, i - 1, False)
        update(s1_ref, i, True)

    acc = acc_ref[...]
    o_ref[...] = (acc[:dv] / acc[dv:dv + 1]).T.astype(o_ref.dtype)


def _causal_attention(q, k, v, side, layer0, n_layers, ada, *, t):
    h, s, dqk = q.shape
    dvp = v.shape[1]
    nq = s // t
    steps = h * nq
    assert s % t == 0 and steps % n_layers == 0
    step = lambda hh, i: hh * nq + i
    per_layer = steps // n_layers
    side_in, side_out, side_shapes = [], [], []
    for w in side:
        rows = w.shape[1] // per_layer
        assert rows * per_layer == w.shape[1] and rows % 16 == 0
        blk = (None, rows) + w.shape[2:]
        side_in.append(pl.BlockSpec(blk, lambda hh, i: (layer0 + step(hh, i) // per_layer, step(hh, i) % per_layer, 0)))
        side_out.append(pl.BlockSpec(blk, lambda hh, i: (step(hh, i) // per_layer, step(hh, i) % per_layer, 0)))
        side_shapes.append(jax.ShapeDtypeStruct((n_layers,) + w.shape[1:], BF16))
    silu_c, ada_w, ada_b, first, count = ada
    n_mod, d, n3 = ada_w.shape
    per_mod = steps // count
    slab = n3 // per_mod
    assert per_mod * count == steps and slab * per_mod == n3 and slab % LANES == 0
    mod_idx = lambda hh, i: jnp.minimum(first + step(hh, i) // per_mod, n_mod - 1)
    ada_in = [
        pl.BlockSpec((d, LANES), lambda hh, i: (0, 0)),
        pl.BlockSpec((None, d, slab), lambda hh, i: (mod_idx(hh, i), 0, step(hh, i) % per_mod)),
        pl.BlockSpec((None, 1, slab), lambda hh, i: (mod_idx(hh, i), 0, step(hh, i) % per_mod)),
    ]
    mods_out = pl.BlockSpec((None, 1, slab), lambda hh, i: (step(hh, i) // per_mod, 0, step(hh, i) % per_mod))
    return pl.pallas_call(
        functools.partial(_attn_kernel, t=t, n_side=len(side)),
        grid=(h, nq),
        in_specs=[
            pl.BlockSpec((1, t, dqk), lambda hh, i: (hh, i, 0)),
            pl.BlockSpec((1, s, dqk), lambda hh, i: (hh, 0, 0)),
            pl.BlockSpec((1, dvp, s), lambda hh, i: (hh, 0, 0)),
        ] + side_in + ada_in,
        out_specs=[pl.BlockSpec((t, MLA_V), lambda hh, i: (i, hh))] + side_out + [mods_out],
        out_shape=[jax.ShapeDtypeStruct((s, h * MLA_V), BF16)] + side_shapes
        + [jax.ShapeDtypeStruct((count, 1, n3), F32)],
        scratch_shapes=[pltpu.VMEM((t, t), F32), pltpu.VMEM((t, t), F32), pltpu.VMEM((1, t), F32),
                        pltpu.VMEM((dvp, t), F32)],
        compiler_params=_cparams("arbitrary", "arbitrary"),
        name="mla_attention",
    )(q, k, v, *side, silu_c, ada_w, ada_b)


def _bdot(a, b):
    return lax.dot_general(a, b, (((2,), (1,)), ((0,), (0,))), preferred_element_type=F32)


def _bdot_nt(a, b):
    return lax.dot_general(a, b, (((2,), (2,)), ((0,), (0,))), preferred_element_type=F32)


def _bdot_tn(a, b):
    return lax.dot_general(a, b, (((1,), (1,)), ((0,), (0,))), preferred_element_type=F32)


def _unit_lower_inverse(low, eye, blk=16):
    n = low.shape[-1]
    bdot = lambda a, b: _bdot(a.astype(BF16), b.astype(BF16))
    rb = lax.broadcasted_iota(jnp.int32, (n, n), 0) // blk
    cb = lax.broadcasted_iota(jnp.int32, (n, n), 1) // blk
    diag = jnp.where(rb == cb, low, 0.0)
    p = diag
    d_inv = eye - diag
    for _ in range(int(math.log2(blk)) - 1):
        p = bdot(p, p)
        d_inv = d_inv + bdot(d_inv, p)
    m = bdot(d_inv, low - diag)
    assert n // blk == 4
    i_m = eye - m
    return bdot(i_m + bdot(i_m, bdot(m, m)), d_inv)


def _gdn_prep_kernel(q_ref, k_ref, v_ref, qh_ref, kh_ref, vh_ref, sm_ref, cw_ref, alog_ref, dtb_ref,
                     qd_out, kd_out, u_out, w_out, a_out, gc_out):
    c, d, nh = CHUNK, GDN_HEAD_DIM, GDN_HEADS
    tm = q_ref.shape[0]
    nc = tm // c
    halo = qh_ref.shape[0]

    first = pl.program_id(0) == 0
    n_sh = GDN_CONV - 1
    r = lax.broadcasted_iota(jnp.int32, (n_sh * tm, tm), 0)
    src = r % tm - (n_sh - r // tm)
    sel = jnp.where(lax.broadcasted_iota(jnp.int32, (n_sh * tm, tm), 1) == src, 1.0, 0.0).astype(BF16)
    rh = lax.broadcasted_iota(jnp.int32, (n_sh * tm, halo), 0)
    src_h = rh % tm - (n_sh - rh // tm) + halo
    sel_h = jnp.where(lax.broadcasted_iota(jnp.int32, (n_sh * tm, halo), 1) == src_h, 1.0, 0.0).astype(BF16)
    parts = []
    for n, (cur, prev) in enumerate(((q_ref, qh_ref), (k_ref, kh_ref), (v_ref, vh_ref))):
        cw = cw_ref[:, n * GDN_DIM:(n + 1) * GDN_DIM]
        x = cur[...]
        prev_rows = jnp.where(first, jnp.zeros_like(prev[...]), prev[...])
        sh = _dot(sel, x) + _dot(sel_h, prev_rows)
        yn = x.astype(F32) * cw[n_sh:n_sh + 1]
        for j in range(n_sh):
            yn = yn + sh[j * tm:(j + 1) * tm] * cw[j:j + 1]
        parts.append(yn)
    y = _silu(jnp.concatenate(parts, axis=1))

    sm = sm_ref[...]
    beta_all = jax.nn.sigmoid(sm)
    g_all = -jnp.exp(alog_ref[...]) * _softplus(sm + dtb_ref[...])
    row = lax.broadcasted_iota(jnp.int32, (tm, tm), 0)
    col = lax.broadcasted_iota(jnp.int32, (tm, tm), 1)
    tri = jnp.where((col <= row) & (row // c == col // c), 1.0, 0.0).astype(BF16)
    gc_all = _cumsum_rows(tri, g_all)
    gc_out[...] = gc_all
    zpad = jnp.zeros((LANES - c, LANES), F32)
    gc_t = [jnp.concatenate([gc_all[ci * c:(ci + 1) * c], zpad], axis=0).T for ci in range(nc)]

    def heads(arr, off):
        return jnp.stack([arr[ci * c:(ci + 1) * c, off + h * d:off + (h + 1) * d]
                          for ci in range(nc) for h in range(nh)])

    def cols(arr, off):
        return jnp.stack([arr[ci * c:(ci + 1) * c, off + h:off + h + 1] for ci in range(nc) for h in range(nh)])

    q = heads(y, 0)
    k = heads(y, GDN_DIM)
    v = heads(y, 2 * GDN_DIM)
    q = q * lax.rsqrt(jnp.sum(q * q, axis=-1, keepdims=True) + EPS) * (1.0 / math.sqrt(d))
    k = k * lax.rsqrt(jnp.sum(k * k, axis=-1, keepdims=True) + EPS)
    beta = cols(beta_all, 0)
    gc = cols(gc_all, nh)
    gcr = jnp.stack([gc_t[ci][nh + h:nh + h + 1, 0:c] for ci in range(nc) for h in range(nh)])
    r2 = lax.broadcasted_iota(jnp.int32, (c, c), 0)
    c2 = lax.broadcasted_iota(jnp.int32, (c, c), 1)
    decay = jnp.exp(jnp.where(c2 <= r2, gc - gcr, -jnp.inf))
    eye = jnp.where(c2 == r2, 1.0, 0.0).astype(F32)
    kbeta = k * beta
    kb16 = k.astype(BF16)
    low = jnp.where(c2 < r2, _bdot_nt(kbeta.astype(BF16), kb16) * decay, 0.0)
    t_inv = _unit_lower_inverse(low, eye)
    egc = jnp.exp(gc)
    rhs = jnp.concatenate([v * beta, kbeta * egc], axis=-1)
    sol = _bdot(t_inv.astype(BF16), rhs.astype(BF16))
    intra = _bdot_nt(q.astype(BF16), kb16) * decay
    g_last = gcr[:, :, c - 1:c]
    k_dec = k * jnp.exp(g_last - gc)
    q_dec = q * egc
    for ci in range(nc):
        rows = slice(ci * c, (ci + 1) * c)
        for h in range(nh):
            b = ci * nh + h
            lanes = slice(h * d, (h + 1) * d)
            qd_out[rows, lanes] = q_dec[b].astype(BF16)
            kd_out[rows, lanes] = k_dec[b].astype(BF16)
            u_out[rows, lanes] = sol[b, :, :d]
            w_out[rows, lanes] = sol[b, :, d:].astype(BF16)
            a_out[h, rows, :] = intra[b].astype(BF16)


def _gdn_prep(main, small, conv_w, alog, dtb, *, tm, halo=8):
    s = main.shape[0]
    full = lambda a: pl.BlockSpec(a.shape, lambda i: (0,) * a.ndim)
    blk = lambda jcol: pl.BlockSpec((tm, GDN_DIM), lambda i: (i, jcol))
    prev = lambda jcol: pl.BlockSpec((halo, GDN_DIM), lambda i: (jnp.maximum(i * (tm // halo) - 1, 0), jcol))
    act = jax.ShapeDtypeStruct((s, GDN_DIM), BF16)
    return pl.pallas_call(
        _gdn_prep_kernel,
        grid=(s // tm,),
        in_specs=[blk(1), blk(2), blk(3), prev(1), prev(2), prev(3), pl.BlockSpec((tm, LANES), lambda i: (i, 2)),
                  full(conv_w), full(alog), full(dtb)],
        out_specs=[blk(0), blk(0), blk(0), blk(0),
                   pl.BlockSpec((GDN_HEADS, tm, CHUNK), lambda i: (0, i, 0)),
                   pl.BlockSpec((tm, LANES), lambda i: (i, 0))],
        out_shape=[act, act, jax.ShapeDtypeStruct((s, GDN_DIM), F32), act,
                   jax.ShapeDtypeStruct((GDN_HEADS, s, CHUNK), BF16), jax.ShapeDtypeStruct((s, LANES), F32)],
        compiler_params=_cparams("parallel"),
        name="gdn_prep",
    )(main, main, main, main, main, main, small, conv_w, alog, dtb)


def _gdn_scan_kernel(qd_ref, kd_ref, u_ref, w_ref, a_ref, gc_ref, z_ref, gn_ref, o_ref, state_ref):
    c, d, nh = CHUNK, GDN_HEAD_DIM, GDN_HEADS

    @pl.when(pl.program_id(0) == 0)
    def _():
        state_ref[...] = jnp.zeros(state_ref.shape, F32)

    def heads(ref, rows):
        return jnp.stack([ref[rows, h * d:(h + 1) * d] for h in range(nh)])

    for ci in range(qd_ref.shape[0] // c):
        rows = slice(ci * c, (ci + 1) * c)
        last = ci * c + c - 1
        g_last = jnp.stack([gc_ref[last:last + 1, nh + h:nh + h + 1] for h in range(nh)])
        st = state_ref[...]
        stb = st.astype(BF16)
        v_new = heads(u_ref, rows) - _bdot(heads(w_ref, rows), stb)
        v_newb = v_new.astype(BF16)
        o = _bdot(heads(qd_ref, rows), stb) + _bdot(a_ref[:, rows, :], v_newb)
        state_ref[...] = st * jnp.exp(g_last) + _bdot_tn(heads(kd_ref, rows), v_newb)
        o = _rms(o) * gn_ref[...]
        for h in range(nh):
            lanes = slice(h * d, (h + 1) * d)
            o_ref[rows, lanes] = (o[h] * _silu(z_ref[rows, lanes].astype(F32))).astype(o_ref.dtype)


def _gdn_scan(qd, kd, u, w, a, gc, main, gnorm, *, tm):
    s = qd.shape[0]
    blk = lambda jcol: pl.BlockSpec((tm, GDN_DIM), lambda i: (i, jcol))
    return pl.pallas_call(
        _gdn_scan_kernel,
        grid=(s // tm,),
        in_specs=[blk(0), blk(0), blk(0), blk(0),
                  pl.BlockSpec((GDN_HEADS, tm, CHUNK), lambda i: (0, i, 0)),
                  pl.BlockSpec((tm, LANES), lambda i: (i, 0)),
                  blk(4), pl.BlockSpec(gnorm.shape, lambda i: (0, 0))],
        out_specs=blk(0),
        out_shape=jax.ShapeDtypeStruct((s, GDN_DIM), BF16),
        scratch_shapes=[pltpu.VMEM((GDN_HEADS, GDN_HEAD_DIM, GDN_HEAD_DIM), F32)],
        compiler_params=_cparams("arbitrary"),
        name="gdn_scan",
    )(qd, kd, u, w, a, gc, main, gnorm)


def _gla_kernel(q_ref, k_ref, v_ref, r_ref, sm_ref, wg1_ref, wg2_ref, bg_ref, gn_ref, o_ref, state_ref):
    c = CHUNK
    dk = q_ref.shape[1] // GLA_HEADS
    dv = v_ref.shape[1] // GLA_HEADS

    @pl.when(pl.program_id(0) == 0)
    def _():
        state_ref[...] = jnp.zeros(state_ref.shape, F32)

    a1, a2, _ = _split3(sm_ref[...])
    w1, w2 = wg1_ref[...], wg2_ref[...]
    pre = _dot(a1, w1) + _dot(a1, w2) + _dot(a2, w1) + bg_ref[...]
    gk = (jnp.minimum(pre, 0.0) - jnp.log1p(jnp.exp(-jnp.abs(pre)))) * (1.0 / GLA_GATE_NORMALIZER)
    tm = q_ref.shape[0]
    trow = lax.broadcasted_iota(jnp.int32, (tm, tm), 0)
    tcol = lax.broadcasted_iota(jnp.int32, (tm, tm), 1)
    tri = jnp.where((tcol <= trow) & (trow // c == tcol // c), 1.0, 0.0).astype(BF16)
    b_tile = _cumsum_rows(tri, gk)
    row = lax.broadcasted_iota(jnp.int32, (c, c), 0)
    col = lax.broadcasted_iota(jnp.int32, (c, c), 1)
    sub_row = lax.broadcasted_iota(jnp.int32, (GLA_SUB, 1), 0)
    sub_col = lax.broadcasted_iota(jnp.int32, (GLA_SUB, GLA_SUB), 1)

    def chunk_factored(rows, b_all):
        heads = lambda ref, w: jnp.stack([ref[rows, h * w:(h + 1) * w] for h in range(GLA_HEADS)])
        bh = jnp.stack([b_all[:, h * dk:(h + 1) * dk] for h in range(GLA_HEADS)])
        q = heads(q_ref, dk).astype(F32) * (1.0 / math.sqrt(dk))
        k = heads(k_ref, dk).astype(F32)
        v = heads(v_ref, dv)
        b0, b_last = bh[:, 0:1], bh[:, c - 1:c]
        q_t = q * jnp.exp(bh - b0)
        k_t = k * jnp.exp(b0 - bh)
        a = jnp.where(col <= row, _bdot_nt(q_t.astype(BF16), k_t.astype(BF16)), 0.0)
        q_dec = q_t * jnp.exp(b0)
        k_dec = k_t * jnp.exp(b_last - b0)
        st = state_ref[...]
        o = _bdot_nt(q_dec.astype(BF16), st.astype(BF16)) + _bdot(a.astype(BF16), v)
        state_ref[...] = st * jnp.exp(b_last) + _bdot_tn(v, k_dec.astype(BF16))
        o = _rms(o) * gn_ref[...]
        for h in range(GLA_HEADS):
            lanes = slice(h * dv, (h + 1) * dv)
            o_ref[rows, lanes] = (o[h] * _silu(r_ref[rows, lanes].astype(F32))).astype(o_ref.dtype)

    def intra_exact(qh, kh, bh, vh):
        o_rows = []
        for blk in range(c // GLA_SUB):
            r0 = blk * GLA_SUB
            qs, bs = qh[r0:r0 + GLA_SUB], bh[r0:r0 + GLA_SUB]
            ks = kh[r0:r0 + GLA_SUB]
            a_diag = jnp.zeros((GLA_SUB, GLA_SUB), F32)
            for j in range(GLA_SUB):
                e = jnp.exp(jnp.where(sub_row >= j, bs - bs[j:j + 1], -jnp.inf))
                dj = jnp.sum(qs * ks[j:j + 1] * e, axis=-1, keepdims=True)
                a_diag = jnp.where(sub_col == j, dj, a_diag)
            o_blk = _dot(a_diag.astype(BF16), vh[r0:r0 + GLA_SUB])
            if blk > 0:
                b_ref0 = bs[0:1]
                q_t = qs * jnp.exp(bs - b_ref0)
                k_t = kh[0:r0] * jnp.exp(b_ref0 - bh[0:r0])
                a_off = _dot_nt(q_t.astype(BF16), k_t.astype(BF16))
                o_blk = o_blk + _dot(a_off.astype(BF16), vh[0:r0])
            o_rows.append(o_blk)
        return jnp.concatenate(o_rows, axis=0), qh * jnp.exp(bh), kh * jnp.exp(bh[c - 1:c] - bh)

    def chunk_exact(rows, b_all):
        for h in range(GLA_HEADS):
            bh = b_all[:, h * dk:(h + 1) * dk]
            qh = q_ref[rows, h * dk:(h + 1) * dk].astype(F32) * (1.0 / math.sqrt(dk))
            kh = k_ref[rows, h * dk:(h + 1) * dk].astype(F32)
            vh = v_ref[rows, h * dv:(h + 1) * dv]
            st = state_ref[h]
            o_intra, q_dec, k_dec = intra_exact(qh, kh, bh, vh)
            o = _dot_nt(q_dec.astype(BF16), st.astype(BF16)) + o_intra
            state_ref[h] = st * jnp.exp(bh[c - 1:c]) + _dot_tn(vh, k_dec.astype(BF16))
            o = _rms(o) * gn_ref[...] * _silu(r_ref[rows, h * dv:(h + 1) * dv].astype(F32))
            o_ref[rows, h * dv:(h + 1) * dv] = o.astype(o_ref.dtype)

    for ci in range(tm // c):
        rows = slice(ci * c, (ci + 1) * c)
        b_all = b_tile[rows]
        decay_range = jnp.max(b_all[0:1, :] - b_all[c - 1:c, :])
        factorable = decay_range <= GLA_FACTOR_RANGE
        pl.when(factorable)(functools.partial(chunk_factored, rows, b_all))
        pl.when(jnp.logical_not(factorable))(functools.partial(chunk_exact, rows, b_all))


def _gla(main, small, wg, bg, gnorm, *, key_dim, value_dim, tm):
    s = main.shape[0]
    full = lambda a: pl.BlockSpec(a.shape, lambda i: (0,) * a.ndim)
    wg1 = wg.astype(BF16)
    wg2 = (wg - wg1.astype(F32)).astype(BF16)
    return pl.pallas_call(
        _gla_kernel,
        grid=(s // tm,),
        in_specs=[
            pl.BlockSpec((tm, key_dim), lambda i: (i, 0)),
            pl.BlockSpec((tm, key_dim), lambda i: (i, 1)),
            pl.BlockSpec((tm, value_dim), lambda i: (i, 1)),
            pl.BlockSpec((tm, value_dim), lambda i: (i, 2)),
            pl.BlockSpec((tm, LANES), lambda i: (i, 0)),
            full(wg1), full(wg2), full(bg), full(gnorm),
        ],
        out_specs=pl.BlockSpec((tm, value_dim), lambda i: (i, 0)),
        out_shape=jax.ShapeDtypeStruct((s, value_dim), BF16),
        scratch_shapes=[pltpu.VMEM((GLA_HEADS, value_dim // GLA_HEADS, key_dim // GLA_HEADS), F32)],
        compiler_params=_cparams("arbitrary"),
        name="gated_linear_attention",
    )(main, main, main, main, small, wg1, wg2, bg, gnorm)


def _out_proj_kernel(a1_ref, a2_ref, w_ref, x_ref, gate_ref, o_ref):
    half = a1_ref.shape[1]
    mix = _dot(a1_ref[...], w_ref[0:half, :]) + _dot(a2_ref[...], w_ref[half:, :])
    o_ref[...] = x_ref[...] + gate_ref[...] * mix


def _out_proj(a1, a2, col1, col2, w, layer, x, gate, *, tm):
    s, d = x.shape
    half = w.shape[1] // 2
    return pl.pallas_call(
        _out_proj_kernel,
        grid=(s // tm,),
        in_specs=[
            pl.BlockSpec((tm, half), lambda i: (i, col1)),
            pl.BlockSpec((tm, half), lambda i: (i, col2)),
            pl.BlockSpec((None,) + w.shape[1:], lambda i: (layer, 0, 0)),
            pl.BlockSpec((tm, d), lambda i: (i, 0)),
            pl.BlockSpec((1, d), lambda i: (0, 0)),
        ],
        out_specs=pl.BlockSpec((tm, d), lambda i: (i, 0)),
        out_shape=jax.ShapeDtypeStruct((s, d), F32),
        compiler_params=_cparams("parallel"),
        name="out_proj_residual",
    )(a1, a2, w, x, gate)


def _ffn_kernel(x_ref, g_ref, sc_ref, sh_ref, gate_ref, fin_ref, w1_ref, w3_ref, w2_ref, o_ref, h_ref, acc_ref,
                *, final_norm):
    j = pl.program_id(1)

    @pl.when(j == 0)
    def _():
        h_ref[...] = _modulated_norm(x_ref[...], g_ref[...], sc_ref[...], sh_ref[...]).astype(BF16)
        acc_ref[...] = jnp.zeros(acc_ref.shape, F32)

    h = h_ref[...]
    a = _dot(h, w1_ref[...])
    t = (_silu(a) * _dot(h, w3_ref[...])).astype(BF16)
    acc_ref[...] += _dot(t, w2_ref[...])

    @pl.when(j == pl.num_programs(1) - 1)
    def _():
        y = x_ref[...] + gate_ref[...] * acc_ref[...]
        if final_norm:
            y = _rms(y) * fin_ref[...]
        o_ref[...] = y


def _ffn(x, g, sc, sh, gate, fin, w1, w3, w2, layer, *, tm, tf, final_norm):
    s, d = x.shape
    dff = w1.shape[2]
    vec = pl.BlockSpec((1, d), lambda i, j: (0, 0))
    return pl.pallas_call(
        functools.partial(_ffn_kernel, final_norm=final_norm),
        grid=(s // tm, dff // tf),
        in_specs=[
            pl.BlockSpec((tm, d), lambda i, j: (i, 0)),
            vec, vec, vec, vec, vec,
            pl.BlockSpec((None, d, tf), lambda i, j: (layer, 0, j)),
            pl.BlockSpec((None, d, tf), lambda i, j: (layer, 0, j)),
            pl.BlockSpec((None, tf, d), lambda i, j: (layer, j, 0)),
        ],
        out_specs=pl.BlockSpec((tm, d), lambda i, j: (i, 0)),
        out_shape=jax.ShapeDtypeStruct((s, d), F32),
        scratch_shapes=[pltpu.VMEM((tm, d), BF16), pltpu.VMEM((tm, d), F32)],
        compiler_params=_cparams("parallel", "arbitrary"),
        name="swiglu_ffn",
    )(x, g, sc, sh, gate, fin, w1, w3, w2)


def _rot_cols(w):
    half = w.shape[-1] // 2
    return jnp.concatenate([-w[..., half:], w[..., :half]], axis=-1)


def _pad_cols(w, n):
    return jnp.pad(w, [(0, 0)] * (w.ndim - 1) + [(0, n - w.shape[-1])])


def _pad_lanes(p):
    return jnp.concatenate([p, jnp.zeros((p.shape[0], LANES - p.shape[1]), p.dtype)], axis=-1)


def _pad_rows(p):
    return jnp.concatenate([p, jnp.zeros((LANES - p.shape[0], p.shape[1]), p.dtype)], axis=0)


def _even_in_layout_kernel(w_ref, main_ref, small_ref):
    o_kr = MLA_Q_RANK + MLA_KV_RANK
    o_qkv = o_kr + MLA_ROPE
    o_ba = o_qkv + 4 * GDN_DIM
    half = MLA_ROPE // 2
    w = w_ref[...]
    main_ref[:o_kr, :] = w[:o_kr].astype(BF16)
    main_ref[o_kr:, :] = w[o_qkv:o_ba].astype(BF16)
    kr = w[o_kr:o_qkv]
    rot = jnp.concatenate([-kr[half:], kr[:half]], axis=0)
    small_ref[...] = jnp.concatenate([_pad_rows(kr), _pad_rows(rot), _pad_rows(w[o_ba:])], axis=0).astype(BF16)


def _gla_in_layout_kernel(w_ref, main_ref, small_ref):
    n = main_ref.shape[0]
    w = w_ref[...]
    main_ref[...] = w[:n].astype(BF16)
    small_ref[...] = _pad_rows(w[n:]).astype(BF16)


def _in_weight_layout(body, w_in, n_main, n_small, *, tc=256):
    w_t = jnp.swapaxes(w_in, 1, 2)
    nl, n, d = w_t.shape
    return pl.pallas_call(
        body,
        grid=(nl, d // tc),
        in_specs=[pl.BlockSpec((None, n, tc), lambda l, i: (l, 0, i))],
        out_specs=[pl.BlockSpec((None, n_main, tc), lambda l, i: (l, 0, i)),
                   pl.BlockSpec((None, n_small, tc), lambda l, i: (l, 0, i))],
        out_shape=[jax.ShapeDtypeStruct((nl, n_main, d), BF16), jax.ShapeDtypeStruct((nl, n_small, d), BF16)],
        compiler_params=_cparams("parallel", "parallel"),
        name="in_weight_layout",
    )(w_t)


def _mla_weights(w_uq, w_ukv):
    r = w_uq.shape[0]
    wq = w_uq.reshape(r, MLA_HEADS, MLA_QK)
    nope, rope = wq[..., :MLA_NOPE], wq[..., MLA_NOPE:]
    flat = lambda t: t.reshape(r, MLA_HEADS * LANES)
    wq_all = jnp.concatenate([flat(nope), flat(_pad_cols(rope, LANES)), flat(_pad_cols(_rot_cols(rope), LANES))], axis=1)
    wkv = w_ukv.reshape(w_ukv.shape[0], MLA_HEADS, MLA_NOPE + MLA_V)
    wkv_all = jnp.concatenate([flat(wkv[..., :MLA_NOPE]), flat(wkv[..., MLA_NOPE:])], axis=1)
    return wq_all.astype(BF16), wkv_all.astype(BF16)


def kernel(x, c, positions, norm_g, ada_w, ada_b, ab_w_in, mla_q_norm, mla_w_uq, mla_kv_norm, mla_w_ukv, gdn_conv_w,
           gdn_a_log, gdn_dt_bias, gdn_norm, ab_w_out, gla_w_in, gla_w_gk2, gla_b_gk2, gla_norm, gla_w_out, ffn_w1,
           ffn_w3, ffn_w2, final_norm):
    batch, s, d = x.shape
    assert batch == 1 and c.shape == (1, d)
    depth = norm_g.shape[0]
    key_dim = gla_w_gk2.shape[2]
    value_dim = gla_w_out.shape[1]

    ada_w2, ada_b2 = ada_w.reshape(depth * 2, d, 3 * d), ada_b.reshape(depth * 2, 1, 3 * d)
    mods0, silu_c = _ada_modulation(c.reshape(d, 1), ada_w2, ada_b2, 1)
    mods = {0: mods0[0]}
    xs = x.reshape(s, d)
    pos_col = positions.reshape(s, 1)
    half = MLA_ROPE // 2
    inv_freq = ROPE_THETA ** (-jnp.arange(half, dtype=F32) / half)
    invf = jnp.tile(inv_freq, LANES // half).reshape(1, LANES)
    row = lambda v: v.reshape(1, -1)

    n_gla = 2 * key_dim + 2 * value_dim
    n_even = MLA_Q_RANK + MLA_KV_RANK + 4 * GDN_DIM
    ab_main_w, ab_small_w = _in_weight_layout(_even_in_layout_kernel, ab_w_in, n_even, 3 * LANES)
    gla_main_w, gla_small_w = _in_weight_layout(_gla_in_layout_kernel, gla_w_in, n_gla, LANES)
    ab_out_w, gla_out_w = ab_w_out.astype(BF16), gla_w_out.astype(BF16)
    assert depth % 2 == 0

    for layer in range(depth):
        i = layer // 2
        m = mods[2 * layer]
        shift, scale, gate = m[:, :d], m[:, d:2 * d], m[:, 2 * d:]
        gain = row(norm_g[layer, 0])
        if layer % 2 == 0:
            main, small = _norm_proj(xs, gain, scale, shift, ab_main_w, ab_small_w, i, tm=TM_PROJ, tn=TN_PROJ)
            wq, wkv = _mla_weights(mla_w_uq[i], mla_w_ukv[i])
            q, k, v = _mla_up(main, small, pos_col, invf, row(mla_q_norm[i]), row(mla_kv_norm[i]), wq, wkv,
                              tm=TM_MLA_UP)
            first = 2 * layer + 1
            o_a, w1, w3, w2, later = _causal_attention(q, k, v, (ffn_w1, ffn_w3, ffn_w2), layer, 2,
                                                       (silu_c, ada_w2, ada_b2, first, 4), t=T_ATTN)
            mods.update({first + n: later[n] for n in range(4)})
            lane_pad = lambda vec: jnp.pad(vec, (GDN_HEADS, LANES - 2 * GDN_HEADS)).reshape(1, LANES)
            qd, kd, u, w, a, gc = _gdn_prep(main, small, gdn_conv_w[i], lane_pad(gdn_a_log[i]),
                                            lane_pad(gdn_dt_bias[i]), tm=TM_GDN_PREP)
            o_b = _gdn_scan(qd, kd, u, w, a, gc, main, row(gdn_norm[i]), tm=TM_GDN_SCAN)
            xs = _out_proj(o_a, o_b, 0, 0, ab_out_w, i, xs, gate, tm=TM_OUT)
        else:
            main, small = _norm_proj(xs, gain, scale, shift, gla_main_w, gla_small_w, i, tm=TM_PROJ,
                                     tn=TN_PROJ)
            wg = jnp.pad(gla_w_gk2[i], ((0, LANES - GLA_GATE_RANK), (0, 0)))
            o_c = _gla(main, small, wg, row(gla_b_gk2[i]), row(gla_norm[i]), key_dim=key_dim, value_dim=value_dim,
                       tm=TM_GLA)
            xs = _out_proj(o_c, o_c, 0, 1, gla_out_w, i, xs, gate, tm=TM_OUT)
        m = mods[2 * layer + 1]
        shift, scale, gate = m[:, :d], m[:, d:2 * d], m[:, 2 * d:]
        xs = _ffn(xs, row(norm_g[layer, 1]), scale, shift, gate, row(final_norm), w1, w3, w2, layer % 2,
                  tm=TM_FFN, tf=TF_FFN,
                  final_norm=(layer == depth - 1))
    return xs.reshape(batch, s, d)
```

```python
import functools
import math

import jax
import jax.numpy as jnp
from jax import lax
from jax.experimental import pallas as pl
from jax.experimental.pallas import tpu as pltpu

F32 = jnp.float32
BF16 = jnp.bfloat16

EPS = 1e-6
ROPE_THETA = 10000.0
MLA_HEADS = 8
MLA_NOPE = 128
MLA_ROPE = 64
MLA_V = 128
MLA_QK = MLA_NOPE + MLA_ROPE
MLA_QK_PAD = 256
MLA_V_PAD = MLA_V + 16
MLA_Q_RANK = 512
MLA_KV_RANK = 512
GDN_HEADS = 8
GDN_HEAD_DIM = 128
GDN_DIM = GDN_HEADS * GDN_HEAD_DIM
GDN_CONV = 4
GLA_HEADS = 4
GLA_GATE_RANK = 16
GLA_GATE_NORMALIZER = 16.0
GLA_SUB = 16
GLA_FACTOR_RANGE = 60.0
CHUNK = 64
LANES = 128
VMEM_LIMIT = 48 * 1024 * 1024

TM_PROJ, TN_PROJ = 1024, 1024
TM_MLA_UP = 256
T_ATTN = 1024
TM_GDN_PREP = 256
TM_GDN_SCAN = 512
TM_GLA = 256
TM_OUT = 512
TM_FFN, TF_FFN = 512, 512


def _cparams(*sem):
    return pltpu.CompilerParams(dimension_semantics=sem, vmem_limit_bytes=VMEM_LIMIT)


def _dot(a, b):
    return jnp.dot(a, b, preferred_element_type=F32)


def _dot_nt(a, b):
    return lax.dot_general(a, b, (((1,), (1,)), ((), ())), preferred_element_type=F32)


def _dot_tn(a, b):
    return lax.dot_general(a, b, (((0,), (0,)), ((), ())), preferred_element_type=F32)


def _split3(x):
    x1 = x.astype(BF16)
    r1 = x - x1.astype(F32)
    x2 = r1.astype(BF16)
    x3 = (r1 - x2.astype(F32)).astype(BF16)
    return x1, x2, x3


def _cumsum_rows(tri, x):
    x1, x2, x3 = _split3(x)
    return _dot(tri, x1) + _dot(tri, x2) + _dot(tri, x3)


def _silu(x):
    return x * jax.nn.sigmoid(x)


def _softplus(x):
    return jnp.maximum(x, 0.0) + jnp.log1p(jnp.exp(-jnp.abs(x)))


def _rms(x):
    return x * lax.rsqrt(jnp.mean(x * x, axis=-1, keepdims=True) + EPS)


def _ada_kernel(c_ref, w_ref, b_ref, o_ref, s_ref):
    s = _silu(c_ref[...])
    o_ref[0] = jnp.sum(w_ref[0] * s, axis=0, keepdims=True) + b_ref[0]
    s_ref[...] = jnp.broadcast_to(s, s_ref.shape)


def _ada_modulation(c_col, w, b, n_mod, *, tn=768):
    _, d, n = w.shape
    return pl.pallas_call(
        _ada_kernel,
        grid=(n_mod, n // tn),
        in_specs=[
            pl.BlockSpec((d, 1), lambda l, j: (0, 0)),
            pl.BlockSpec((1, d, tn), lambda l, j: (l, 0, j)),
            pl.BlockSpec((1, 1, tn), lambda l, j: (l, 0, j)),
        ],
        out_specs=[pl.BlockSpec((1, 1, tn), lambda l, j: (l, 0, j)), pl.BlockSpec((d, LANES), lambda l, j: (0, 0))],
        out_shape=[jax.ShapeDtypeStruct((n_mod, 1, n), F32), jax.ShapeDtypeStruct((d, LANES), F32)],
        compiler_params=_cparams("arbitrary", "arbitrary"),
        name="ada_modulation",
    )(c_col, w, b)


def _modulated_norm(x, g, sc, sh):
    return _rms(x) * (g * (1.0 + sc)) + sh


def _norm_proj_kernel(x_ref, g_ref, sc_ref, sh_ref, w_ref, ws_ref, o_ref, os_ref, h_ref):
    @pl.when(pl.program_id(1) == 0)
    def _():
        h = _modulated_norm(x_ref[...], g_ref[...], sc_ref[...], sh_ref[...]).astype(BF16)
        h_ref[...] = h
        os_ref[...] = _dot_nt(h, ws_ref[...])

    o_ref[...] = _dot_nt(h_ref[...], w_ref[...]).astype(o_ref.dtype)


def _norm_proj(x, g, sc, sh, w, ws, layer, *, tm, tn):
    s, d = x.shape
    n, ns = w.shape[1], ws.shape[1]
    assert s % tm == 0 and n % tn == 0
    vec = pl.BlockSpec((1, d), lambda i, j: (0, 0))
    return pl.pallas_call(
        _norm_proj_kernel,
        grid=(s // tm, n // tn),
        in_specs=[
            pl.BlockSpec((tm, d), lambda i, j: (i, 0)),
            vec, vec, vec,
            pl.BlockSpec((None, tn, d), lambda i, j: (layer, j, 0)),
            pl.BlockSpec((None, ns, d), lambda i, j: (layer, 0, 0)),
        ],
        out_specs=[
            pl.BlockSpec((tm, tn), lambda i, j: (i, j)),
            pl.BlockSpec((tm, ns), lambda i, j: (i, 0)),
        ],
        out_shape=[jax.ShapeDtypeStruct((s, n), BF16), jax.ShapeDtypeStruct((s, ns), F32)],
        scratch_shapes=[pltpu.VMEM((tm, d), BF16)],
        compiler_params=_cparams("parallel", "arbitrary"),
        name="norm_proj",
    )(x, g, sc, sh, w, ws)


def _mla_up_kernel(cq_ref, ckv_ref, kr_ref, krr_ref, pos_ref, invf_ref, qn_ref, kvn_ref, wq_ref, wkv_ref,
                   q_out, k_out, v_out):
    hd = MLA_HEADS * LANES
    cqn = (_rms(cq_ref[...].astype(F32)) * qn_ref[...]).astype(BF16)
    ckvn = (_rms(ckv_ref[...].astype(F32)) * kvn_ref[...]).astype(BF16)
    qall = _dot(cqn, wq_ref[...])
    kv = _dot(ckvn, wkv_ref[...])
    ang = pos_ref[...].astype(F32) * invf_ref[...]
    cos, sin = jnp.cos(ang), jnp.sin(ang)
    k_rope = kr_ref[...] * cos + krr_ref[...] * sin
    scale = math.log2(math.e) / math.sqrt(MLA_QK)
    ones = jnp.ones((MLA_V_PAD - MLA_V, cq_ref.shape[0]), F32)
    for h in range(MLA_HEADS):
        lo, hi = h * LANES, (h + 1) * LANES
        q_rope = qall[:, hd + lo:hd + hi] * cos + qall[:, 2 * hd + lo:2 * hd + hi] * sin
        q_out[h] = (jnp.concatenate([qall[:, lo:hi], q_rope], axis=-1) * scale).astype(BF16)
        k_out[h] = jnp.concatenate([kv[:, lo:hi], k_rope], axis=-1).astype(BF16)
        v_out[h] = jnp.concatenate([kv[:, hd + lo:hd + hi].T, ones], axis=0).astype(BF16)


def _mla_up(main, small, pos_col, invf, qn, kvn, wq, wkv, *, tm):
    s = main.shape[0]
    full = lambda a: pl.BlockSpec(a.shape, lambda i: (0,) * a.ndim)
    qk_shape = jax.ShapeDtypeStruct((MLA_HEADS, s, MLA_QK_PAD), BF16)
    return pl.pallas_call(
        _mla_up_kernel,
        grid=(s // tm,),
        in_specs=[
            pl.BlockSpec((tm, MLA_Q_RANK), lambda i: (i, 0)),
            pl.BlockSpec((tm, MLA_KV_RANK), lambda i: (i, 1)),
            pl.BlockSpec((tm, LANES), lambda i: (i, 0)),
            pl.BlockSpec((tm, LANES), lambda i: (i, 1)),
            pl.BlockSpec((tm, 1), lambda i: (i, 0)),
            full(invf), full(qn), full(kvn), full(wq), full(wkv),
        ],
        out_specs=[
            pl.BlockSpec((MLA_HEADS, tm, MLA_QK_PAD), lambda i: (0, i, 0)),
            pl.BlockSpec((MLA_HEADS, tm, MLA_QK_PAD), lambda i: (0, i, 0)),
            pl.BlockSpec((MLA_HEADS, MLA_V_PAD, tm), lambda i: (0, 0, i)),
        ],
        out_shape=[qk_shape, qk_shape, jax.ShapeDtypeStruct((MLA_HEADS, MLA_V_PAD, s), BF16)],
        compiler_params=_cparams("parallel"),
        name="mla_up",
    )(main, main, small, small, pos_col, invf, qn, kvn, wq, wkv)


def _attn_kernel(q_ref, k_ref, v_ref, *rest, t, n_side):
    side_in, (sc_ref, aw_ref, ab_ref) = rest[:n_side], rest[n_side:n_side + 3]
    (o_ref, *side_out), mods_ref = rest[n_side + 3:2 * n_side + 4], rest[2 * n_side + 4]
    s0_ref, s1_ref, m_ref, acc_ref = rest[2 * n_side + 5:]
    for src, dst in zip(side_in, side_out):
        dst[...] = src[...].astype(dst.dtype)
    silu_c = jnp.concatenate([sc_ref[...]] * (aw_ref.shape[1] // LANES), axis=1)
    mods_ref[...] = jnp.sum(aw_ref[...] * silu_c, axis=0, keepdims=True) + ab_ref[...]
    i = pl.program_id(1)
    q = q_ref[0]
    dv = MLA_V

    def tile(j):
        return pl.ds(pl.multiple_of(j * t, t), t)

    def scores(s_ref, j):
        s_ref[...] = _dot_nt(k_ref[0, tile(j), :], q)

    def update(s_ref, j, diagonal):
        s = s_ref[...]
        if diagonal:
            key = lax.broadcasted_iota(jnp.int32, (t, t), 0)
            qry = lax.broadcasted_iota(jnp.int32, (t, t), 1)
            s = jnp.where(key <= qry, s, -jnp.inf)
        m = m_ref[...]
        m_new = jnp.maximum(m, jnp.max(s, axis=0, keepdims=True))
        p = jnp.exp2(s - m_new)
        alpha = jnp.exp2(m - m_new)
        acc_ref[...] = alpha * acc_ref[...] + _dot(v_ref[0, :, tile(j)], p.astype(BF16))
        m_ref[...] = m_new

    m_ref[...] = jnp.full(m_ref.shape, -jnp.inf, F32)
    acc_ref[...] = jnp.zeros(acc_ref.shape, F32)
    scores(s0_ref, 0)

    def pair(jj, carry):
        j = 2 * jj
        scores(s1_ref, j + 1)
        update(s0_ref, j, False)
        scores(s0_ref, j + 2)
        update(s1_ref, j + 1, False)
        return carry

    lax.fori_loop(0, i // 2, pair, 0)

    @pl.when(i % 2 == 0)
    def _():
        update(s0_ref, i, True)

    @pl.when(i % 2 == 1)
    def _():
        scores(s1_ref, i)
        update(s0_ref, i - 1, False)
        update(s1_ref, i, True)

    acc = acc_ref[...]
    o_ref[...] = (acc[:dv] / acc[dv:dv + 1]).T.astype(o_ref.dtype)


def _causal_attention(q, k, v, side, n_layers, ada, *, t):
    h, s, dqk = q.shape
    dvp = v.shape[1]
    nq = s // t
    steps = h * nq
    assert s % t == 0 and steps % n_layers == 0
    step = lambda hh, i: hh * nq + i
    per_layer = steps // n_layers
    side_in, side_out, side_shapes = [], [], []
    for w, l0 in side:
        rows = w.shape[1] // per_layer
        assert rows * per_layer == w.shape[1] and rows % 16 == 0
        blk = (None, rows) + w.shape[2:]
        side_in.append(pl.BlockSpec(
            blk, lambda hh, i, l0=l0: (l0 + step(hh, i) // per_layer, step(hh, i) % per_layer, 0)))
        side_out.append(pl.BlockSpec(blk, lambda hh, i: (step(hh, i) // per_layer, step(hh, i) % per_layer, 0)))
        side_shapes.append(jax.ShapeDtypeStruct((n_layers,) + w.shape[1:], BF16))
    silu_c, ada_w, ada_b, first, count = ada
    n_mod, d, n3 = ada_w.shape
    per_mod = steps // count
    slab = n3 // per_mod
    assert per_mod * count == steps and slab * per_mod == n3 and slab % LANES == 0
    mod_idx = lambda hh, i: jnp.minimum(first + step(hh, i) // per_mod, n_mod - 1)
    ada_in = [
        pl.BlockSpec((d, LANES), lambda hh, i: (0, 0)),
        pl.BlockSpec((None, d, slab), lambda hh, i: (mod_idx(hh, i), 0, step(hh, i) % per_mod)),
        pl.BlockSpec((None, 1, slab), lambda hh, i: (mod_idx(hh, i), 0, step(hh, i) % per_mod)),
    ]
    mods_out = pl.BlockSpec((None, 1, slab), lambda hh, i: (step(hh, i) // per_mod, 0, step(hh, i) % per_mod))
    return pl.pallas_call(
        functools.partial(_attn_kernel, t=t, n_side=len(side)),
        grid=(h, nq),
        in_specs=[
            pl.BlockSpec((1, t, dqk), lambda hh, i: (hh, i, 0)),
            pl.BlockSpec((1, s, dqk), lambda hh, i: (hh, 0, 0)),
            pl.BlockSpec((1, dvp, s), lambda hh, i: (hh, 0, 0)),
        ] + side_in + ada_in,
        out_specs=[pl.BlockSpec((t, MLA_V), lambda hh, i: (i, hh))] + side_out + [mods_out],
        out_shape=[jax.ShapeDtypeStruct((s, h * MLA_V), BF16)] + side_shapes
        + [jax.ShapeDtypeStruct((count, 1, n3), F32)],
        scratch_shapes=[pltpu.VMEM((t, t), F32), pltpu.VMEM((t, t), F32), pltpu.VMEM((1, t), F32),
                        pltpu.VMEM((dvp, t), F32)],
        compiler_params=_cparams("arbitrary", "arbitrary"),
        name="mla_attention",
    )(q, k, v, *[w for w, _ in side], silu_c, ada_w, ada_b)


def _bdot(a, b):
    return lax.dot_general(a, b, (((2,), (1,)), ((0,), (0,))), preferred_element_type=F32)


def _bdot_nt(a, b):
    return lax.dot_general(a, b, (((2,), (2,)), ((0,), (0,))), preferred_element_type=F32)


def _bdot_tn(a, b):
    return lax.dot_general(a, b, (((1,), (1,)), ((0,), (0,))), preferred_element_type=F32)


def _unit_lower_inverse(low, eye, blk=16):
    n = low.shape[-1]
    bdot = lambda a, b: _bdot(a.astype(BF16), b.astype(BF16))
    rb = lax.broadcasted_iota(jnp.int32, (n, n), 0) // blk
    cb = lax.broadcasted_iota(jnp.int32, (n, n), 1) // blk
    diag = jnp.where(rb == cb, low, 0.0)
    p = diag
    d_inv = eye - diag
    for _ in range(int(math.log2(blk)) - 1):
        p = bdot(p, p)
        d_inv = d_inv + bdot(d_inv, p)
    m = bdot(d_inv, low - diag)
    assert n // blk == 4
    i_m = eye - m
    return bdot(i_m + bdot(i_m, bdot(m, m)), d_inv)


def _gdn_prep_kernel(q_ref, k_ref, v_ref, qh_ref, kh_ref, vh_ref, sm_ref, cw_ref, alog_ref, dtb_ref,
                     qd_out, kd_out, u_out, w_out, a_out, gc_out):
    c, d, nh = CHUNK, GDN_HEAD_DIM, GDN_HEADS
    tm = q_ref.shape[0]
    nc = tm // c
    halo = qh_ref.shape[0]

    first = pl.program_id(0) == 0
    n_sh = GDN_CONV - 1
    r = lax.broadcasted_iota(jnp.int32, (n_sh * tm, tm), 0)
    src = r % tm - (n_sh - r // tm)
    sel = jnp.where(lax.broadcasted_iota(jnp.int32, (n_sh * tm, tm), 1) == src, 1.0, 0.0).astype(BF16)
    rh = lax.broadcasted_iota(jnp.int32, (n_sh * tm, halo), 0)
    src_h = rh % tm - (n_sh - rh // tm) + halo
    sel_h = jnp.where(lax.broadcasted_iota(jnp.int32, (n_sh * tm, halo), 1) == src_h, 1.0, 0.0).astype(BF16)
    parts = []
    for n, (cur, prev) in enumerate(((q_ref, qh_ref), (k_ref, kh_ref), (v_ref, vh_ref))):
        cw = cw_ref[:, n * GDN_DIM:(n + 1) * GDN_DIM]
        x = cur[...]
        prev_rows = jnp.where(first, jnp.zeros_like(prev[...]), prev[...])
        sh = _dot(sel, x) + _dot(sel_h, prev_rows)
        yn = x.astype(F32) * cw[n_sh:n_sh + 1]
        for j in range(n_sh):
            yn = yn + sh[j * tm:(j + 1) * tm] * cw[j:j + 1]
        parts.append(yn)
    y = _silu(jnp.concatenate(parts, axis=1))

    sm = sm_ref[...]
    beta_all = jax.nn.sigmoid(sm)
    g_all = -jnp.exp(alog_ref[...]) * _softplus(sm + dtb_ref[...])
    row = lax.broadcasted_iota(jnp.int32, (tm, tm), 0)
    col = lax.broadcasted_iota(jnp.int32, (tm, tm), 1)
    tri = jnp.where((col <= row) & (row // c == col // c), 1.0, 0.0).astype(BF16)
    gc_all = _cumsum_rows(tri, g_all)
    gc_out[...] = gc_all
    zpad = jnp.zeros((LANES - c, LANES), F32)
    gc_t = [jnp.concatenate([gc_all[ci * c:(ci + 1) * c], zpad], axis=0).T for ci in range(nc)]

    def heads(arr, off):
        return jnp.stack([arr[ci * c:(ci + 1) * c, off + h * d:off + (h + 1) * d]
                          for ci in range(nc) for h in range(nh)])

    def cols(arr, off):
        return jnp.stack([arr[ci * c:(ci + 1) * c, off + h:off + h + 1] for ci in range(nc) for h in range(nh)])

    q = heads(y, 0)
    k = heads(y, GDN_DIM)
    v = heads(y, 2 * GDN_DIM)
    q = q * lax.rsqrt(jnp.sum(q * q, axis=-1, keepdims=True) + EPS) * (1.0 / math.sqrt(d))
    k = k * lax.rsqrt(jnp.sum(k * k, axis=-1, keepdims=True) + EPS)
    beta = cols(beta_all, 0)
    gc = cols(gc_all, nh)
    gcr = jnp.stack([gc_t[ci][nh + h:nh + h + 1, 0:c] for ci in range(nc) for h in range(nh)])
    r2 = lax.broadcasted_iota(jnp.int32, (c, c), 0)
    c2 = lax.broadcasted_iota(jnp.int32, (c, c), 1)
    decay = jnp.exp(jnp.where(c2 <= r2, gc - gcr, -jnp.inf))
    eye = jnp.where(c2 == r2, 1.0, 0.0).astype(F32)
    kbeta = k * beta
    kb16 = k.astype(BF16)
    low = jnp.where(c2 < r2, _bdot_nt(kbeta.astype(BF16), kb16) * decay, 0.0)
    t_inv = _unit_lower_inverse(low, eye)
    egc = jnp.exp(gc)
    rhs = jnp.concatenate([v * beta, kbeta * egc], axis=-1)
    sol = _bdot(t_inv.astype(BF16), rhs.astype(BF16))
    intra = _bdot_nt(q.astype(BF16), kb16) * decay
    g_last = gcr[:, :, c - 1:c]
    k_dec = k * jnp.exp(g_last - gc)
    q_dec = q * egc
    for ci in range(nc):
        rows = slice(ci * c, (ci + 1) * c)
        for h in range(nh):
            b = ci * nh + h
            lanes = slice(h * d, (h + 1) * d)
            qd_out[rows, lanes] = q_dec[b].astype(BF16)
            kd_out[rows, lanes] = k_dec[b].astype(BF16)
            u_out[rows, lanes] = sol[b, :, :d]
            w_out[rows, lanes] = sol[b, :, d:].astype(BF16)
            a_out[h, rows, :] = intra[b].astype(BF16)


def _gdn_prep(main, small, conv_w, alog, dtb, *, tm, halo=8):
    s = main.shape[0]
    full = lambda a: pl.BlockSpec(a.shape, lambda i: (0,) * a.ndim)
    blk = lambda jcol: pl.BlockSpec((tm, GDN_DIM), lambda i: (i, jcol))
    prev = lambda jcol: pl.BlockSpec((halo, GDN_DIM), lambda i: (jnp.maximum(i * (tm // halo) - 1, 0), jcol))
    act = jax.ShapeDtypeStruct((s, GDN_DIM), BF16)
    return pl.pallas_call(
        _gdn_prep_kernel,
        grid=(s // tm,),
        in_specs=[blk(1), blk(2), blk(3), prev(1), prev(2), prev(3), pl.BlockSpec((tm, LANES), lambda i: (i, 2)),
                  full(conv_w), full(alog), full(dtb)],
        out_specs=[blk(0), blk(0), blk(0), blk(0),
                   pl.BlockSpec((GDN_HEADS, tm, CHUNK), lambda i: (0, i, 0)),
                   pl.BlockSpec((tm, LANES), lambda i: (i, 0))],
        out_shape=[act, act, jax.ShapeDtypeStruct((s, GDN_DIM), F32), act,
                   jax.ShapeDtypeStruct((GDN_HEADS, s, CHUNK), BF16), jax.ShapeDtypeStruct((s, LANES), F32)],
        compiler_params=_cparams("parallel"),
        name="gdn_prep",
    )(main, main, main, main, main, main, small, conv_w, alog, dtb)


def _gdn_scan_kernel(qd_ref, kd_ref, u_ref, w_ref, a_ref, gc_ref, z_ref, gn_ref, o_ref, state_ref):
    c, d, nh = CHUNK, GDN_HEAD_DIM, GDN_HEADS

    @pl.when(pl.program_id(0) == 0)
    def _():
        state_ref[...] = jnp.zeros(state_ref.shape, F32)

    def heads(ref, rows):
        return jnp.stack([ref[rows, h * d:(h + 1) * d] for h in range(nh)])

    for ci in range(qd_ref.shape[0] // c):
        rows = slice(ci * c, (ci + 1) * c)
        last = ci * c + c - 1
        g_last = jnp.stack([gc_ref[last:last + 1, nh + h:nh + h + 1] for h in range(nh)])
        st = state_ref[...]
        stb = st.astype(BF16)
        v_new = heads(u_ref, rows) - _bdot(heads(w_ref, rows), stb)
        v_newb = v_new.astype(BF16)
        o = _bdot(heads(qd_ref, rows), stb) + _bdot(a_ref[:, rows, :], v_newb)
        state_ref[...] = st * jnp.exp(g_last) + _bdot_tn(heads(kd_ref, rows), v_newb)
        o = _rms(o) * gn_ref[...]
        for h in range(nh):
            lanes = slice(h * d, (h + 1) * d)
            o_ref[rows, lanes] = (o[h] * _silu(z_ref[rows, lanes].astype(F32))).astype(o_ref.dtype)


def _gdn_scan(qd, kd, u, w, a, gc, main, gnorm, *, tm):
    s = qd.shape[0]
    blk = lambda jcol: pl.BlockSpec((tm, GDN_DIM), lambda i: (i, jcol))
    return pl.pallas_call(
        _gdn_scan_kernel,
        grid=(s // tm,),
        in_specs=[blk(0), blk(0), blk(0), blk(0),
                  pl.BlockSpec((GDN_HEADS, tm, CHUNK), lambda i: (0, i, 0)),
                  pl.BlockSpec((tm, LANES), lambda i: (i, 0)),
                  blk(4), pl.BlockSpec(gnorm.shape, lambda i: (0, 0))],
        out_specs=blk(0),
        out_shape=jax.ShapeDtypeStruct((s, GDN_DIM), BF16),
        scratch_shapes=[pltpu.VMEM((GDN_HEADS, GDN_HEAD_DIM, GDN_HEAD_DIM), F32)],
        compiler_params=_cparams("arbitrary"),
        name="gdn_scan",
    )(qd, kd, u, w, a, gc, main, gnorm)


def _gla_kernel(q_ref, k_ref, v_ref, r_ref, sm_ref, wg1_ref, wg2_ref, bg_ref, gn_ref, o_ref, state_ref):
    c = CHUNK
    dk = q_ref.shape[1] // GLA_HEADS
    dv = v_ref.shape[1] // GLA_HEADS

    @pl.when(pl.program_id(0) == 0)
    def _():
        state_ref[...] = jnp.zeros(state_ref.shape, F32)

    a1, a2, _ = _split3(sm_ref[...])
    w1, w2 = wg1_ref[...], wg2_ref[...]
    pre = _dot(a1, w1) + _dot(a1, w2) + _dot(a2, w1) + bg_ref[...]
    gk = (jnp.minimum(pre, 0.0) - jnp.log1p(jnp.exp(-jnp.abs(pre)))) * (1.0 / GLA_GATE_NORMALIZER)
    tm = q_ref.shape[0]
    trow = lax.broadcasted_iota(jnp.int32, (tm, tm), 0)
    tcol = lax.broadcasted_iota(jnp.int32, (tm, tm), 1)
    tri = jnp.where((tcol <= trow) & (trow // c == tcol // c), 1.0, 0.0).astype(BF16)
    b_tile = _cumsum_rows(tri, gk)
    row = lax.broadcasted_iota(jnp.int32, (c, c), 0)
    col = lax.broadcasted_iota(jnp.int32, (c, c), 1)
    sub_row = lax.broadcasted_iota(jnp.int32, (GLA_SUB, 1), 0)
    sub_col = lax.broadcasted_iota(jnp.int32, (GLA_SUB, GLA_SUB), 1)

    def chunk_factored(rows, b_all):
        heads = lambda ref, w: jnp.stack([ref[rows, h * w:(h + 1) * w] for h in range(GLA_HEADS)])
        bh = jnp.stack([b_all[:, h * dk:(h + 1) * dk] for h in range(GLA_HEADS)])
        q = heads(q_ref, dk).astype(F32) * (1.0 / math.sqrt(dk))
        k = heads(k_ref, dk).astype(F32)
        v = heads(v_ref, dv)
        b0, b_last = bh[:, 0:1], bh[:, c - 1:c]
        q_t = q * jnp.exp(bh - b0)
        k_t = k * jnp.exp(b0 - bh)
        a = jnp.where(col <= row, _bdot_nt(q_t.astype(BF16), k_t.astype(BF16)), 0.0)
        q_dec = q_t * jnp.exp(b0)
        k_dec = k_t * jnp.exp(b_last - b0)
        st = state_ref[...]
        o = _bdot_nt(q_dec.astype(BF16), st.astype(BF16)) + _bdot(a.astype(BF16), v)
        state_ref[...] = st * jnp.exp(b_last) + _bdot_tn(v, k_dec.astype(BF16))
        o = _rms(o) * gn_ref[...]
        for h in range(GLA_HEADS):
            lanes = slice(h * dv, (h + 1) * dv)
            o_ref[rows, lanes] = (o[h] * _silu(r_ref[rows, lanes].astype(F32))).astype(o_ref.dtype)

    def intra_exact(qh, kh, bh, vh):
        o_rows = []
        for blk in range(c // GLA_SUB):
            r0 = blk * GLA_SUB
            qs, bs = qh[r0:r0 + GLA_SUB], bh[r0:r0 + GLA_SUB]
            ks = kh[r0:r0 + GLA_SUB]
            a_diag = jnp.zeros((GLA_SUB, GLA_SUB), F32)
            for j in range(GLA_SUB):
                e = jnp.exp(jnp.where(sub_row >= j, bs - bs[j:j + 1], -jnp.inf))
                dj = jnp.sum(qs * ks[j:j + 1] * e, axis=-1, keepdims=True)
                a_diag = jnp.where(sub_col == j, dj, a_diag)
            o_blk = _dot(a_diag.astype(BF16), vh[r0:r0 + GLA_SUB])
            if blk > 0:
                b_ref0 = bs[0:1]
                q_t = qs * jnp.exp(bs - b_ref0)
                k_t = kh[0:r0] * jnp.exp(b_ref0 - bh[0:r0])
                a_off = _dot_nt(q_t.astype(BF16), k_t.astype(BF16))
                o_blk = o_blk + _dot(a_off.astype(BF16), vh[0:r0])
            o_rows.append(o_blk)
        return jnp.concatenate(o_rows, axis=0), qh * jnp.exp(bh), kh * jnp.exp(bh[c - 1:c] - bh)

    def chunk_exact(rows, b_all):
        for h in range(GLA_HEADS):
            bh = b_all[:, h * dk:(h + 1) * dk]
            qh = q_ref[rows, h * dk:(h + 1) * dk].astype(F32) * (1.0 / math.sqrt(dk))
            kh = k_ref[rows, h * dk:(h + 1) * dk].astype(F32)
            vh = v_ref[rows, h * dv:(h + 1) * dv]
            st = state_ref[h]
            o_intra, q_dec, k_dec = intra_exact(qh, kh, bh, vh)
            o = _dot_nt(q_dec.astype(BF16), st.astype(BF16)) + o_intra
            state_ref[h] = st * jnp.exp(bh[c - 1:c]) + _dot_tn(vh, k_dec.astype(BF16))
            o = _rms(o) * gn_ref[...] * _silu(r_ref[rows, h * dv:(h + 1) * dv].astype(F32))
            o_ref[rows, h * dv:(h + 1) * dv] = o.astype(o_ref.dtype)

    for ci in range(tm // c):
        rows = slice(ci * c, (ci + 1) * c)
        b_all = b_tile[rows]
        decay_range = jnp.max(b_all[0:1, :] - b_all[c - 1:c, :])
        factorable = decay_range <= GLA_FACTOR_RANGE
        pl.when(factorable)(functools.partial(chunk_factored, rows, b_all))
        pl.when(jnp.logical_not(factorable))(functools.partial(chunk_exact, rows, b_all))


def _gla(main, small, wg, bg, gnorm, *, key_dim, value_dim, tm):
    s = main.shape[0]
    full = lambda a: pl.BlockSpec(a.shape, lambda i: (0,) * a.ndim)
    wg1 = wg.astype(BF16)
    wg2 = (wg - wg1.astype(F32)).astype(BF16)
    return pl.pallas_call(
        _gla_kernel,
        grid=(s // tm,),
        in_specs=[
            pl.BlockSpec((tm, key_dim), lambda i: (i, 0)),
            pl.BlockSpec((tm, key_dim), lambda i: (i, 1)),
            pl.BlockSpec((tm, value_dim), lambda i: (i, 1)),
            pl.BlockSpec((tm, value_dim), lambda i: (i, 2)),
            pl.BlockSpec((tm, LANES), lambda i: (i, 0)),
            full(wg1), full(wg2), full(bg), full(gnorm),
        ],
        out_specs=pl.BlockSpec((tm, value_dim), lambda i: (i, 0)),
        out_shape=jax.ShapeDtypeStruct((s, value_dim), BF16),
        scratch_shapes=[pltpu.VMEM((GLA_HEADS, value_dim // GLA_HEADS, key_dim // GLA_HEADS), F32)],
        compiler_params=_cparams("arbitrary"),
        name="gated_linear_attention",
    )(main, main, main, main, small, wg1, wg2, bg, gnorm)


def _out_proj_kernel(a1_ref, a2_ref, w_ref, x_ref, gate_ref, o_ref):
    half = a1_ref.shape[1]
    mix = _dot(a1_ref[...], w_ref[0:half, :]) + _dot(a2_ref[...], w_ref[half:, :])
    o_ref[...] = x_ref[...] + gate_ref[...] * mix


def _out_proj(a1, a2, col1, col2, w, layer, x, gate, *, tm):
    s, d = x.shape
    half = w.shape[1] // 2
    return pl.pallas_call(
        _out_proj_kernel,
        grid=(s // tm,),
        in_specs=[
            pl.BlockSpec((tm, half), lambda i: (i, col1)),
            pl.BlockSpec((tm, half), lambda i: (i, col2)),
            pl.BlockSpec((None,) + w.shape[1:], lambda i: (layer, 0, 0)),
            pl.BlockSpec((tm, d), lambda i: (i, 0)),
            pl.BlockSpec((1, d), lambda i: (0, 0)),
        ],
        out_specs=pl.BlockSpec((tm, d), lambda i: (i, 0)),
        out_shape=jax.ShapeDtypeStruct((s, d), F32),
        compiler_params=_cparams("parallel"),
        name="out_proj_residual",
    )(a1, a2, w, x, gate)


def _ffn_kernel(x_ref, g_ref, sc_ref, sh_ref, gate_ref, fin_ref, w1_ref, w3_ref, w2_ref, o_ref, h_ref, acc_ref,
                *, final_norm):
    j = pl.program_id(1)

    @pl.when(j == 0)
    def _():
        h_ref[...] = _modulated_norm(x_ref[...], g_ref[...], sc_ref[...], sh_ref[...]).astype(BF16)
        acc_ref[...] = jnp.zeros(acc_ref.shape, F32)

    h = h_ref[...]
    a = _dot(h, w1_ref[...])
    t = (_silu(a) * _dot(h, w3_ref[...])).astype(BF16)
    acc_ref[...] += _dot(t, w2_ref[...])

    @pl.when(j == pl.num_programs(1) - 1)
    def _():
        y = x_ref[...] + gate_ref[...] * acc_ref[...]
        if final_norm:
            y = _rms(y) * fin_ref[...]
        o_ref[...] = y


def _ffn(x, g, sc, sh, gate, fin, w1, w3, w2, layer, *, tm, tf, final_norm):
    s, d = x.shape
    dff = w1.shape[2]
    vec = pl.BlockSpec((1, d), lambda i, j: (0, 0))
    return pl.pallas_call(
        functools.partial(_ffn_kernel, final_norm=final_norm),
        grid=(s // tm, dff // tf),
        in_specs=[
            pl.BlockSpec((tm, d), lambda i, j: (i, 0)),
            vec, vec, vec, vec, vec,
            pl.BlockSpec((None, d, tf), lambda i, j: (layer, 0, j)),
            pl.BlockSpec((None, d, tf), lambda i, j: (layer, 0, j)),
            pl.BlockSpec((None, tf, d), lambda i, j: (layer, j, 0)),
        ],
        out_specs=pl.BlockSpec((tm, d), lambda i, j: (i, 0)),
        out_shape=jax.ShapeDtypeStruct((s, d), F32),
        scratch_shapes=[pltpu.VMEM((tm, d), BF16), pltpu.VMEM((tm, d), F32)],
        compiler_params=_cparams("parallel", "arbitrary"),
        name="swiglu_ffn",
    )(x, g, sc, sh, gate, fin, w1, w3, w2)


def _rot_cols(w):
    half = w.shape[-1] // 2
    return jnp.concatenate([-w[..., half:], w[..., :half]], axis=-1)


def _pad_cols(w, n):
    return jnp.pad(w, [(0, 0)] * (w.ndim - 1) + [(0, n - w.shape[-1])])


def _pad_lanes(p):
    return jnp.concatenate([p, jnp.zeros((p.shape[0], LANES - p.shape[1]), p.dtype)], axis=-1)


def _pad_rows(p):
    return jnp.concatenate([p, jnp.zeros((LANES - p.shape[0], p.shape[1]), p.dtype)], axis=0)


def _even_in_layout_kernel(w_ref, main_ref, small_ref):
    o_kr = MLA_Q_RANK + MLA_KV_RANK
    o_qkv = o_kr + MLA_ROPE
    o_ba = o_qkv + 4 * GDN_DIM
    half = MLA_ROPE // 2
    w = w_ref[...]
    main_ref[:o_kr, :] = w[:o_kr].astype(BF16)
    main_ref[o_kr:, :] = w[o_qkv:o_ba].astype(BF16)
    kr = w[o_kr:o_qkv]
    rot = jnp.concatenate([-kr[half:], kr[:half]], axis=0)
    small_ref[...] = jnp.concatenate([_pad_rows(kr), _pad_rows(rot), _pad_rows(w[o_ba:])], axis=0).astype(BF16)


def _gla_in_layout_kernel(w_ref, main_ref, small_ref):
    n = main_ref.shape[0]
    w = w_ref[...]
    main_ref[...] = w[:n].astype(BF16)
    small_ref[...] = _pad_rows(w[n:]).astype(BF16)


def _in_weight_layout(body, w_in, n_main, n_small, *, tc=256):
    w_t = jnp.swapaxes(w_in, 1, 2)
    nl, n, d = w_t.shape
    return pl.pallas_call(
        body,
        grid=(nl, d // tc),
        in_specs=[pl.BlockSpec((None, n, tc), lambda l, i: (l, 0, i))],
        out_specs=[pl.BlockSpec((None, n_main, tc), lambda l, i: (l, 0, i)),
                   pl.BlockSpec((None, n_small, tc), lambda l, i: (l, 0, i))],
        out_shape=[jax.ShapeDtypeStruct((nl, n_main, d), BF16), jax.ShapeDtypeStruct((nl, n_small, d), BF16)],
        compiler_params=_cparams("parallel", "parallel"),
        name="in_weight_layout",
    )(w_t)


def _mla_weights(w_uq, w_ukv):
    r = w_uq.shape[0]
    wq = w_uq.reshape(r, MLA_HEADS, MLA_QK)
    nope, rope = wq[..., :MLA_NOPE], wq[..., MLA_NOPE:]
    flat = lambda t: t.reshape(r, MLA_HEADS * LANES)
    wq_all = jnp.concatenate([flat(nope), flat(_pad_cols(rope, LANES)), flat(_pad_cols(_rot_cols(rope), LANES))], axis=1)
    wkv = w_ukv.reshape(w_ukv.shape[0], MLA_HEADS, MLA_NOPE + MLA_V)
    wkv_all = jnp.concatenate([flat(wkv[..., :MLA_NOPE]), flat(wkv[..., MLA_NOPE:])], axis=1)
    return wq_all.astype(BF16), wkv_all.astype(BF16)


def kernel(x, c, positions, norm_g, ada_w, ada_b, ab_w_in, mla_q_norm, mla_w_uq, mla_kv_norm, mla_w_ukv, gdn_conv_w,
           gdn_a_log, gdn_dt_bias, gdn_norm, ab_w_out, gla_w_in, gla_w_gk2, gla_b_gk2, gla_norm, gla_w_out, ffn_w1,
           ffn_w3, ffn_w2, final_norm):
    batch, s, d = x.shape
    assert batch == 1 and c.shape == (1, d)
    depth = norm_g.shape[0]
    key_dim = gla_w_gk2.shape[2]
    value_dim = gla_w_out.shape[1]

    ada_w2, ada_b2 = ada_w.reshape(depth * 2, d, 3 * d), ada_b.reshape(depth * 2, 1, 3 * d)
    mods0, silu_c = _ada_modulation(c.reshape(d, 1), ada_w2, ada_b2, 1)
    mods = {0: mods0[0]}
    xs = x.reshape(s, d)
    pos_col = positions.reshape(s, 1)
    half = MLA_ROPE // 2
    inv_freq = ROPE_THETA ** (-jnp.arange(half, dtype=F32) / half)
    invf = jnp.tile(inv_freq, LANES // half).reshape(1, LANES)
    row = lambda v: v.reshape(1, -1)

    n_gla = 2 * key_dim + 2 * value_dim
    n_even = MLA_Q_RANK + MLA_KV_RANK + 4 * GDN_DIM
    ab_main_w, ab_small_w = _in_weight_layout(_even_in_layout_kernel, ab_w_in, n_even, 3 * LANES)
    gla_main_w, gla_small_w = _in_weight_layout(_gla_in_layout_kernel, gla_w_in, n_gla, LANES)
    assert ab_w_out.shape[0] == 2 and gla_w_out.shape[0] == 2
    assert depth % 2 == 0

    for layer in range(depth):
        i = layer // 2
        m = mods[2 * layer]
        shift, scale, gate = m[:, :d], m[:, d:2 * d], m[:, 2 * d:]
        gain = row(norm_g[layer, 0])
        if layer % 2 == 0:
            main, small = _norm_proj(xs, gain, scale, shift, ab_main_w, ab_small_w, i, tm=TM_PROJ, tn=TN_PROJ)
            wq, wkv = _mla_weights(mla_w_uq[i], mla_w_ukv[i])
            q, k, v = _mla_up(main, small, pos_col, invf, row(mla_q_norm[i]), row(mla_kv_norm[i]), wq, wkv,
                              tm=TM_MLA_UP)
            first = 2 * layer + 1
            side = [(ffn_w1, layer), (ffn_w3, layer), (ffn_w2, layer)]
            if layer == 0:
                side += [(ab_w_out, 0), (gla_w_out, 0)]
            o_a, w1, w3, w2, *rest = _causal_attention(q, k, v, side, 2, (silu_c, ada_w2, ada_b2, first, 4),
                                                       t=T_ATTN)
            later = rest[-1]
            if layer == 0:
                ab_out_w, gla_out_w = rest[0], rest[1]
            mods.update({first + n: later[n] for n in range(4)})
            lane_pad = lambda vec: jnp.pad(vec, (GDN_HEADS, LANES - 2 * GDN_HEADS)).reshape(1, LANES)
            qd, kd, u, w, a, gc = _gdn_prep(main, small, gdn_conv_w[i], lane_pad(gdn_a_log[i]),
                                            lane_pad(gdn_dt_bias[i]), tm=TM_GDN_PREP)
            o_b = _gdn_scan(qd, kd, u, w, a, gc, main, row(gdn_norm[i]), tm=TM_GDN_SCAN)
            xs = _out_proj(o_a, o_b, 0, 0, ab_out_w, i, xs, gate, tm=TM_OUT)
        else:
            main, small = _norm_proj(xs, gain, scale, shift, gla_main_w, gla_small_w, i, tm=TM_PROJ,
                                     tn=TN_PROJ)
            wg = jnp.pad(gla_w_gk2[i], ((0, LANES - GLA_GATE_RANK), (0, 0)))
            o_c = _gla(main, small, wg, row(gla_b_gk2[i]), row(gla_norm[i]), key_dim=key_dim, value_dim=value_dim,
                       tm=TM_GLA)
            xs = _out_proj(o_c, o_c, 0, 1, gla_out_w, i, xs, gate, tm=TM_OUT)
        m = mods[2 * layer + 1]
        shift, scale, gate = m[:, :d], m[:, d:2 * d], m[:, 2 * d:]
        xs = _ffn(xs, row(norm_g[layer, 1]), scale, shift, gate, row(final_norm), w1, w3, w2, layer % 2,
                  tm=TM_FFN, tf=TF_FFN,
                  final_norm=(layer == depth - 1))
    return xs.reshape(batch, s, d)
```
